```python
import jax, jax.numpy as jnp
from jax import lax
import numpy as np

D_MODEL = 1024
BATCH = 32
SEQ = 256
DEPTH = 2
DEC_BATCH = 8
DEC_SEQ = 1024
PAST_LEN = 256

GRID_W = 64
HEAD_DIM = 64
N_Q_HEADS = 4
N_KV_HEADS = 2
Q_PER_KV = N_Q_HEADS // N_KV_HEADS
WINDOW = 128
BLOCK = 128
ROPE_BASE = 10000.0
ATTN_WIDTH = N_Q_HEADS * HEAD_DIM
KV_WIDTH = N_KV_HEADS * HEAD_DIM
CONV_CH = 256
RWKV_HEADS = 4
RWKV_WIDTH = RWKV_HEADS * HEAD_DIM
DECAY_RANK = 32
ICLR_RANK = 32
GATE_RANK = 64
FOURIER_GROUPS = 4
FOURIER_WIDTH = FOURIER_GROUPS * HEAD_DIM
MIX_WIDTH = ATTN_WIDTH + CONV_CH + RWKV_WIDTH + FOURIER_WIDTH
IN_SPLITS = (ATTN_WIDTH, KV_WIDTH, KV_WIDTH,
             CONV_CH, CONV_CH, CONV_CH,
             RWKV_WIDTH, RWKV_WIDTH, RWKV_WIDTH, 2 * DECAY_RANK, 2 * ICLR_RANK, GATE_RANK,
             FOURIER_WIDTH)
IN_WIDTH = sum(IN_SPLITS)
D_FF = 2816
N_EXPERTS = 8
TOP_K = 2
D_FF_EXPERT = 1024
N_DENSE = (DEPTH + 1) // 2
N_MOE = DEPTH // 2
NORM_EPS = 1e-6
GN_EPS = 64e-5
NEG_INF = -1e30

kernel_name = 'hybrid_diffusion_step'


def rms_norm(x, w):
    xf = x.astype(jnp.float32)
    y = xf * lax.rsqrt(jnp.mean(xf * xf, axis=-1, keepdims=True) + NORM_EPS)
    return (y * w.astype(jnp.float32)).astype(x.dtype)


def modulate(h, shift, scale):
    return h * (1.0 + scale) + shift


def axial_rope_tables(n_tokens):
    n_rows = n_tokens // GRID_W
    row = jnp.repeat(jnp.arange(n_rows), GRID_W).astype(jnp.float32)
    col = jnp.tile(jnp.arange(GRID_W), n_rows).astype(jnp.float32)
    quarter = HEAD_DIM // 4
    inv_freq = ROPE_BASE ** (-jnp.arange(quarter, dtype=jnp.float32) / quarter)
    ang = jnp.stack([row[:, None] * inv_freq, col[:, None] * inv_freq], axis=1)
    return jnp.cos(ang), jnp.sin(ang)


def apply_rope(x, cos, sin):
    n_head_axes = x.ndim - 3
    shp = (cos.shape[0],) + (1,) * n_head_axes + cos.shape[1:]
    cos, sin = cos.reshape(shp), sin.reshape(shp)
    xr = x.reshape(x.shape[:-1] + (2, 2, HEAD_DIM // 4))
    x1, x2 = xr[..., 0, :], xr[..., 1, :]
    out = jnp.stack([x1 * cos - x2 * sin, x2 * cos + x1 * sin], axis=-2)
    return out.reshape(x.shape).astype(x.dtype)


def softmax_with_sink(logits, sink):
    s = sink.astype(jnp.float32).reshape((1, N_KV_HEADS, Q_PER_KV) + (1,) * (logits.ndim - 3))
    s = jnp.broadcast_to(s, logits.shape[:-1] + (1,))
    return jax.nn.softmax(jnp.concatenate([logits, s], axis=-1), axis=-1)[..., :-1]


def context_attention(q, k, v, sink):
    B, L = q.shape[:2]
    logits = jnp.einsum('bqhgd,bkhd->bhgqk', q, k).astype(jnp.float32) * HEAD_DIM ** -0.5
    p = softmax_with_sink(logits, sink)
    out = jnp.einsum('bhgqk,bkhd->bqhgd', p.astype(v.dtype), v)
    return out.reshape(B, L, ATTN_WIDTH)


def latent_attention(q, k, v, k_ctx, v_ctx, sink):
    B, T = q.shape[:2]
    nb = T // BLOCK
    qb = q.reshape(B, nb, BLOCK, N_KV_HEADS, Q_PER_KV, HEAD_DIM)

    def band(x):
        xp = jnp.pad(x, ((0, 0), (BLOCK, BLOCK), (0, 0), (0, 0))).reshape(B, nb + 2, BLOCK, N_KV_HEADS, HEAD_DIM)
        return jnp.concatenate([xp[:, :-2], xp[:, 1:-1], xp[:, 2:]], axis=2)

    kw, vw = band(k), band(v)
    q_pos = jnp.arange(T).reshape(nb, BLOCK)
    k_pos = (jnp.arange(nb)[:, None] - 1) * BLOCK + jnp.arange(3 * BLOCK)[None, :]
    kp = k_pos[:, None, :]
    valid = (jnp.abs(q_pos[:, :, None] - kp) <= WINDOW) & (kp >= 0) & (kp < T)
    scale = HEAD_DIM ** -0.5
    lw = jnp.einsum('bnqhgd,bnkhd->bhgnqk', qb, kw).astype(jnp.float32) * scale
    lw = jnp.where(valid, lw, NEG_INF)
    lc = jnp.einsum('bnqhgd,bchd->bhgnqc', qb, k_ctx).astype(jnp.float32) * scale
    p = softmax_with_sink(jnp.concatenate([lw, lc], axis=-1), sink)
    pw, pc = p[..., :3 * BLOCK], p[..., 3 * BLOCK:]
    out = (jnp.einsum('bhgnqk,bnkhd->bnqhgd', pw.astype(v.dtype), vw)
           + jnp.einsum('bhgnqc,bchd->bnqhgd', pc.astype(v_ctx.dtype), v_ctx))
    return out.reshape(B, T, ATTN_WIDTH)


def short_conv(x_in, b_gate, c_gate, conv_w):
    u = c_gate * x_in
    up = jnp.pad(u, ((0, 0), (1, 1), (0, 0)))
    y = conv_w[0] * up[:, :-2] + conv_w[1] * up[:, 1:-1] + conv_w[2] * up[:, 2:]
    return b_gate * y


def fourier_mix(u):
    B, T, _ = u.shape
    ug = u.reshape(B, T, FOURIER_GROUPS, HEAD_DIM).astype(jnp.float32)
    y = jnp.fft.fft2(ug, axes=(1, 3), norm='ortho').real
    return y.astype(u.dtype).reshape(B, T, FOURIER_WIDTH)


def wkv_scan(s0, r, w, k, v, z, b, reverse):
    xs = tuple(jnp.moveaxis(t, 1, 0) for t in (r, w, k, v, z, b))

    def step(S, inp):
        r_t, w_t, k_t, v_t, z_t, b_t = inp
        sz = jnp.einsum('bhvk,bhk->bhv', S, z_t)
        S = S * w_t[:, :, None, :] + sz[..., None] * b_t[:, :, None, :] + v_t[..., None] * k_t[:, :, None, :]
        return S, jnp.einsum('bhvk,bhk->bhv', S, r_t)

    s_fin, ys = lax.scan(step, s0, xs, reverse=reverse)
    return jnp.moveaxis(ys, 0, 1), s_fin


def rwkv_mixer(r, k, v, wd, ad, gd, p, s0):
    B, T, _ = r.shape
    f32 = jnp.float32

    def heads(t):
        return t.reshape(t.shape[:-1] + (RWKV_HEADS, HEAD_DIM))

    dec = p['w0'] + jnp.einsum('btjr,jrc->btjc', jnp.tanh(wd.reshape(B, T, 2, DECAY_RANK)), p['w_up'])
    w = jnp.exp(-jnp.exp(-jax.nn.softplus(-dec.astype(f32)) - 0.5))
    a = jax.nn.sigmoid((p['a0'] + jnp.einsum('btjr,jrc->btjc', ad.reshape(B, T, 2, ICLR_RANK), p['a_up'])).astype(f32))
    g = jnp.einsum('btr,rc->btc', jax.nn.sigmoid(gd), p['g_up']).astype(f32)
    rf, kf, vf = r.astype(f32), k.astype(f32), v.astype(f32)
    kk = heads(kf * p['k_k'].astype(f32))
    kk = kk / jnp.maximum(jnp.sqrt(jnp.sum(kk * kk, axis=-1, keepdims=True)), 1e-12)
    k_dir = kf[:, :, None, :] * (1.0 + (a - 1.0) * p['k_a'].astype(f32))
    b_dir = kk.reshape(B, T, 1, RWKV_WIDTH) * a
    s0 = s0.astype(f32)
    rh, vh = heads(rf), heads(vf)
    y_f, s_f = wkv_scan(s0[:, 0], rh, heads(w[:, :, 0]), heads(k_dir[:, :, 0]), vh, -kk, heads(b_dir[:, :, 0]), False)
    y_b, s_b = wkv_scan(s0[:, 1], rh, heads(w[:, :, 1]), heads(k_dir[:, :, 1]), vh, -kk, heads(b_dir[:, :, 1]), True)
    y = y_f + y_b
    mu = jnp.mean(y, axis=-1, keepdims=True)
    var = jnp.mean(jnp.square(y - mu), axis=-1, keepdims=True)
    y = (y - mu) * lax.rsqrt(var + GN_EPS) * heads(p['ln_w'].astype(f32)) + heads(p['ln_b'].astype(f32))
    bonus = jnp.sum(rh * heads(kf) * p['r_k'].astype(f32), axis=-1, keepdims=True) * vh
    out = ((y + bonus).reshape(B, T, RWKV_WIDTH) * g).astype(r.dtype)
    return out, jnp.stack([s_f, s_b], axis=1)


def token_mixers(h, p, rope=None, ctx=None):
    B, T, _ = h.shape
    proj = jnp.einsum('btd,dn->btn', h, p['w_in'])
    offsets = [int(i) for i in np.cumsum(IN_SPLITS)[:-1]]
    (aq, ak, av, bx, bb, bc, cr, ck, cv, cwd, cad, cgd, dx) = jnp.split(proj, offsets, axis=-1)
    q = rms_norm(aq.reshape(B, T, N_KV_HEADS, Q_PER_KV, HEAD_DIM), p['q_norm'])
    k = rms_norm(ak.reshape(B, T, N_KV_HEADS, HEAD_DIM), p['k_norm'])
    v = av.reshape(B, T, N_KV_HEADS, HEAD_DIM)
    if ctx is None:
        attn = context_attention(q, k, v, p['sink'])
        s0 = jnp.zeros((B, 2, RWKV_HEADS, HEAD_DIM, HEAD_DIM), jnp.float32)
    else:
        k_ctx, v_ctx, s0 = ctx
        cos, sin = rope
        attn = latent_attention(apply_rope(q, cos, sin), apply_rope(k, cos, sin), v, k_ctx, v_ctx, p['sink'])
    conv = short_conv(bx, bb, bc, p['conv_w'])
    rw, s_fin = rwkv_mixer(cr, ck, cv, cwd, cad, cgd, p, s0)
    four = fourier_mix(dx)
    out = jnp.einsum('btc,cd->btd', jnp.concatenate([attn, conv, rw, four], axis=-1), p['w_out'])
    if ctx is None:
        return out, (k, v, s_fin)
    return out, None


def swiglu(h, w1, w3, w2):
    return jnp.einsum('btf,fd->btd', jax.nn.silu(jnp.einsum('btd,df->btf', h, w1)) * jnp.einsum('btd,df->btf', h, w3), w2)


def moe_swiglu(h, router_w, w1, w3, w2):
    logits = jnp.einsum('btd,de->bte', h, router_w).astype(jnp.float32)
    probs = jax.nn.softmax(logits, axis=-1)
    top_p, top_i = lax.top_k(probs, TOP_K)
    top_p = top_p / jnp.sum(top_p, axis=-1, keepdims=True)
    gates = jnp.sum(jax.nn.one_hot(top_i, N_EXPERTS, dtype=jnp.float32) * top_p[..., None], axis=-2)
    gates = gates.astype(h.dtype)
    y = jnp.zeros_like(h)
    for e in range(N_EXPERTS):
        y = y + gates[..., e:e + 1] * swiglu(h, w1[e], w3[e], w2[e])
    return y


def setup_inputs(seed: int = 0) -> dict:
    key = jax.random.key(seed)
    ks = iter(jax.random.split(key, 48))
    D = D_MODEL

    def nrm(shape, scale=1.0):
        return jax.random.normal(next(ks), shape, jnp.float32) * scale

    return {
        'x_prompt': nrm((BATCH, SEQ, D)),
        'x_sample': nrm((DEC_BATCH, DEC_SEQ, D)),
        'cache_k': nrm((DEC_BATCH, DEPTH, PAST_LEN, N_KV_HEADS, HEAD_DIM)),
        'cache_v': nrm((DEC_BATCH, DEPTH, PAST_LEN, N_KV_HEADS, HEAD_DIM)),
        'state_wkv': nrm((DEC_BATCH, DEPTH, 2, RWKV_HEADS, HEAD_DIM, HEAD_DIM), 0.3),
        'c': nrm((DEC_BATCH, D)),
        'c_ctx': nrm((D,)),
        'mod_w': nrm((DEPTH, D, 6 * D), 0.5 * D ** -0.5),
        'mod_b': nrm((DEPTH, 6 * D), 0.02),
        'norm1_w': 1.0 + nrm((DEPTH, D), 0.05),
        'norm2_w': 1.0 + nrm((DEPTH, D), 0.05),
        'w_in': nrm((DEPTH, D, IN_WIDTH), D ** -0.5),
        'q_norm_w': 1.0 + nrm((DEPTH, HEAD_DIM), 0.05),
        'k_norm_w': 1.0 + nrm((DEPTH, HEAD_DIM), 0.05),
        'attn_sink': nrm((DEPTH, N_Q_HEADS), 0.5),
        'conv_w': nrm((DEPTH, 3, CONV_CH), 3 ** -0.5),
        'rwkv_w0': nrm((DEPTH, 2, RWKV_WIDTH)),
        'rwkv_w_up': nrm((DEPTH, 2, DECAY_RANK, RWKV_WIDTH), 0.5 * DECAY_RANK ** -0.5),
        'rwkv_a0': nrm((DEPTH, 2, RWKV_WIDTH), 0.5),
        'rwkv_a_up': nrm((DEPTH, 2, ICLR_RANK, RWKV_WIDTH), 0.5 * ICLR_RANK ** -0.5),
        'rwkv_g_up': nrm((DEPTH, GATE_RANK, RWKV_WIDTH), GATE_RANK ** -0.5),
        'rwkv_k_k': 0.85 + nrm((DEPTH, RWKV_WIDTH), 0.05),
        'rwkv_k_a': 1.0 + nrm((DEPTH, RWKV_WIDTH), 0.05),
        'rwkv_r_k': nrm((DEPTH, RWKV_HEADS, HEAD_DIM), 0.1),
        'rwkv_ln_w': 1.0 + nrm((DEPTH, RWKV_WIDTH), 0.05),
        'rwkv_ln_b': nrm((DEPTH, RWKV_WIDTH), 0.02),
        'w_out': nrm((DEPTH, MIX_WIDTH, D), MIX_WIDTH ** -0.5),
        'ffn_w1': nrm((N_DENSE, D, D_FF), D ** -0.5),
        'ffn_w3': nrm((N_DENSE, D, D_FF), D ** -0.5),
        'ffn_w2': nrm((N_DENSE, D_FF, D), D_FF ** -0.5),
        'router_w': nrm((N_MOE, D, N_EXPERTS), D ** -0.5),
        'moe_w1': nrm((N_MOE, N_EXPERTS, D, D_FF_EXPERT), D ** -0.5),
        'moe_w3': nrm((N_MOE, N_EXPERTS, D, D_FF_EXPERT), D ** -0.5),
        'moe_w2': nrm((N_MOE, N_EXPERTS, D_FF_EXPERT, D), D_FF_EXPERT ** -0.5),
    }


def reference(x_prompt, x_sample, cache_k, cache_v, state_wkv, c, c_ctx, mod_w, mod_b, norm1_w, norm2_w,
              w_in, q_norm_w, k_norm_w, attn_sink, conv_w, rwkv_w0, rwkv_w_up, rwkv_a0, rwkv_a_up, rwkv_g_up,
              rwkv_k_k, rwkv_k_a, rwkv_r_k, rwkv_ln_w, rwkv_ln_b, w_out, ffn_w1, ffn_w3, ffn_w2,
              router_w, moe_w1, moe_w3, moe_w2):
    rope = axial_rope_tables(x_sample.shape[1])
    xp, xs = x_prompt, x_sample
    new_k, new_v, new_s = [], [], []
    for l in range(DEPTH):
        p = {'w_in': w_in[l], 'q_norm': q_norm_w[l], 'k_norm': k_norm_w[l], 'sink': attn_sink[l],
             'conv_w': conv_w[l], 'w0': rwkv_w0[l], 'w_up': rwkv_w_up[l], 'a0': rwkv_a0[l],
             'a_up': rwkv_a_up[l], 'g_up': rwkv_g_up[l], 'k_k': rwkv_k_k[l], 'k_a': rwkv_k_a[l],
             'r_k': rwkv_r_k[l], 'ln_w': rwkv_ln_w[l], 'ln_b': rwkv_ln_b[l], 'w_out': w_out[l]}
        mp = jnp.split(jax.nn.silu(c_ctx) @ mod_w[l] + mod_b[l], 6, axis=-1)
        ms = jnp.split((jax.nn.silu(c) @ mod_w[l] + mod_b[l])[:, None, :], 6, axis=-1)
        out_p, (kc, vc, sc) = token_mixers(modulate(rms_norm(xp, norm1_w[l]), mp[0], mp[1]), p)
        xp = xp + mp[2] * out_p
        out_s, _ = token_mixers(modulate(rms_norm(xs, norm1_w[l]), ms[0], ms[1]), p, rope=rope,
                                ctx=(cache_k[:, l], cache_v[:, l], state_wkv[:, l]))
        xs = xs + ms[2] * out_s
        new_k.append(kc)
        new_v.append(vc)
        new_s.append(sc.astype(x_prompt.dtype))
        hp = modulate(rms_norm(xp, norm2_w[l]), mp[3], mp[4])
        hs = modulate(rms_norm(xs, norm2_w[l]), ms[3], ms[4])
        j = l // 2
        if l % 2 == 0:
            fp = swiglu(hp, ffn_w1[j], ffn_w3[j], ffn_w2[j])
            fs = swiglu(hs, ffn_w1[j], ffn_w3[j], ffn_w2[j])
        else:
            fp = moe_swiglu(hp, router_w[j], moe_w1[j], moe_w3[j], moe_w2[j])
            fs = moe_swiglu(hs, router_w[j], moe_w1[j], moe_w3[j], moe_w2[j])
        xp = xp + mp[5] * fp
        xs = xs + ms[5] * fs
    return (xp, xs, jnp.stack(new_k, axis=1), jnp.stack(new_v, axis=1), jnp.stack(new_s, axis=1))
```

```python
import functools
import math

import jax
import jax.numpy as jnp
from jax import lax
from jax.experimental import pallas as pl
from jax.experimental.pallas import tpu as pltpu

F32 = jnp.float32
BF16 = jnp.bfloat16
HIGHEST = lax.Precision.HIGHEST

D_MODEL = 1024
DEPTH = 2
GRID_W = 64
HEAD_DIM = 64
N_Q_HEADS = 4
N_KV_HEADS = 2
WINDOW = 128
BLOCK = 128
ROPE_BASE = 10000.0
ATTN_WIDTH = N_Q_HEADS * HEAD_DIM
KV_WIDTH = N_KV_HEADS * HEAD_DIM
CONV_CH = 256
RWKV_HEADS = 4
RWKV_WIDTH = RWKV_HEADS * HEAD_DIM
DECAY_RANK = 32
ICLR_RANK = 32
GATE_RANK = 64
FOURIER_GROUPS = 4
FOURIER_WIDTH = FOURIER_GROUPS * HEAD_DIM
MIX_WIDTH = ATTN_WIDTH + CONV_CH + RWKV_WIDTH + FOURIER_WIDTH
D_FF = 2816
N_EXPERTS = 8
D_FF_EXPERT = 1024
NORM_EPS = 1e-6
GN_EPS = 64e-5
NEG_INF = -1e30

LANES = 128
VMEM_LIMIT_BYTES = 56 * 1024 * 1024

PROJ_WIDTH = 2560
COLBLK = 256
LOWRANK_WIDTH = 2 * DECAY_RANK + 2 * ICLR_RANK + GATE_RANK
MOD_ROWS = 16
TOKEN_TILE = 512
RWKV_CHUNK = 64
RWKV_PREP_ROWS = 256
FF_TILE = 1408
DECAY_SCALE = -math.exp(-0.5)


def _params(*sem):
    return pltpu.CompilerParams(dimension_semantics=sem, vmem_limit_bytes=VMEM_LIMIT_BYTES)


def _mm(a, b):
    return jnp.dot(a.astype(BF16), b.astype(BF16), preferred_element_type=F32)


def _mm_nt(a, b):
    return lax.dot_general(a.astype(BF16), b.astype(BF16), (((1,), (1,)), ((), ())), preferred_element_type=F32)


def _mm_tn(a, b):
    return jnp.dot(a.T.astype(BF16), b.astype(BF16), preferred_element_type=F32)


def _mm_hi(a, b):
    return jnp.dot(a, b, precision=HIGHEST, preferred_element_type=F32)


def _silu(x):
    return x * jax.nn.sigmoid(x)


def _iota(shape, dim):
    return lax.broadcasted_iota(jnp.int32, shape, dim)


def _mod_kernel(c_ref, w_ref, b_ref, o_ref):
    o_ref[0] = _mm(_silu(c_ref[...]), w_ref[0]) + b_ref[0]


def _modulation(cond, mod_w, mod_b):
    n_chunks = mod_w.shape[-1] // D_MODEL
    return pl.pallas_call(
        _mod_kernel,
        grid=(DEPTH, n_chunks),
        in_specs=[
            pl.BlockSpec((MOD_ROWS, D_MODEL), lambda l, j: (0, 0)),
            pl.BlockSpec((1, D_MODEL, D_MODEL), lambda l, j: (l, 0, j)),
            pl.BlockSpec((1, 1, D_MODEL), lambda l, j: (l, 0, j)),
        ],
        out_specs=pl.BlockSpec((1, MOD_ROWS, D_MODEL), lambda l, j: (l, 0, j)),
        out_shape=jax.ShapeDtypeStruct((DEPTH, MOD_ROWS, n_chunks * D_MODEL), F32),
        compiler_params=_params("parallel", "parallel"),
        name="modulation",
    )(cond, mod_w, mod_b.reshape(DEPTH, 1, -1))


def _mod_spec(chunk, row_of_tile):
    return pl.BlockSpec((None, None, 1, D_MODEL), lambda i, *_: (row_of_tile(i), chunk, 0, 0))


def _rms(x, w):
    return x * lax.rsqrt(jnp.mean(x * x, axis=-1, keepdims=True) + NORM_EPS) * w


def _in_proj_kernel(x_ref, nw_ref, sh_ref, sc_ref, w_ref, o_ref):
    h = _rms(x_ref[...], nw_ref[...]) * (1.0 + sc_ref[...]) + sh_ref[...]
    o_ref[...] = jnp.dot(h.astype(BF16), w_ref[...], preferred_element_type=F32)


def _in_proj(x, norm_w, mod, row_of_tile, w_in):
    n_tok = x.shape[0]
    return pl.pallas_call(
        _in_proj_kernel,
        grid=(n_tok // TOKEN_TILE,),
        in_specs=[
            pl.BlockSpec((TOKEN_TILE, D_MODEL), lambda i: (i, 0)),
            pl.BlockSpec((1, D_MODEL), lambda i: (0, 0)),
            _mod_spec(0, row_of_tile),
            _mod_spec(1, row_of_tile),
            pl.BlockSpec((D_MODEL, PROJ_WIDTH), lambda i: (0, 0)),
        ],
        out_specs=pl.BlockSpec((TOKEN_TILE, PROJ_WIDTH), lambda i: (i, 0)),
        out_shape=jax.ShapeDtypeStruct((n_tok, PROJ_WIDTH), F32),
        compiler_params=_params("parallel"),
        name="in_proj",
    )(x, norm_w, mod, mod, w_in)


def _group_sum_matrix(width):
    r = _iota((width, width), 0) // HEAD_DIM
    c = _iota((width, width), 1) // HEAD_DIM
    return jnp.where(r == c, 1.0, 0.0).astype(F32)


def _head_rms(x, w):
    ms = _mm_hi(x * x, _group_sum_matrix(x.shape[-1])) * (1.0 / HEAD_DIM)
    return x * lax.rsqrt(ms + NORM_EPS) * w


def _dup_kv_head(x, g):
    lane = _iota((1, KV_WIDTH), 1)
    rolled = pltpu.roll(x, HEAD_DIM, 1)
    first = lane < HEAD_DIM
    return jnp.where(first, x, rolled) if g == 0 else jnp.where(first, rolled, x)


def _half_mask(j):
    lane = _iota((1, KV_WIDTH), 1)
    return (lane >= j * HEAD_DIM) & (lane < (j + 1) * HEAD_DIM)


def _ctx_attn_kernel(qkv_ref, qw_ref, kw_ref, sink_ref, o_ref, k_ref, v_ref):
    qkv = qkv_ref[...]
    q = _head_rms(qkv[:, :ATTN_WIDTH], qw_ref[...]) * HEAD_DIM ** -0.5
    k = _head_rms(qkv[:, ATTN_WIDTH:ATTN_WIDTH + KV_WIDTH], kw_ref[...])
    v = qkv[:, ATTN_WIDTH + KV_WIDTH:]
    k_ref[...] = k
    v_ref[...] = v
    for g in range(N_KV_HEADS):
        qg = q[:, g * KV_WIDTH:(g + 1) * KV_WIDTH]
        kg = _dup_kv_head(k, g).astype(BF16)
        vg = _dup_kv_head(v, g)
        acc = jnp.zeros(qg.shape, F32)
        for j in range(2):
            mj = _half_mask(j)
            s = sink_ref[2 * g + j]
            logits = _mm_nt(jnp.where(mj, qg, 0.0), kg)
            m = jnp.maximum(jnp.max(logits, axis=-1, keepdims=True), s)
            e = jnp.exp(logits - m)
            den = jnp.sum(e, axis=-1, keepdims=True) + jnp.exp(s - m)
            acc = acc + _mm(e, jnp.where(mj, vg, 0.0)) / den
        o_ref[:, g * KV_WIDTH:(g + 1) * KV_WIDTH] = acc


def _ctx_attention(proj, q_norm, k_norm, sink, batch, seq):
    blk = ATTN_WIDTH + 2 * KV_WIDTH
    return pl.pallas_call(
        _ctx_attn_kernel,
        grid=(batch,),
        in_specs=[
            pl.BlockSpec((seq, blk), lambda b: (b, 0)),
            pl.BlockSpec((1, ATTN_WIDTH), lambda b: (0, 0)),
            pl.BlockSpec((1, KV_WIDTH), lambda b: (0, 0)),
            pl.BlockSpec(memory_space=pltpu.SMEM),
        ],
        out_specs=[
            pl.BlockSpec((seq, ATTN_WIDTH), lambda b: (b, 0)),
            pl.BlockSpec((seq, KV_WIDTH), lambda b: (b, 0)),
            pl.BlockSpec((seq, KV_WIDTH), lambda b: (b, 0)),
        ],
        out_shape=[
            jax.ShapeDtypeStruct((batch * seq, ATTN_WIDTH), F32),
            jax.ShapeDtypeStruct((batch * seq, KV_WIDTH), F32),
            jax.ShapeDtypeStruct((batch * seq, KV_WIDTH), F32),
        ],
        compiler_params=_params("parallel"),
        name="ctx_attention",
    )(proj, q_norm, k_norm, sink)


def _rope(x, cos, sin_signed):
    width = x.shape[-1]
    lane = _iota((1, width), 1)
    first_half = (lane % (HEAD_DIM // 2)) < (HEAD_DIM // 4)
    partner = jnp.where(first_half, pltpu.roll(x, width - HEAD_DIM // 4, 1), pltpu.roll(x, HEAD_DIM // 4, 1))
    return x * cos + partner * sin_signed


def _lat_attn_kernel(seq, past, qkv_ref, kc_ref, vc_ref, qw_ref, kw_ref, cosq_ref, snq_ref, cosk_ref, snk_ref,
                     sink_ref, o_ref, q_s, k_s, v_s):
    qkv = qkv_ref[...]
    q = _rope(_head_rms(qkv[:, :ATTN_WIDTH], qw_ref[...]), cosq_ref[...], snq_ref[...]) * HEAD_DIM ** -0.5
    k = _rope(_head_rms(qkv[:, ATTN_WIDTH:ATTN_WIDTH + KV_WIDTH], kw_ref[...]), cosk_ref[...], snk_ref[...])
    v = qkv[:, ATTN_WIDTH + KV_WIDTH:]
    kc = kc_ref[...]
    vc = vc_ref[...]
    q_s[...] = q
    zero_blk = jnp.zeros((BLOCK, KV_WIDTH), BF16)
    for g in range(N_KV_HEADS):
        k_s[g, 0:BLOCK] = zero_blk
        k_s[g, BLOCK:BLOCK + seq] = _dup_kv_head(k, g).astype(BF16)
        k_s[g, BLOCK + seq:2 * BLOCK + seq] = zero_blk
        k_s[g, 2 * BLOCK + seq:] = _dup_kv_head(kc, g).astype(BF16)
        vg = _dup_kv_head(v, g)
        vcg = _dup_kv_head(vc, g)
        for j in range(2):
            mj = _half_mask(j)
            v_s[2 * g + j, 0:BLOCK] = zero_blk
            v_s[2 * g + j, BLOCK:BLOCK + seq] = jnp.where(mj, vg, 0.0).astype(BF16)
            v_s[2 * g + j, BLOCK + seq:2 * BLOCK + seq] = zero_blk
            v_s[2 * g + j, 2 * BLOCK + seq:] = jnp.where(mj, vcg, 0.0).astype(BF16)

    win = 3 * BLOCK
    ctx0 = 2 * BLOCK + seq

    def q_block(n, carry):
        q0 = pl.multiple_of(n * BLOCK, BLOCK)
        qpos = n * BLOCK + _iota((BLOCK, win), 0)
        kpos = (n - 1) * BLOCK + _iota((BLOCK, win), 1)
        valid = (jnp.abs(qpos - kpos) <= WINDOW) & (kpos >= 0) & (kpos < seq)
        for g in range(N_KV_HEADS):
            qg = q_s[pl.ds(q0, BLOCK), g * KV_WIDTH:(g + 1) * KV_WIDTH]
            kw = k_s[g, pl.ds(q0, win), :]
            kcx = k_s[g, ctx0:ctx0 + past, :]
            acc = jnp.zeros((BLOCK, KV_WIDTH), F32)
            for j in range(2):
                h = 2 * g + j
                s = sink_ref[h]
                qm = jnp.where(_half_mask(j), qg, 0.0)
                lw = jnp.where(valid, _mm_nt(qm, kw), NEG_INF)
                lc = _mm_nt(qm, kcx)
                m = jnp.maximum(jnp.maximum(jnp.max(lw, axis=-1, keepdims=True),
                                            jnp.max(lc, axis=-1, keepdims=True)), s)
                ew = jnp.exp(lw - m)
                ec = jnp.exp(lc - m)
                den = jnp.sum(ew, axis=-1, keepdims=True) + jnp.sum(ec, axis=-1, keepdims=True) + jnp.exp(s - m)
                pv = (jnp.dot(ew.astype(BF16), v_s[h, pl.ds(q0, win), :], preferred_element_type=F32)
                      + jnp.dot(ec.astype(BF16), v_s[h, ctx0:ctx0 + past, :], preferred_element_type=F32))
                acc = acc + pv / den
            o_ref[pl.ds(q0, BLOCK), g * KV_WIDTH:(g + 1) * KV_WIDTH] = acc
        return carry

    lax.fori_loop(0, seq // BLOCK, q_block, 0)


def _lat_attention(proj, tok_block0, cache_k, cache_v, q_norm, k_norm, rope_q, rope_k, sink, batch, seq, past):
    blk = ATTN_WIDTH + 2 * KV_WIDTH
    cos_q, sin_q = rope_q
    cos_k, sin_k = rope_k
    rows = 2 * BLOCK + seq + past
    const = lambda shape: pl.BlockSpec(shape, lambda b: (0, 0))
    return pl.pallas_call(
        functools.partial(_lat_attn_kernel, seq, past),
        grid=(batch,),
        in_specs=[
            pl.BlockSpec((seq, blk), lambda b: (tok_block0 + b, 0)),
            pl.BlockSpec((None, past, KV_WIDTH), lambda b: (b, 0, 0)),
            pl.BlockSpec((None, past, KV_WIDTH), lambda b: (b, 0, 0)),
            const((1, ATTN_WIDTH)),
            const((1, KV_WIDTH)),
            const((seq, ATTN_WIDTH)),
            const((seq, ATTN_WIDTH)),
            const((seq, KV_WIDTH)),
            const((seq, KV_WIDTH)),
            pl.BlockSpec(memory_space=pltpu.SMEM),
        ],
        out_specs=pl.BlockSpec((seq, ATTN_WIDTH), lambda b: (b, 0)),
        out_shape=jax.ShapeDtypeStruct((batch * seq, ATTN_WIDTH), F32),
        scratch_shapes=[
            pltpu.VMEM((seq, ATTN_WIDTH), F32),
            pltpu.VMEM((N_KV_HEADS, rows, KV_WIDTH), BF16),
            pltpu.VMEM((N_Q_HEADS, rows, KV_WIDTH), BF16),
        ],
        compiler_params=_params("parallel"),
        name="latent_attention",
    )(proj, cache_k, cache_v, q_norm, k_norm, cos_q, sin_q, cos_k, sin_k, sink)


def _convfour_kernel(seq, bx_ref, bb_ref, bc_ref, dx_ref, cw_ref, ct_ref, st_ref, cc_ref, sc_ref, conv_ref, four_ref):
    u = bc_ref[...] * bx_ref[...]
    row = _iota((seq, 1), 0)
    prev = jnp.where(row == 0, 0.0, pltpu.roll(u, 1, 0))
    nxt = jnp.where(row == seq - 1, 0.0, pltpu.roll(u, seq - 1, 0))
    cw = cw_ref[...]
    conv_ref[...] = bb_ref[...] * (cw[0:1] * prev + cw[1:2] * u + cw[2:3] * nxt)
    x = dx_ref[...].astype(BF16)
    a = jnp.dot(ct_ref[...], x, preferred_element_type=F32)
    b = jnp.dot(st_ref[...], x, preferred_element_type=F32)
    y = (jnp.dot(a.astype(BF16), cc_ref[...], preferred_element_type=F32)
         - jnp.dot(b.astype(BF16), sc_ref[...], preferred_element_type=F32))
    four_ref[...] = y * (1.0 / math.sqrt(seq * HEAD_DIM))


def _dft_tables(n):
    idx = jnp.arange(n, dtype=jnp.int32)
    ang = ((idx[:, None] * idx[None, :]) % n).astype(F32) * (2.0 * math.pi / n)
    return jnp.cos(ang), jnp.sin(ang)


def _convfour(proj, tok_block0, conv_w, batch, seq):
    ct, st = _dft_tables(seq)
    c64, s64 = _dft_tables(HEAD_DIM)
    eye = jnp.eye(FOURIER_GROUPS, dtype=F32)
    cc, sc = jnp.kron(eye, c64), jnp.kron(eye, s64)
    col = lambda c: pl.BlockSpec((seq, COLBLK), lambda b: (tok_block0 + b, c))
    const = lambda shape: pl.BlockSpec(shape, lambda b: (0, 0))
    out = pl.BlockSpec((seq, COLBLK), lambda b: (b, 0))
    return pl.pallas_call(
        functools.partial(_convfour_kernel, seq),
        grid=(batch,),
        in_specs=[col(2), col(3), col(4), col(9), const((8, CONV_CH)), const((seq, seq)), const((seq, seq)),
                  const((FOURIER_WIDTH, FOURIER_WIDTH)), const((FOURIER_WIDTH, FOURIER_WIDTH))],
        out_specs=[out, out],
        out_shape=[jax.ShapeDtypeStruct((batch * seq, COLBLK), F32)] * 2,
        compiler_params=_params("parallel"),
        name="conv_fourier",
    )(proj, proj, proj, proj, conv_w, ct.astype(BF16), st.astype(BF16), cc.astype(BF16), sc.astype(BF16))


def _rwkv_kernel(latent, seq, *refs):
    if latent:
        (r_ref, k_ref, v_ref, lr_ref, lrw_ref, bias_ref, kk_ref, ka_ref, rk_ref, lnw_ref, lnb_ref, s0_ref,
         o_ref, logw_s, kd_s, b_s, z_s, g_s, y_s, st_s) = refs
    else:
        (r_ref, k_ref, v_ref, lr_ref, lrw_ref, bias_ref, kk_ref, ka_ref, rk_ref, lnw_ref, lnb_ref,
         o_ref, sfin_ref, logw_s, kd_s, b_s, z_s, g_s, y_s, st_s) = refs
    W = RWKV_WIDTH
    C = RWKV_CHUNK
    gsum = _group_sum_matrix(W)
    lane = _iota((1, W), 1)

    def prep(i, carry):
        rows = pl.ds(pl.multiple_of(i * RWKV_PREP_ROWS, RWKV_PREP_ROWS), RWKV_PREP_ROWS)
        lr = lr_ref[rows, :]
        f = jnp.where(lane < 2 * DECAY_RANK, jnp.tanh(lr),
                      jnp.where(lane < 2 * DECAY_RANK + 2 * ICLR_RANK, lr, jax.nn.sigmoid(lr)))
        low = jnp.dot(f.astype(BF16), lrw_ref[...], preferred_element_type=F32) + bias_ref[...]
        logw_s[rows, :] = DECAY_SCALE * jax.nn.sigmoid(low[:, :2 * W])
        a = jax.nn.sigmoid(low[:, 2 * W:4 * W])
        g_s[rows, :] = low[:, 4 * W:]
        k = k_ref[rows, :]
        kk = k * kk_ref[...]
        kk = kk / jnp.maximum(jnp.sqrt(_mm_hi(kk * kk, gsum)), 1e-12)
        z_s[rows, :] = -kk
        for j in range(2):
            aj = a[:, j * W:(j + 1) * W]
            kd_s[rows, j * W:(j + 1) * W] = k * (1.0 + (aj - 1.0) * ka_ref[...])
            b_s[rows, j * W:(j + 1) * W] = kk * aj
        return carry

    lax.fori_loop(0, seq // RWKV_PREP_ROWS, prep, 0)

    y_s[...] = jnp.zeros((seq, W), F32)
    if latent:
        st_s[...] = s0_ref[...]
    else:
        st_s[...] = jnp.zeros((2, W, W), F32)

    ti = _iota((C, C), 0)
    si = _iota((C, C), 1)
    eye_c = jnp.where(ti == si, 1.0, 0.0).astype(F32)
    bd = (_iota((W, W), 0) // HEAD_DIM) == (_iota((W, W), 1) // HEAD_DIM)
    eye_w = _iota((W, W), 0) == _iota((W, W), 1)
    n_chunks = seq // C

    def chunk(j, c):
        rows = pl.ds(pl.multiple_of(c * C, C), C)
        cols = slice(j * W, (j + 1) * W)
        rc, vc, zc = r_ref[rows, :], v_ref[rows, :], z_s[rows, :]
        lw, kd, bc = logw_s[rows, cols], kd_s[rows, cols], b_s[rows, cols]
        if j == 0:
            strict, incl = si < ti, si <= ti
        else:
            strict, incl = si > ti, si >= ti
        cl = _mm_hi(jnp.where(incl, 1.0, 0.0).astype(F32), lw)
        tot = cl[C - 1:C, :] if j == 0 else cl[0:1, :]
        w_in, w_ex, w_inv, w_rest = jnp.exp(cl), jnp.exp(cl - lw), jnp.exp(-cl), jnp.exp(tot - cl)
        rt, zt = rc * w_in, zc * w_ex
        bk = jnp.concatenate([bc * w_inv, kd * w_inv], axis=0).astype(BF16)
        tz = jnp.zeros((C, W), F32)
        u0 = jnp.zeros((C, W), F32)
        rz = rt
        y0 = jnp.zeros((C, W), F32)
        for h in range(RWKV_HEADS):
            mh = (lane >= h * HEAD_DIM) & (lane < (h + 1) * HEAD_DIM)
            zh = jnp.where(mh, zt, 0.0)
            aa = _mm_nt(jnp.concatenate([zh, jnp.where(mh, rt, 0.0)], axis=0), bk)
            a_zb = jnp.where(strict, aa[:C, :C], 0.0)
            a_zk = jnp.where(strict, aa[:C, C:], 0.0)
            a_rb = jnp.where(incl, aa[C:, :C], 0.0)
            a_rk = jnp.where(incl, aa[C:, C:], 0.0)
            tinv = eye_c + a_zb
            ak = a_zb
            for _ in range(int(math.log2(C)) - 1):
                ak = _mm(ak, ak)
                tinv = tinv + _mm(tinv, ak)
            av = jnp.where(mh, _mm(a_zk, vc), 0.0)
            x = _mm(tinv, jnp.concatenate([zh, av], axis=1))
            tz = tz + x[:, :W]
            u0 = u0 + x[:, W:]
            ry = _mm(a_rb, x)
            rz = rz + ry[:, :W]
            y0 = y0 + ry[:, W:] + _mm(a_rk, jnp.where(mh, vc, 0.0))
        s_prev = st_s[j]
        y_s[rows, :] += _mm(rz, s_prev) + y0
        bh = bc * w_rest
        bt = _mm_tn(bh, jnp.concatenate([tz, u0], axis=1))
        mt = jnp.where(bd, bt[:, :W], 0.0) + jnp.where(eye_w, jnp.exp(tot), 0.0)
        nt = jnp.where(bd, bt[:, W:] + _mm_tn(kd * w_rest, vc), 0.0)
        st_s[j] = _mm(mt, s_prev) + nt

    def step(i, carry):
        chunk(0, i)
        chunk(1, n_chunks - 1 - i)
        return carry

    lax.fori_loop(0, n_chunks, step, 0)

    if not latent:
        sfin_ref[...] = st_s[...]

    def finish(i, carry):
        rows = pl.ds(pl.multiple_of(i * RWKV_PREP_ROWS, RWKV_PREP_ROWS), RWKV_PREP_ROWS)
        y = y_s[rows, :]
        mu = _mm_hi(y, gsum) * (1.0 / HEAD_DIM)
        d = y - mu
        var = _mm_hi(d * d, gsum) * (1.0 / HEAD_DIM)
        yn = d * lax.rsqrt(var + GN_EPS) * lnw_ref[...] + lnb_ref[...]
        r, k, v = r_ref[rows, :], k_ref[rows, :], v_ref[rows, :]
        bonus = _mm_hi(r * k * rk_ref[...], gsum) * v
        o_ref[rows, :] = (yn + bonus) * g_s[rows, :]
        return carry

    lax.fori_loop(0, seq // RWKV_PREP_ROWS, finish, 0)


def _rwkv(proj, tok_block0, p, batch, seq, s0=None):
    latent = s0 is not None
    W = RWKV_WIDTH
    col = lambda c: pl.BlockSpec((seq, COLBLK), lambda b: (tok_block0 + b, c))
    const = lambda shape: pl.BlockSpec(shape, lambda b: (0,) * len(shape))
    in_specs = [col(5), col(6), col(7), col(8), const((COLBLK, 5 * W)), const((1, 5 * W))] + [const((1, W))] * 5
    args = [proj, proj, proj, proj, p["lrw"], p["lr_bias"], p["k_k"], p["k_a"], p["r_k"], p["ln_w"], p["ln_b"]]
    out_specs = [pl.BlockSpec((seq, W), lambda b: (b, 0))]
    out_shape = [jax.ShapeDtypeStruct((batch * seq, W), F32)]
    if latent:
        in_specs.append(pl.BlockSpec((None, 2, W, W), lambda b: (b, 0, 0, 0)))
        args.append(s0)
    else:
        out_specs.append(pl.BlockSpec((None, 2, W, W), lambda b: (b, 0, 0, 0)))
        out_shape.append(jax.ShapeDtypeStruct((batch, 2, W, W), F32))
    res = pl.pallas_call(
        functools.partial(_rwkv_kernel, latent, seq),
        grid=(batch,),
        in_specs=in_specs,
        out_specs=out_specs,
        out_shape=out_shape,
        scratch_shapes=[
            pltpu.VMEM((seq, 2 * W), F32),
            pltpu.VMEM((seq, 2 * W), F32),
            pltpu.VMEM((seq, 2 * W), F32),
            pltpu.VMEM((seq, W), F32),
            pltpu.VMEM((seq, W), F32),
            pltpu.VMEM((seq, W), F32),
            pltpu.VMEM((2, W, W), F32),
        ],
        compiler_params=_params("parallel"),
        name="rwkv_latent" if latent else "rwkv_ctx",
    )(*args)
    return res if not latent else res[0]


def _post_prologue(x_ref, mix_refs, wout_ref, n2_ref, g1_ref, sh_ref, sc_ref, xn_s):
    mix = jnp.zeros(x_ref.shape, F32)
    for i, m_ref in enumerate(mix_refs):
        mix = mix + jnp.dot(m_ref[...].astype(BF16), wout_ref[i * COLBLK:(i + 1) * COLBLK, :],
                            preferred_element_type=F32)
    xn = x_ref[...] + g1_ref[...] * mix
    xn_s[...] = xn
    return _rms(xn, n2_ref[...]) * (1.0 + sc_ref[...]) + sh_ref[...]


def _post_dense_kernel(x_ref, a_ref, c_ref, rw_ref, f_ref, wout_ref, n2_ref, g1_ref, sh_ref, sc_ref, g2_ref,
                       w1_ref, w3_ref, w2_ref, o_ref, xn_s, h_s, acc_s):
    f = pl.program_id(1)

    @pl.when(f == 0)
    def _():
        h = _post_prologue(x_ref, (a_ref, c_ref, rw_ref, f_ref), wout_ref, n2_ref, g1_ref, sh_ref, sc_ref, xn_s)
        h_s[...] = h.astype(BF16)
        acc_s[...] = jnp.zeros(acc_s.shape, F32)

    h = h_s[...]
    t = _silu(jnp.dot(h, w1_ref[...], preferred_element_type=F32)) * jnp.dot(h, w3_ref[...],
                                                                            preferred_element_type=F32)
    acc_s[...] += jnp.dot(t.astype(BF16), w2_ref[...], preferred_element_type=F32)

    @pl.when(f == pl.num_programs(1) - 1)
    def _():
        o_ref[...] = xn_s[...] + g2_ref[...] * acc_s[...]


def _post_moe_kernel(x_ref, a_ref, c_ref, rw_ref, f_ref, wout_ref, n2_ref, g1_ref, sh_ref, sc_ref, g2_ref,
                     router_ref, w1_ref, w3_ref, w2_ref, o_ref, xn_s, h_s, acc_s, gate_s):
    e = pl.program_id(1)
    lane = _iota((1, LANES), 1)

    @pl.when(e == 0)
    def _():
        h = _post_prologue(x_ref, (a_ref, c_ref, rw_ref, f_ref), wout_ref, n2_ref, g1_ref, sh_ref, sc_ref, xn_s)
        h_s[...] = h.astype(BF16)
        acc_s[...] = jnp.zeros(acc_s.shape, F32)
        logits = jnp.where(lane < N_EXPERTS, _mm_hi(h, router_ref[...]), NEG_INF)
        ex = jnp.exp(logits - jnp.max(logits, axis=-1, keepdims=True))
        probs = ex / jnp.sum(ex, axis=-1, keepdims=True)
        p1 = jnp.max(probs, axis=-1, keepdims=True)
        i1 = jnp.min(jnp.where(probs == p1, lane, LANES), axis=-1, keepdims=True)
        rest = jnp.where(lane == i1, -1.0, probs)
        p2 = jnp.max(rest, axis=-1, keepdims=True)
        i2 = jnp.min(jnp.where(rest == p2, lane, LANES), axis=-1, keepdims=True)
        top = p1 + p2
        gate_s[...] = jnp.where(lane == i1, p1 / top, jnp.where(lane == i2, p2 / top, 0.0))

    h = h_s[...]
    t = _silu(jnp.dot(h, w1_ref[...], preferred_element_type=F32)) * jnp.dot(h, w3_ref[...],
                                                                            preferred_element_type=F32)
    gate = jnp.sum(jnp.where(lane == e, gate_s[...], 0.0), axis=-1, keepdims=True)
    acc_s[...] += gate * jnp.dot(t.astype(BF16), w2_ref[...], preferred_element_type=F32)

    @pl.when(e == pl.num_programs(1) - 1)
    def _():
        o_ref[...] = xn_s[...] + g2_ref[...] * acc_s[...]


def _post(x, mixes, w_out, norm_w, mod, row_of_tile, ffn, moe):
    n_tok = x.shape[0]
    tile = lambda width: pl.BlockSpec((TOKEN_TILE, width), lambda i, f: (i, 0))
    const = lambda shape: pl.BlockSpec(shape, lambda i, f: (0, 0))
    in_specs = [tile(D_MODEL)] + [tile(COLBLK)] * 4 + [const((MIX_WIDTH, D_MODEL)), const((1, D_MODEL)),
                                                       _mod_spec(2, row_of_tile), _mod_spec(3, row_of_tile),
                                                       _mod_spec(4, row_of_tile), _mod_spec(5, row_of_tile)]
    args = [x, *mixes, w_out, norm_w, mod, mod, mod, mod]
    scratch = [pltpu.VMEM((TOKEN_TILE, D_MODEL), F32), pltpu.VMEM((TOKEN_TILE, D_MODEL), BF16),
               pltpu.VMEM((TOKEN_TILE, D_MODEL), F32)]
    if moe:
        router, w1, w3, w2 = ffn
        in_specs += [const((D_MODEL, LANES)),
                     pl.BlockSpec((None, D_MODEL, D_FF_EXPERT), lambda i, e: (e, 0, 0)),
                     pl.BlockSpec((None, D_MODEL, D_FF_EXPERT), lambda i, e: (e, 0, 0)),
                     pl.BlockSpec((None, D_FF_EXPERT, D_MODEL), lambda i, e: (e, 0, 0))]
        args += [router, w1, w3, w2]
        scratch.append(pltpu.VMEM((TOKEN_TILE, LANES), F32))
        body, steps, name = _post_moe_kernel, N_EXPERTS, "post_moe"
    else:
        w1, w3, w2 = ffn
        in_specs += [pl.BlockSpec((D_MODEL, FF_TILE), lambda i, f: (0, f)),
                     pl.BlockSpec((D_MODEL, FF_TILE), lambda i, f: (0, f)),
                     pl.BlockSpec((FF_TILE, D_MODEL), lambda i, f: (f, 0))]
        args += [w1, w3, w2]
        body, steps, name = _post_dense_kernel, D_FF // FF_TILE, "post_dense"
    return pl.pallas_call(
        body,
        grid=(n_tok // TOKEN_TILE, steps),
        in_specs=in_specs,
        out_specs=tile(D_MODEL),
        out_shape=jax.ShapeDtypeStruct((n_tok, D_MODEL), F32),
        scratch_shapes=scratch,
        compiler_params=_params("parallel", "arbitrary"),
        name=name,
    )(*args)


def _rope_tables(n_tokens, n_heads):
    n_rows = n_tokens // GRID_W
    row = jnp.repeat(jnp.arange(n_rows), GRID_W).astype(F32)
    col = jnp.tile(jnp.arange(GRID_W), n_rows).astype(F32)
    quarter = HEAD_DIM // 4
    inv_freq = ROPE_BASE ** (-jnp.arange(quarter, dtype=F32) / quarter)
    ang_r, ang_c = row[:, None] * inv_freq, col[:, None] * inv_freq
    cos = jnp.concatenate([jnp.cos(ang_r)] * 2 + [jnp.cos(ang_c)] * 2, axis=1)
    sin = jnp.concatenate([-jnp.sin(ang_r), jnp.sin(ang_r), -jnp.sin(ang_c), jnp.sin(ang_c)], axis=1)
    return jnp.tile(cos, (1, n_heads)), jnp.tile(sin, (1, n_heads))


def _layout_w_in(w):
    lr_end = ATTN_WIDTH + 2 * KV_WIDTH + 3 * CONV_CH + 3 * RWKV_WIDTH + LOWRANK_WIDTH
    pad = jnp.zeros((w.shape[0], COLBLK - LOWRANK_WIDTH), w.dtype)
    return jnp.concatenate([w[:, :lr_end], pad, w[:, lr_end:]], axis=1).astype(BF16)


def _layout_rwkv(l, w0, w_up, a0, a_up, g_up, k_k, k_a, r_k, ln_w, ln_b):
    W = RWKV_WIDTH
    lrw = jnp.zeros((COLBLK, 5 * W), F32)
    for j in range(2):
        lrw = lrw.at[j * DECAY_RANK:(j + 1) * DECAY_RANK, j * W:(j + 1) * W].set(w_up[l, j])
        r0 = 2 * DECAY_RANK + j * ICLR_RANK
        lrw = lrw.at[r0:r0 + ICLR_RANK, (2 + j) * W:(3 + j) * W].set(a_up[l, j])
    r0 = 2 * DECAY_RANK + 2 * ICLR_RANK
    lrw = lrw.at[r0:r0 + GATE_RANK, 4 * W:].set(g_up[l])
    bias = jnp.concatenate([w0[l, 0], w0[l, 1], a0[l, 0], a0[l, 1], jnp.zeros((W,), F32)])[None, :]
    row = lambda t: t[l].reshape(1, W)
    return {"lrw": lrw.astype(BF16), "lr_bias": bias, "k_k": row(k_k), "k_a": row(k_a), "r_k": row(r_k),
            "ln_w": row(ln_w), "ln_b": row(ln_b)}


def _state_to_blockdiag(s):
    eye = jnp.eye(RWKV_HEADS, dtype=s.dtype)
    bd = jnp.einsum("bjhvk,hg->bjhkgv", s, eye)
    return bd.reshape(s.shape[0], 2, RWKV_WIDTH, RWKV_WIDTH)


def _blockdiag_to_state(bd):
    b = bd.shape[0]
    t = bd.reshape(b, 2, RWKV_HEADS, HEAD_DIM, RWKV_HEADS, HEAD_DIM)
    diag = jnp.stack([t[:, :, h, :, h, :] for h in range(RWKV_HEADS)], axis=2)
    return jnp.swapaxes(diag, -1, -2)


def kernel(x_prompt, x_sample, cache_k, cache_v, state_wkv, c, c_ctx, mod_w, mod_b, norm1_w, norm2_w, w_in, q_norm_w, k_norm_w, attn_sink, conv_w, rwkv_w0, rwkv_w_up, rwkv_a0, rwkv_a_up, rwkv_g_up, rwkv_k_k, rwkv_k_a, rwkv_r_k, rwkv_ln_w, rwkv_ln_b, w_out, ffn_w1, ffn_w3, ffn_w2, router_w, moe_w1, moe_w3, moe_w2):
    batch, seq, _ = x_prompt.shape
    dec_batch, dec_seq, _ = x_sample.shape
    past = cache_k.shape[2]
    assert seq % BLOCK == 0 and dec_seq % TOKEN_TILE == 0 and (batch * seq) % TOKEN_TILE == 0
    assert 1 + dec_batch <= MOD_ROWS

    cond = jnp.concatenate([c_ctx[None, :], c, jnp.zeros((MOD_ROWS - 1 - dec_batch, D_MODEL), F32)], axis=0)
    mod_all = _modulation(cond, mod_w, mod_b)
    prompt_row = lambda i: 0
    sample_row = lambda i: 1 + (i * TOKEN_TILE) // dec_seq

    rope_q = _rope_tables(dec_seq, N_Q_HEADS)
    rope_k = _rope_tables(dec_seq, N_KV_HEADS)

    xp = x_prompt.reshape(batch * seq, D_MODEL)
    xs = x_sample.reshape(dec_batch * dec_seq, D_MODEL)
    new_k, new_v, new_s = [], [], []
    for l in range(DEPTH):
        mod = mod_all[l].reshape(MOD_ROWS, 6, 1, D_MODEL)
        w_in_l = _layout_w_in(w_in[l])
        n1 = norm1_w[l][None, :]
        n2 = norm2_w[l][None, :]
        qn = jnp.tile(q_norm_w[l], N_Q_HEADS)[None, :]
        kn = jnp.tile(k_norm_w[l], N_KV_HEADS)[None, :]
        cw = jnp.concatenate([conv_w[l], jnp.zeros((5, CONV_CH), F32)], axis=0)
        rp = _layout_rwkv(l, rwkv_w0, rwkv_w_up, rwkv_a0, rwkv_a_up, rwkv_g_up, rwkv_k_k, rwkv_k_a, rwkv_r_k,
                          rwkv_ln_w, rwkv_ln_b)
        w_out_l = w_out[l].astype(BF16)
        if l % 2 == 0:
            j = l // 2
            ffn = (ffn_w1[j].astype(BF16), ffn_w3[j].astype(BF16), ffn_w2[j].astype(BF16))
        else:
            j = l // 2
            router = jnp.concatenate([router_w[j], jnp.zeros((D_MODEL, LANES - N_EXPERTS), F32)], axis=1)
            ffn = (router, moe_w1[j].astype(BF16), moe_w3[j].astype(BF16), moe_w2[j].astype(BF16))

        proj_p = _in_proj(xp, n1, mod, prompt_row, w_in_l)
        attn_p, k_p, v_p = _ctx_attention(proj_p, qn, kn, attn_sink[l], batch, seq)
        conv_p, four_p = _convfour(proj_p, 0, cw, batch, seq)
        rw_p, sfin = _rwkv(proj_p, 0, rp, batch, seq)
        xp = _post(xp, (attn_p, conv_p, rw_p, four_p), w_out_l, n2, mod, prompt_row, ffn, l % 2 == 1)
        new_k.append(k_p.reshape(batch, seq, N_KV_HEADS, HEAD_DIM))
        new_v.append(v_p.reshape(batch, seq, N_KV_HEADS, HEAD_DIM))
        new_s.append(_blockdiag_to_state(sfin))

        proj_s = _in_proj(xs, n1, mod, sample_row, w_in_l)
        kc = cache_k[:, l].reshape(dec_batch, past, KV_WIDTH)
        vc = cache_v[:, l].reshape(dec_batch, past, KV_WIDTH)
        attn_s = _lat_attention(proj_s, 0, kc, vc, qn, kn, rope_q, rope_k, attn_sink[l], dec_batch, dec_seq, past)
        conv_s, four_s = _convfour(proj_s, 0, cw, dec_batch, dec_seq)
        rw_s = _rwkv(proj_s, 0, rp, dec_batch, dec_seq, s0=_state_to_blockdiag(state_wkv[:, l]))
        xs = _post(xs, (attn_s, conv_s, rw_s, four_s), w_out_l, n2, mod, sample_row, ffn, l % 2 == 1)

    return (xp.reshape(batch, seq, D_MODEL), xs.reshape(dec_batch, dec_seq, D_MODEL),
            jnp.stack(new_k, axis=1), jnp.stack(new_v, axis=1), jnp.stack(new_s, axis=1))
```

```python
import functools
import math

import jax
import jax.numpy as jnp
from jax import lax
from jax.experimental import pallas as pl
from jax.experimental.pallas import tpu as pltpu

F32 = jnp.float32
BF16 = jnp.bfloat16

D_MODEL = 1024
DEPTH = 2
GRID_W = 64
HEAD_DIM = 64
N_Q_HEADS = 4
N_KV_HEADS = 2
WINDOW = 128
BLOCK = 128
ROPE_BASE = 10000.0
ATTN_WIDTH = N_Q_HEADS * HEAD_DIM
KV_WIDTH = N_KV_HEADS * HEAD_DIM
CONV_CH = 256
RWKV_HEADS = 4
RWKV_WIDTH = RWKV_HEADS * HEAD_DIM
DECAY_RANK = 32
ICLR_RANK = 32
GATE_RANK = 64
FOURIER_GROUPS = 4
FOURIER_WIDTH = FOURIER_GROUPS * HEAD_DIM
MIX_WIDTH = ATTN_WIDTH + CONV_CH + RWKV_WIDTH + FOURIER_WIDTH
D_FF = 2816
N_EXPERTS = 8
D_FF_EXPERT = 1024
NORM_EPS = 1e-6
GN_EPS = 64e-5
NEG_INF = -1e30

LANES = 128
VMEM_LIMIT_BYTES = 56 * 1024 * 1024

PROJ_WIDTH = 2560
COLBLK = 256
LOWRANK_WIDTH = 2 * DECAY_RANK + 2 * ICLR_RANK + GATE_RANK
MOD_ROWS = 16
TOKEN_TILE = 512
RWKV_CHUNK = 64
RWKV_CHUNKS_PER_STEP = 2
RWKV_PREP_ROWS = 256
FF_TILE = 1408
MOE_CAPACITY = 160
DECAY_SCALE = -math.exp(-0.5)


def _params(*sem):
    return pltpu.CompilerParams(dimension_semantics=sem, vmem_limit_bytes=VMEM_LIMIT_BYTES)


def _mm(a, b):
    return jnp.dot(a.astype(BF16), b.astype(BF16), preferred_element_type=F32)


def _mm_nt(a, b):
    return lax.dot_general(a.astype(BF16), b.astype(BF16), (((1,), (1,)), ((), ())), preferred_element_type=F32)


def _mm_tn(a, b):
    return jnp.dot(a.T.astype(BF16), b.astype(BF16), preferred_element_type=F32)


def _mm_split(a, b, split_rhs=False, parts=2):
    exact, x = (a, b) if split_rhs else (b, a)
    acc = None
    for _ in range(parts):
        piece = x.astype(BF16)
        x = x - piece.astype(F32)
        term = (jnp.dot(exact, piece, preferred_element_type=F32) if split_rhs
                else jnp.dot(piece, exact, preferred_element_type=F32))
        acc = term if acc is None else acc + term
    return acc


def _silu(x):
    return x * jax.nn.sigmoid(x)


def _iota(shape, dim):
    return lax.broadcasted_iota(jnp.int32, shape, dim)


def _mod_kernel(c_ref, w_ref, b_ref, o_ref):
    o_ref[0] = _mm(_silu(c_ref[...]), w_ref[0]) + b_ref[0]


def _modulation(cond, mod_w, mod_b):
    n_chunks = mod_w.shape[-1] // D_MODEL
    return pl.pallas_call(
        _mod_kernel,
        grid=(DEPTH, n_chunks),
        in_specs=[
            pl.BlockSpec((MOD_ROWS, D_MODEL), lambda l, j: (0, 0)),
            pl.BlockSpec((1, D_MODEL, D_MODEL), lambda l, j: (l, 0, j)),
            pl.BlockSpec((1, 1, D_MODEL), lambda l, j: (l, 0, j)),
        ],
        out_specs=pl.BlockSpec((1, MOD_ROWS, D_MODEL), lambda l, j: (l, 0, j)),
        out_shape=jax.ShapeDtypeStruct((DEPTH, MOD_ROWS, n_chunks * D_MODEL), F32),
        compiler_params=_params("parallel", "parallel"),
        name="modulation",
    )(cond, mod_w, mod_b.reshape(DEPTH, 1, -1))


def _mod_spec(chunk, row_of_tile):
    return pl.BlockSpec((None, None, 1, D_MODEL), lambda i, *_: (row_of_tile(i), chunk, 0, 0))


def _rms(x, w):
    return x * lax.rsqrt(jnp.mean(x * x, axis=-1, keepdims=True) + NORM_EPS) * w


def _in_proj_kernel(x_ref, nw_ref, sh_ref, sc_ref, w_ref, o_ref):
    h = _rms(x_ref[...], nw_ref[...]) * (1.0 + sc_ref[...]) + sh_ref[...]
    o_ref[...] = jnp.dot(h.astype(BF16), w_ref[...], preferred_element_type=F32)


def _in_proj(x, norm_w, mod, row_of_tile, w_in):
    n_tok = x.shape[0]
    return pl.pallas_call(
        _in_proj_kernel,
        grid=(n_tok // TOKEN_TILE,),
        in_specs=[
            pl.BlockSpec((TOKEN_TILE, D_MODEL), lambda i: (i, 0)),
            pl.BlockSpec((1, D_MODEL), lambda i: (0, 0)),
            _mod_spec(0, row_of_tile),
            _mod_spec(1, row_of_tile),
            pl.BlockSpec((D_MODEL, PROJ_WIDTH), lambda i: (0, 0)),
        ],
        out_specs=pl.BlockSpec((TOKEN_TILE, PROJ_WIDTH), lambda i: (i, 0)),
        out_shape=jax.ShapeDtypeStruct((n_tok, PROJ_WIDTH), F32),
        compiler_params=_params("parallel"),
        name="in_proj",
    )(x, norm_w, mod, mod, w_in)


def _group_sum_matrix(width):
    r = _iota((width, width), 0) // HEAD_DIM
    c = _iota((width, width), 1) // HEAD_DIM
    return jnp.where(r == c, 1.0, 0.0).astype(BF16)


def _head_rms(x, w):
    ms = _mm_split(x * x, _group_sum_matrix(x.shape[-1])) * (1.0 / HEAD_DIM)
    return x * lax.rsqrt(ms + NORM_EPS) * w


def _dup_kv_head(x, g):
    lane = _iota((1, KV_WIDTH), 1)
    rolled = pltpu.roll(x, HEAD_DIM, 1)
    first = lane < HEAD_DIM
    return jnp.where(first, x, rolled) if g == 0 else jnp.where(first, rolled, x)


def _half_mask(j):
    lane = _iota((1, KV_WIDTH), 1)
    return (lane >= j * HEAD_DIM) & (lane < (j + 1) * HEAD_DIM)


def _ctx_attn_kernel(qkv_ref, qw_ref, kw_ref, sink_ref, o_ref, k_ref, v_ref):
    qkv = qkv_ref[...]
    q = _head_rms(qkv[:, :ATTN_WIDTH], qw_ref[...]) * HEAD_DIM ** -0.5
    k = _head_rms(qkv[:, ATTN_WIDTH:ATTN_WIDTH + KV_WIDTH], kw_ref[...])
    v = qkv[:, ATTN_WIDTH + KV_WIDTH:]
    k_ref[...] = k
    v_ref[...] = v
    for g in range(N_KV_HEADS):
        qg = q[:, g * KV_WIDTH:(g + 1) * KV_WIDTH]
        kg = _dup_kv_head(k, g).astype(BF16)
        vg = _dup_kv_head(v, g)
        acc = jnp.zeros(qg.shape, F32)
        for j in range(2):
            mj = _half_mask(j)
            s = sink_ref[2 * g + j]
            logits = _mm_nt(jnp.where(mj, qg, 0.0), kg)
            m = jnp.maximum(jnp.max(logits, axis=-1, keepdims=True), s)
            e = jnp.exp(logits - m)
            den = jnp.sum(e, axis=-1, keepdims=True) + jnp.exp(s - m)
            acc = acc + _mm(e, jnp.where(mj, vg, 0.0)) / den
        o_ref[:, g * KV_WIDTH:(g + 1) * KV_WIDTH] = acc


def _ctx_attention(proj, q_norm, k_norm, sink, batch, seq):
    blk = ATTN_WIDTH + 2 * KV_WIDTH
    return pl.pallas_call(
        _ctx_attn_kernel,
        grid=(batch,),
        in_specs=[
            pl.BlockSpec((seq, blk), lambda b: (b, 0)),
            pl.BlockSpec((1, ATTN_WIDTH), lambda b: (0, 0)),
            pl.BlockSpec((1, KV_WIDTH), lambda b: (0, 0)),
            pl.BlockSpec(memory_space=pltpu.SMEM),
        ],
        out_specs=[
            pl.BlockSpec((seq, ATTN_WIDTH), lambda b: (b, 0)),
            pl.BlockSpec((seq, KV_WIDTH), lambda b: (b, 0)),
            pl.BlockSpec((seq, KV_WIDTH), lambda b: (b, 0)),
        ],
        out_shape=[
            jax.ShapeDtypeStruct((batch * seq, ATTN_WIDTH), F32),
            jax.ShapeDtypeStruct((batch * seq, KV_WIDTH), F32),
            jax.ShapeDtypeStruct((batch * seq, KV_WIDTH), F32),
        ],
        compiler_params=_params("parallel"),
        name="ctx_attention",
    )(proj, q_norm, k_norm, sink)


def _rope(x, cos, sin_signed):
    width = x.shape[-1]
    lane = _iota((1, width), 1)
    first_half = (lane % (HEAD_DIM // 2)) < (HEAD_DIM // 4)
    partner = jnp.where(first_half, pltpu.roll(x, width - HEAD_DIM // 4, 1), pltpu.roll(x, HEAD_DIM // 4, 1))
    return x * cos + partner * sin_signed


def _lat_attn_kernel(seq, past, qkv_ref, kc_ref, vc_ref, qw_ref, kw_ref, cosq_ref, snq_ref, cosk_ref, snk_ref,
                     sink_ref, o_ref, q_s, k_s, v_s):
    qkv = qkv_ref[...]
    q = _rope(_head_rms(qkv[:, :ATTN_WIDTH], qw_ref[...]), cosq_ref[...], snq_ref[...]) * HEAD_DIM ** -0.5
    k = _rope(_head_rms(qkv[:, ATTN_WIDTH:ATTN_WIDTH + KV_WIDTH], kw_ref[...]), cosk_ref[...], snk_ref[...])
    v = qkv[:, ATTN_WIDTH + KV_WIDTH:]
    kc = kc_ref[...]
    vc = vc_ref[...]
    q_s[...] = q
    zero_blk = jnp.zeros((BLOCK, KV_WIDTH), BF16)
    for g in range(N_KV_HEADS):
        k_s[g, 0:BLOCK] = zero_blk
        k_s[g, BLOCK:BLOCK + seq] = _dup_kv_head(k, g).astype(BF16)
        k_s[g, BLOCK + seq:2 * BLOCK + seq] = zero_blk
        k_s[g, 2 * BLOCK + seq:] = _dup_kv_head(kc, g).astype(BF16)
        vg = _dup_kv_head(v, g)
        vcg = _dup_kv_head(vc, g)
        for j in range(2):
            mj = _half_mask(j)
            v_s[2 * g + j, 0:BLOCK] = zero_blk
            v_s[2 * g + j, BLOCK:BLOCK + seq] = jnp.where(mj, vg, 0.0).astype(BF16)
            v_s[2 * g + j, BLOCK + seq:2 * BLOCK + seq] = zero_blk
            v_s[2 * g + j, 2 * BLOCK + seq:] = jnp.where(mj, vcg, 0.0).astype(BF16)

    win = 3 * BLOCK
    ctx0 = 2 * BLOCK + seq

    def q_block(n, carry):
        q0 = pl.multiple_of(n * BLOCK, BLOCK)
        qpos = n * BLOCK + _iota((BLOCK, win), 0)
        kpos = (n - 1) * BLOCK + _iota((BLOCK, win), 1)
        valid = (jnp.abs(qpos - kpos) <= WINDOW) & (kpos >= 0) & (kpos < seq)
        for g in range(N_KV_HEADS):
            qg = q_s[pl.ds(q0, BLOCK), g * KV_WIDTH:(g + 1) * KV_WIDTH]
            kw = k_s[g, pl.ds(q0, win), :]
            kcx = k_s[g, ctx0:ctx0 + past, :]
            acc = jnp.zeros((BLOCK, KV_WIDTH), F32)
            for j in range(2):
                h = 2 * g + j
                s = sink_ref[h]
                qm = jnp.where(_half_mask(j), qg, 0.0)
                lw = jnp.where(valid, _mm_nt(qm, kw), NEG_INF)
                lc = _mm_nt(qm, kcx)
                m = jnp.maximum(jnp.maximum(jnp.max(lw, axis=-1, keepdims=True),
                                            jnp.max(lc, axis=-1, keepdims=True)), s)
                ew = jnp.exp(lw - m)
                ec = jnp.exp(lc - m)
                den = jnp.sum(ew, axis=-1, keepdims=True) + jnp.sum(ec, axis=-1, keepdims=True) + jnp.exp(s - m)
                pv = (jnp.dot(ew.astype(BF16), v_s[h, pl.ds(q0, win), :], preferred_element_type=F32)
                      + jnp.dot(ec.astype(BF16), v_s[h, ctx0:ctx0 + past, :], preferred_element_type=F32))
                acc = acc + pv / den
            o_ref[pl.ds(q0, BLOCK), g * KV_WIDTH:(g + 1) * KV_WIDTH] = acc
        return carry

    lax.fori_loop(0, seq // BLOCK, q_block, 0)


def _lat_attention(proj, tok_block0, cache_k, cache_v, q_norm, k_norm, rope_q, rope_k, sink, batch, seq, past):
    blk = ATTN_WIDTH + 2 * KV_WIDTH
    cos_q, sin_q = rope_q
    cos_k, sin_k = rope_k
    rows = 2 * BLOCK + seq + past
    const = lambda shape: pl.BlockSpec(shape, lambda b: (0, 0))
    return pl.pallas_call(
        functools.partial(_lat_attn_kernel, seq, past),
        grid=(batch,),
        in_specs=[
            pl.BlockSpec((seq, blk), lambda b: (tok_block0 + b, 0)),
            pl.BlockSpec((None, past, KV_WIDTH), lambda b: (b, 0, 0)),
            pl.BlockSpec((None, past, KV_WIDTH), lambda b: (b, 0, 0)),
            const((1, ATTN_WIDTH)),
            const((1, KV_WIDTH)),
            const((seq, ATTN_WIDTH)),
            const((seq, ATTN_WIDTH)),
            const((seq, KV_WIDTH)),
            const((seq, KV_WIDTH)),
            pl.BlockSpec(memory_space=pltpu.SMEM),
        ],
        out_specs=pl.BlockSpec((seq, ATTN_WIDTH), lambda b: (b, 0)),
        out_shape=jax.ShapeDtypeStruct((batch * seq, ATTN_WIDTH), F32),
        scratch_shapes=[
            pltpu.VMEM((seq, ATTN_WIDTH), F32),
            pltpu.VMEM((N_KV_HEADS, rows, KV_WIDTH), BF16),
            pltpu.VMEM((N_Q_HEADS, rows, KV_WIDTH), BF16),
        ],
        compiler_params=_params("parallel"),
        name="latent_attention",
    )(proj, cache_k, cache_v, q_norm, k_norm, cos_q, sin_q, cos_k, sin_k, sink)


def _convfour_kernel(seq, bx_ref, bb_ref, bc_ref, dx_ref, cw_ref, ct_ref, st_ref, cc_ref, sc_ref, conv_ref, four_ref):
    u = bc_ref[...] * bx_ref[...]
    row = _iota((seq, 1), 0)
    prev = jnp.where(row == 0, 0.0, pltpu.roll(u, 1, 0))
    nxt = jnp.where(row == seq - 1, 0.0, pltpu.roll(u, seq - 1, 0))
    cw = cw_ref[...]
    conv_ref[...] = bb_ref[...] * (cw[0:1] * prev + cw[1:2] * u + cw[2:3] * nxt)
    x = dx_ref[...].astype(BF16)
    a = jnp.dot(ct_ref[...], x, preferred_element_type=F32)
    b = jnp.dot(st_ref[...], x, preferred_element_type=F32)
    y = (jnp.dot(a.astype(BF16), cc_ref[...], preferred_element_type=F32)
         - jnp.dot(b.astype(BF16), sc_ref[...], preferred_element_type=F32))
    four_ref[...] = y * (1.0 / math.sqrt(seq * HEAD_DIM))


def _dft_tables(n):
    idx = jnp.arange(n, dtype=jnp.int32)
    ang = ((idx[:, None] * idx[None, :]) % n).astype(F32) * (2.0 * math.pi / n)
    return jnp.cos(ang), jnp.sin(ang)


def _convfour(proj, tok_block0, conv_w, batch, seq):
    ct, st = _dft_tables(seq)
    c64, s64 = _dft_tables(HEAD_DIM)
    eye = jnp.eye(FOURIER_GROUPS, dtype=F32)
    cc, sc = jnp.kron(eye, c64), jnp.kron(eye, s64)
    col = lambda c: pl.BlockSpec((seq, COLBLK), lambda b: (tok_block0 + b, c))
    const = lambda shape: pl.BlockSpec(shape, lambda b: (0, 0))
    out = pl.BlockSpec((seq, COLBLK), lambda b: (b, 0))
    return pl.pallas_call(
        functools.partial(_convfour_kernel, seq),
        grid=(batch,),
        in_specs=[col(2), col(3), col(4), col(9), const((8, CONV_CH)), const((seq, seq)), const((seq, seq)),
                  const((FOURIER_WIDTH, FOURIER_WIDTH)), const((FOURIER_WIDTH, FOURIER_WIDTH))],
        out_specs=[out, out],
        out_shape=[jax.ShapeDtypeStruct((batch * seq, COLBLK), F32)] * 2,
        compiler_params=_params("parallel"),
        name="conv_fourier",
    )(proj, proj, proj, proj, conv_w, ct.astype(BF16), st.astype(BF16), cc.astype(BF16), sc.astype(BF16))


def _rwkv_kernel(latent, seq, *refs):
    if latent:
        (r_ref, k_ref, v_ref, lr_ref, lrw_ref, bias_ref, kk_ref, ka_ref, rk_ref, lnw_ref, lnb_ref, s0_ref,
         o_ref, logw_s, kd_s, b_s, z_s, g_s, y_s, st_s) = refs
    else:
        (r_ref, k_ref, v_ref, lr_ref, lrw_ref, bias_ref, kk_ref, ka_ref, rk_ref, lnw_ref, lnb_ref,
         o_ref, sfin_ref, logw_s, kd_s, b_s, z_s, g_s, y_s, st_s) = refs
    W = RWKV_WIDTH
    C = RWKV_CHUNK
    gsum = _group_sum_matrix(W)
    lane = _iota((1, W), 1)

    def prep(i, carry):
        rows = pl.ds(pl.multiple_of(i * RWKV_PREP_ROWS, RWKV_PREP_ROWS), RWKV_PREP_ROWS)
        lr = lr_ref[rows, :]
        f = jnp.where(lane < 2 * DECAY_RANK, jnp.tanh(lr),
                      jnp.where(lane < 2 * DECAY_RANK + 2 * ICLR_RANK, lr, jax.nn.sigmoid(lr)))
        low = jnp.dot(f.astype(BF16), lrw_ref[...], preferred_element_type=F32) + bias_ref[...]
        logw_s[rows, :] = DECAY_SCALE * jax.nn.sigmoid(low[:, :2 * W])
        a = jax.nn.sigmoid(low[:, 2 * W:4 * W])
        g_s[rows, :] = low[:, 4 * W:]
        k = k_ref[rows, :]
        kk = k * kk_ref[...]
        kk = kk / jnp.maximum(jnp.sqrt(_mm_split(kk * kk, gsum)), 1e-12)
        z_s[rows, :] = -kk
        for j in range(2):
            aj = a[:, j * W:(j + 1) * W]
            kd_s[rows, j * W:(j + 1) * W] = k * (1.0 + (aj - 1.0) * ka_ref[...])
            b_s[rows, j * W:(j + 1) * W] = kk * aj
        return carry

    lax.fori_loop(0, seq // RWKV_PREP_ROWS, prep, 0)

    y_s[...] = jnp.zeros((seq, W), F32)
    if latent:
        st_s[...] = s0_ref[...]
    else:
        st_s[...] = jnp.zeros((2, W, W), F32)

    HC = RWKV_HEADS * C
    row_i = _iota((HC, HC), 0)
    col_i = _iota((HC, HC), 1)
    same_head = (row_i // C) == (col_i // C)
    t_i, s_i = row_i % C, col_i % C
    eye_s = jnp.where(row_i == col_i, 1.0, 0.0).astype(F32)
    bd = (_iota((W, W), 0) // HEAD_DIM) == (_iota((W, W), 1) // HEAD_DIM)
    eye_w = _iota((W, W), 0) == _iota((W, W), 1)
    tc_i = _iota((C, C), 0)
    sc_i = _iota((C, C), 1)
    head_masks = [(lane >= h * HEAD_DIM) & (lane < (h + 1) * HEAD_DIM) for h in range(RWKV_HEADS)]
    n_chunks = seq // C

    def stack_masked(x):
        return jnp.concatenate([jnp.where(m, x, 0.0) for m in head_masks], axis=0).astype(BF16)

    def stack_plain(x):
        return jnp.concatenate([x.astype(BF16)] * RWKV_HEADS, axis=0)

    def unstack(x):
        out = x[0:C]
        for h in range(1, RWKV_HEADS):
            out = out + x[h * C:(h + 1) * C]
        return out

    def bdot(a, b):
        return jnp.dot(a, b, preferred_element_type=F32)

    def chunk(j, c):
        rows = pl.ds(pl.multiple_of(c * C, C), C)
        cols = slice(j * W, (j + 1) * W)
        rc, vc, zc = r_ref[rows, :], v_ref[rows, :], z_s[rows, :]
        lw, kd, bc = logw_s[rows, cols], kd_s[rows, cols], b_s[rows, cols]
        if j == 0:
            strict, incl, tri = same_head & (s_i < t_i), same_head & (s_i <= t_i), sc_i <= tc_i
        else:
            strict, incl, tri = same_head & (s_i > t_i), same_head & (s_i >= t_i), sc_i >= tc_i
        cl = _mm_split(jnp.where(tri, 1.0, 0.0).astype(BF16), lw, split_rhs=True, parts=3)
        yield
        tot = cl[C - 1:C, :] if j == 0 else cl[0:1, :]
        w_in, w_ex, w_inv, w_rest = jnp.exp(cl), jnp.exp(cl - lw), jnp.exp(-cl), jnp.exp(tot - cl)
        rt = rc * w_in
        zs = stack_masked(zc * w_ex)
        vs = stack_masked(vc)
        aa = lax.dot_general(jnp.concatenate([zs, stack_masked(rt)], axis=0),
                             jnp.concatenate([stack_plain(bc * w_inv), stack_plain(kd * w_inv)], axis=0),
                             (((1,), (1,)), ((), ())), preferred_element_type=F32)
        yield
        a_zb = jnp.where(strict, aa[:HC, :HC], 0.0)
        a_zk = jnp.where(strict, aa[:HC, HC:], 0.0).astype(BF16)
        a_rb = jnp.where(incl, aa[HC:, :HC], 0.0).astype(BF16)
        a_rk = jnp.where(incl, aa[HC:, HC:], 0.0).astype(BF16)
        tinv = eye_s + a_zb
        a1 = a_zb.astype(BF16)
        apow = bdot(a1, a1)
        av = bdot(a_zk, vs)
        yield
        for _ in range(int(math.log2(C)) - 2):
            ab = apow.astype(BF16)
            both = bdot(jnp.concatenate([ab, tinv.astype(BF16)], axis=0), ab)
            yield
            apow = both[:HC]
            tinv = tinv + both[HC:]
        tinv = tinv + bdot(tinv.astype(BF16), apow.astype(BF16))
        yield
        x = bdot(tinv.astype(BF16), jnp.concatenate([zs, av.astype(BF16)], axis=1))
        rk = bdot(a_rk, vs)
        yield
        ry = bdot(a_rb, x.astype(BF16))
        yield
        rz = rt + unstack(ry[:, :W])
        y0 = unstack(ry[:, W:] + rk)
        lhs = jnp.concatenate([bc * w_rest, kd * w_rest], axis=0)
        rhs = jnp.concatenate([unstack(x), jnp.concatenate([jnp.zeros((C, W), F32), vc], axis=1)], axis=0)
        mn = _mm_tn(lhs, rhs)
        mt = jnp.where(bd, mn[:, :W], 0.0) + jnp.where(eye_w, jnp.exp(tot), 0.0)
        return rows, rz.astype(BF16), y0, mt.astype(BF16), jnp.where(bd, mn[:, W:], 0.0)

    def advance(j, local):
        rows, rz, y0, mt, nt = local
        s_prev = st_s[j].astype(BF16)
        y_s[rows, :] += bdot(rz, s_prev) + y0
        st_s[j] = bdot(mt, s_prev) + nt

    def run_interleaved(gens):
        results = [None] * len(gens)
        pending = list(range(len(gens)))
        while pending:
            for idx in list(pending):
                try:
                    next(gens[idx])
                except StopIteration as done:
                    results[idx] = done.value
                    pending.remove(idx)
        return results

    def step(i, carry):
        n = RWKV_CHUNKS_PER_STEP
        local = run_interleaved([chunk(0, n * i + u) for u in range(n)]
                                + [chunk(1, n_chunks - 1 - (n * i + u)) for u in range(n)])
        for u in range(n):
            advance(0, local[u])
            advance(1, local[n + u])
        return carry

    lax.fori_loop(0, n_chunks // RWKV_CHUNKS_PER_STEP, step, 0)

    if not latent:
        sfin_ref[...] = st_s[...]

    def finish(i, carry):
        rows = pl.ds(pl.multiple_of(i * RWKV_PREP_ROWS, RWKV_PREP_ROWS), RWKV_PREP_ROWS)
        y = y_s[rows, :]
        mu = _mm_split(y, gsum) * (1.0 / HEAD_DIM)
        d = y - mu
        var = _mm_split(d * d, gsum) * (1.0 / HEAD_DIM)
        yn = d * lax.rsqrt(var + GN_EPS) * lnw_ref[...] + lnb_ref[...]
        r, k, v = r_ref[rows, :], k_ref[rows, :], v_ref[rows, :]
        bonus = _mm_split(r * k * rk_ref[...], gsum) * v
        o_ref[rows, :] = (yn + bonus) * g_s[rows, :]
        return carry

    lax.fori_loop(0, seq // RWKV_PREP_ROWS, finish, 0)


def _rwkv(proj, tok_block0, p, batch, seq, s0=None):
    latent = s0 is not None
    W = RWKV_WIDTH
    col = lambda c: pl.BlockSpec((seq, COLBLK), lambda b: (tok_block0 + b, c))
    const = lambda shape: pl.BlockSpec(shape, lambda b: (0,) * len(shape))
    in_specs = [col(5), col(6), col(7), col(8), const((COLBLK, 5 * W)), const((1, 5 * W))] + [const((1, W))] * 5
    args = [proj, proj, proj, proj, p["lrw"], p["lr_bias"], p["k_k"], p["k_a"], p["r_k"], p["ln_w"], p["ln_b"]]
    out_specs = [pl.BlockSpec((seq, W), lambda b: (b, 0))]
    out_shape = [jax.ShapeDtypeStruct((batch * seq, W), F32)]
    if latent:
        in_specs.append(pl.BlockSpec((None, 2, W, W), lambda b: (b, 0, 0, 0)))
        args.append(s0)
    else:
        out_specs.append(pl.BlockSpec((None, 2, W, W), lambda b: (b, 0, 0, 0)))
        out_shape.append(jax.ShapeDtypeStruct((batch, 2, W, W), F32))
    res = pl.pallas_call(
        functools.partial(_rwkv_kernel, latent, seq),
        grid=(batch,),
        in_specs=in_specs,
        out_specs=out_specs,
        out_shape=out_shape,
        scratch_shapes=[
            pltpu.VMEM((seq, 2 * W), F32),
            pltpu.VMEM((seq, 2 * W), F32),
            pltpu.VMEM((seq, 2 * W), F32),
            pltpu.VMEM((seq, W), F32),
            pltpu.VMEM((seq, W), F32),
            pltpu.VMEM((seq, W), F32),
            pltpu.VMEM((2, W, W), F32),
        ],
        compiler_params=_params("parallel"),
        name="rwkv_latent" if latent else "rwkv_ctx",
    )(*args)
    return res if not latent else res[0]


def _post_prologue(x_ref, mix_refs, wout_ref, n2_ref, g1_ref, sh_ref, sc_ref, xn_s):
    mix = jnp.zeros(x_ref.shape, F32)
    for i, m_ref in enumerate(mix_refs):
        mix = mix + jnp.dot(m_ref[...].astype(BF16), wout_ref[i * COLBLK:(i + 1) * COLBLK, :],
                            preferred_element_type=F32)
    xn = x_ref[...] + g1_ref[...] * mix
    xn_s[...] = xn
    return _rms(xn, n2_ref[...]) * (1.0 + sc_ref[...]) + sh_ref[...]


def _post_dense_kernel(x_ref, a_ref, c_ref, rw_ref, f_ref, wout_ref, n2_ref, g1_ref, sh_ref, sc_ref, g2_ref,
                       w1_ref, w3_ref, w2_ref, o_ref, xn_s, h_s, acc_s):
    f = pl.program_id(1)

    @pl.when(f == 0)
    def _():
        h = _post_prologue(x_ref, (a_ref, c_ref, rw_ref, f_ref), wout_ref, n2_ref, g1_ref, sh_ref, sc_ref, xn_s)
        h_s[...] = h.astype(BF16)
        acc_s[...] = jnp.zeros(acc_s.shape, F32)

    h = h_s[...]
    t = _silu(jnp.dot(h, w1_ref[...], preferred_element_type=F32)) * jnp.dot(h, w3_ref[...],
                                                                            preferred_element_type=F32)
    acc_s[...] += jnp.dot(t.astype(BF16), w2_ref[...], preferred_element_type=F32)

    @pl.when(f == pl.num_programs(1) - 1)
    def _():
        o_ref[...] = xn_s[...] + g2_ref[...] * acc_s[...]


def _post_moe_kernel(x_ref, a_ref, c_ref, rw_ref, f_ref, wout_ref, n2_ref, g1_ref, sh_ref, sc_ref, g2_ref,
                     router_ref, w1_ref, w3_ref, w2_ref, o_ref, xn_s, h_s, acc_s, gate_s, rank_s, cnt_s):
    e = pl.program_id(1)
    lane = _iota((1, LANES), 1)
    tile = x_ref.shape[0]

    @pl.when(e == 0)
    def _():
        h = _post_prologue(x_ref, (a_ref, c_ref, rw_ref, f_ref), wout_ref, n2_ref, g1_ref, sh_ref, sc_ref, xn_s)
        h_s[...] = h.astype(BF16)
        acc_s[...] = jnp.zeros(acc_s.shape, F32)
        logits = jnp.dot(h, router_ref[...], precision=lax.Precision.HIGHEST, preferred_element_type=F32)
        logits = jnp.where(lane < N_EXPERTS, logits, NEG_INF)
        ex = jnp.exp(logits - jnp.max(logits, axis=-1, keepdims=True))
        probs = ex / jnp.sum(ex, axis=-1, keepdims=True)
        p1 = jnp.max(probs, axis=-1, keepdims=True)
        i1 = jnp.min(jnp.where(probs == p1, lane, LANES), axis=-1, keepdims=True)
        rest = jnp.where(lane == i1, -1.0, probs)
        p2 = jnp.max(rest, axis=-1, keepdims=True)
        i2 = jnp.min(jnp.where(rest == p2, lane, LANES), axis=-1, keepdims=True)
        top = p1 + p2
        gate_s[...] = jnp.where(lane == i1, p1 / top, jnp.where(lane == i2, p2 / top, 0.0))
        sel_t = jnp.where((lane == i1) | (lane == i2), 1.0, 0.0).T[:N_EXPERTS]
        before = _iota((tile, tile), 0) < _iota((tile, tile), 1)
        rank = jnp.dot(sel_t.astype(BF16), jnp.where(before, 1.0, 0.0).astype(BF16), preferred_element_type=F32)
        rank_s[...] = jnp.where(sel_t > 0.0, rank, -1.0)
        for ex_i in range(N_EXPERTS):
            cnt_s[ex_i] = jnp.sum(sel_t[ex_i:ex_i + 1, :]).astype(jnp.int32)

    gate = jnp.sum(jnp.where(lane == e, gate_s[...], 0.0), axis=-1, keepdims=True)
    routed = cnt_s[e]

    def swiglu(hb):
        t = _silu(jnp.dot(hb, w1_ref[...], preferred_element_type=F32)) * jnp.dot(hb, w3_ref[...],
                                                                                 preferred_element_type=F32)
        return jnp.dot(t.astype(BF16), w2_ref[...], preferred_element_type=F32)

    @pl.when(routed <= MOE_CAPACITY)
    def _():
        slot = _iota((MOE_CAPACITY, tile), 0).astype(F32)
        pick = jnp.where(rank_s[pl.ds(e, 1), :] == slot, 1.0, 0.0)
        y = swiglu(jnp.dot(pick.astype(BF16), h_s[...], preferred_element_type=F32).astype(BF16))
        acc_s[...] += gate * _mm_split(pick.T.astype(BF16), y, split_rhs=True)

    @pl.when(routed > MOE_CAPACITY)
    def _():
        acc_s[...] += gate * swiglu(h_s[...])

    @pl.when(e == pl.num_programs(1) - 1)
    def _():
        o_ref[...] = xn_s[...] + g2_ref[...] * acc_s[...]


def _post(x, mixes, w_out, norm_w, mod, row_of_tile, ffn, moe):
    n_tok = x.shape[0]
    tile = lambda width: pl.BlockSpec((TOKEN_TILE, width), lambda i, f: (i, 0))
    const = lambda shape: pl.BlockSpec(shape, lambda i, f: (0, 0))
    in_specs = [tile(D_MODEL)] + [tile(COLBLK)] * 4 + [const((MIX_WIDTH, D_MODEL)), const((1, D_MODEL)),
                                                       _mod_spec(2, row_of_tile), _mod_spec(3, row_of_tile),
                                                       _mod_spec(4, row_of_tile), _mod_spec(5, row_of_tile)]
    args = [x, *mixes, w_out, norm_w, mod, mod, mod, mod]
    scratch = [pltpu.VMEM((TOKEN_TILE, D_MODEL), F32), pltpu.VMEM((TOKEN_TILE, D_MODEL), BF16),
               pltpu.VMEM((TOKEN_TILE, D_MODEL), F32)]
    if moe:
        router, w1, w3, w2 = ffn
        in_specs += [const((D_MODEL, LANES)),
                     pl.BlockSpec((None, D_MODEL, D_FF_EXPERT), lambda i, e: (e, 0, 0)),
                     pl.BlockSpec((None, D_MODEL, D_FF_EXPERT), lambda i, e: (e, 0, 0)),
                     pl.BlockSpec((None, D_FF_EXPERT, D_MODEL), lambda i, e: (e, 0, 0))]
        args += [router, w1, w3, w2]
        scratch += [pltpu.VMEM((TOKEN_TILE, LANES), F32), pltpu.VMEM((N_EXPERTS, TOKEN_TILE), F32),
                    pltpu.SMEM((N_EXPERTS,), jnp.int32)]
        body, steps, name = _post_moe_kernel, N_EXPERTS, "post_moe"
    else:
        w1, w3, w2 = ffn
        in_specs += [pl.BlockSpec((D_MODEL, FF_TILE), lambda i, f: (0, f)),
                     pl.BlockSpec((D_MODEL, FF_TILE), lambda i, f: (0, f)),
                     pl.BlockSpec((FF_TILE, D_MODEL), lambda i, f: (f, 0))]
        args += [w1, w3, w2]
        body, steps, name = _post_dense_kernel, D_FF // FF_TILE, "post_dense"
    return pl.pallas_call(
        body,
        grid=(n_tok // TOKEN_TILE, steps),
        in_specs=in_specs,
        out_specs=tile(D_MODEL),
        out_shape=jax.ShapeDtypeStruct((n_tok, D_MODEL), F32),
        scratch_shapes=scratch,
        compiler_params=_params("parallel", "arbitrary"),
        name=name,
    )(*args)


def _rope_tables(n_tokens, n_heads):
    n_rows = n_tokens // GRID_W
    row = jnp.repeat(jnp.arange(n_rows), GRID_W).astype(F32)
    col = jnp.tile(jnp.arange(GRID_W), n_rows).astype(F32)
    quarter = HEAD_DIM // 4
    inv_freq = ROPE_BASE ** (-jnp.arange(quarter, dtype=F32) / quarter)
    ang_r, ang_c = row[:, None] * inv_freq, col[:, None] * inv_freq
    cos = jnp.concatenate([jnp.cos(ang_r)] * 2 + [jnp.cos(ang_c)] * 2, axis=1)
    sin = jnp.concatenate([-jnp.sin(ang_r), jnp.sin(ang_r), -jnp.sin(ang_c), jnp.sin(ang_c)], axis=1)
    return jnp.tile(cos, (1, n_heads)), jnp.tile(sin, (1, n_heads))


def _layout_w_in(w):
    lr_end = ATTN_WIDTH + 2 * KV_WIDTH + 3 * CONV_CH + 3 * RWKV_WIDTH + LOWRANK_WIDTH
    pad = jnp.zeros((w.shape[0], COLBLK - LOWRANK_WIDTH), w.dtype)
    return jnp.concatenate([w[:, :lr_end], pad, w[:, lr_end:]], axis=1).astype(BF16)


def _layout_rwkv(l, w0, w_up, a0, a_up, g_up, k_k, k_a, r_k, ln_w, ln_b):
    W = RWKV_WIDTH
    lrw = jnp.zeros((COLBLK, 5 * W), F32)
    for j in range(2):
        lrw = lrw.at[j * DECAY_RANK:(j + 1) * DECAY_RANK, j * W:(j + 1) * W].set(w_up[l, j])
        r0 = 2 * DECAY_RANK + j * ICLR_RANK
        lrw = lrw.at[r0:r0 + ICLR_RANK, (2 + j) * W:(3 + j) * W].set(a_up[l, j])
    r0 = 2 * DECAY_RANK + 2 * ICLR_RANK
    lrw = lrw.at[r0:r0 + GATE_RANK, 4 * W:].set(g_up[l])
    bias = jnp.concatenate([w0[l, 0], w0[l, 1], a0[l, 0], a0[l, 1], jnp.zeros((W,), F32)])[None, :]
    row = lambda t: t[l].reshape(1, W)
    return {"lrw": lrw.astype(BF16), "lr_bias": bias, "k_k": row(k_k), "k_a": row(k_a), "r_k": row(r_k),
            "ln_w": row(ln_w), "ln_b": row(ln_b)}


def _state_to_blockdiag(s):
    eye = jnp.eye(RWKV_HEADS, dtype=s.dtype)
    bd = jnp.einsum("bjhvk,hg->bjhkgv", s, eye)
    return bd.reshape(s.shape[0], 2, RWKV_WIDTH, RWKV_WIDTH)


def _blockdiag_to_state(bd):
    b = bd.shape[0]
    t = bd.reshape(b, 2, RWKV_HEADS, HEAD_DIM, RWKV_HEADS, HEAD_DIM)
    diag = jnp.stack([t[:, :, h, :, h, :] for h in range(RWKV_HEADS)], axis=2)
    return jnp.swapaxes(diag, -1, -2)


def kernel(x_prompt, x_sample, cache_k, cache_v, state_wkv, c, c_ctx, mod_w, mod_b, norm1_w, norm2_w, w_in, q_norm_w, k_norm_w, attn_sink, conv_w, rwkv_w0, rwkv_w_up, rwkv_a0, rwkv_a_up, rwkv_g_up, rwkv_k_k, rwkv_k_a, rwkv_r_k, rwkv_ln_w, rwkv_ln_b, w_out, ffn_w1, ffn_w3, ffn_w2, router_w, moe_w1, moe_w3, moe_w2):
    batch, seq, _ = x_prompt.shape
    dec_batch, dec_seq, _ = x_sample.shape
    past = cache_k.shape[2]
    assert seq % BLOCK == 0 and dec_seq % TOKEN_TILE == 0 and (batch * seq) % TOKEN_TILE == 0
    assert 1 + dec_batch <= MOD_ROWS

    cond = jnp.concatenate([c_ctx[None, :], c, jnp.zeros((MOD_ROWS - 1 - dec_batch, D_MODEL), F32)], axis=0)
    mod_all = _modulation(cond, mod_w, mod_b)
    prompt_row = lambda i: 0
    sample_row = lambda i: 1 + (i * TOKEN_TILE) // dec_seq

    rope_q = _rope_tables(dec_seq, N_Q_HEADS)
    rope_k = _rope_tables(dec_seq, N_KV_HEADS)

    xp = x_prompt.reshape(batch * seq, D_MODEL)
    xs = x_sample.reshape(dec_batch * dec_seq, D_MODEL)
    new_k, new_v, new_s = [], [], []
    for l in range(DEPTH):
        mod = mod_all[l].reshape(MOD_ROWS, 6, 1, D_MODEL)
        w_in_l = _layout_w_in(w_in[l])
        n1 = norm1_w[l][None, :]
        n2 = norm2_w[l][None, :]
        qn = jnp.tile(q_norm_w[l], N_Q_HEADS)[None, :]
        kn = jnp.tile(k_norm_w[l], N_KV_HEADS)[None, :]
        cw = jnp.concatenate([conv_w[l], jnp.zeros((5, CONV_CH), F32)], axis=0)
        rp = _layout_rwkv(l, rwkv_w0, rwkv_w_up, rwkv_a0, rwkv_a_up, rwkv_g_up, rwkv_k_k, rwkv_k_a, rwkv_r_k,
                          rwkv_ln_w, rwkv_ln_b)
        w_out_l = w_out[l].astype(BF16)
        if l % 2 == 0:
            j = l // 2
            ffn = (ffn_w1[j].astype(BF16), ffn_w3[j].astype(BF16), ffn_w2[j].astype(BF16))
        else:
            j = l // 2
            router = jnp.concatenate([router_w[j], jnp.zeros((D_MODEL, LANES - N_EXPERTS), F32)], axis=1)
            ffn = (router, moe_w1[j].astype(BF16), moe_w3[j].astype(BF16), moe_w2[j].astype(BF16))

        proj_p = _in_proj(xp, n1, mod, prompt_row, w_in_l)
        attn_p, k_p, v_p = _ctx_attention(proj_p, qn, kn, attn_sink[l], batch, seq)
        conv_p, four_p = _convfour(proj_p, 0, cw, batch, seq)
        rw_p, sfin = _rwkv(proj_p, 0, rp, batch, seq)
        xp = _post(xp, (attn_p, conv_p, rw_p, four_p), w_out_l, n2, mod, prompt_row, ffn, l % 2 == 1)
        new_k.append(k_p.reshape(batch, seq, N_KV_HEADS, HEAD_DIM))
        new_v.append(v_p.reshape(batch, seq, N_KV_HEADS, HEAD_DIM))
        new_s.append(_blockdiag_to_state(sfin))

        proj_s = _in_proj(xs, n1, mod, sample_row, w_in_l)
        kc = cache_k[:, l].reshape(dec_batch, past, KV_WIDTH)
        vc = cache_v[:, l].reshape(dec_batch, past, KV_WIDTH)
        attn_s = _lat_attention(proj_s, 0, kc, vc, qn, kn, rope_q, rope_k, attn_sink[l], dec_batch, dec_seq, past)
        conv_s, four_s = _convfour(proj_s, 0, cw, dec_batch, dec_seq)
        rw_s = _rwkv(proj_s, 0, rp, dec_batch, dec_seq, s0=_state_to_blockdiag(state_wkv[:, l]))
        xs = _post(xs, (attn_s, conv_s, rw_s, four_s), w_out_l, n2, mod, sample_row, ffn, l % 2 == 1)

    return (xp.reshape(batch, seq, D_MODEL), xs.reshape(dec_batch, dec_seq, D_MODEL),
            jnp.stack(new_k, axis=1), jnp.stack(new_v, axis=1), jnp.stack(new_s, axis=1))
```

```python
import functools
import math

import jax
import jax.numpy as jnp
from jax import lax
from jax.experimental import pallas as pl
from jax.experimental.pallas import tpu as pltpu

F32 = jnp.float32
BF16 = jnp.bfloat16

D_MODEL = 1024
DEPTH = 2
GRID_W = 64
HEAD_DIM = 64
N_Q_HEADS = 4
N_KV_HEADS = 2
WINDOW = 128
BLOCK = 128
ROPE_BASE = 10000.0
ATTN_WIDTH = N_Q_HEADS * HEAD_DIM
KV_WIDTH = N_KV_HEADS * HEAD_DIM
CONV_CH = 256
RWKV_HEADS = 4
RWKV_WIDTH = RWKV_HEADS * HEAD_DIM
DECAY_RANK = 32
ICLR_RANK = 32
GATE_RANK = 64
FOURIER_GROUPS = 4
FOURIER_WIDTH = FOURIER_GROUPS * HEAD_DIM
MIX_WIDTH = ATTN_WIDTH + CONV_CH + RWKV_WIDTH + FOURIER_WIDTH
D_FF = 2816
N_EXPERTS = 8
D_FF_EXPERT = 1024
NORM_EPS = 1e-6
GN_EPS = 64e-5
NEG_INF = -1e30

LANES = 128
VMEM_LIMIT_BYTES = 56 * 1024 * 1024

PROJ_WIDTH = 2560
COLBLK = 256
LOWRANK_WIDTH = 2 * DECAY_RANK + 2 * ICLR_RANK + GATE_RANK
MOD_ROWS = 16
TOKEN_TILE = 512
RWKV_CHUNK = 64
RWKV_CHUNKS_PER_STEP = 4
RWKV_PREP_ROWS = 256
FF_TILE = 1408
MOE_TOKEN_TILE = 1024
MOE_SUB_TILE = 512
MOE_CAPACITIES = (96, 160, 224)
DECAY_SCALE = -math.exp(-0.5)


def _params(*sem):
    return pltpu.CompilerParams(dimension_semantics=sem, vmem_limit_bytes=VMEM_LIMIT_BYTES)


def _mm(a, b):
    return jnp.dot(a.astype(BF16), b.astype(BF16), preferred_element_type=F32)


def _mm_nt(a, b):
    return lax.dot_general(a.astype(BF16), b.astype(BF16), (((1,), (1,)), ((), ())), preferred_element_type=F32)


def _mm_tn(a, b):
    return jnp.dot(a.T.astype(BF16), b.astype(BF16), preferred_element_type=F32)


def _mm_split(a, b, split_rhs=False, parts=2):
    exact, x = (a, b) if split_rhs else (b, a)
    acc = None
    for _ in range(parts):
        piece = x.astype(BF16)
        x = x - piece.astype(F32)
        term = (jnp.dot(exact, piece, preferred_element_type=F32) if split_rhs
                else jnp.dot(piece, exact, preferred_element_type=F32))
        acc = term if acc is None else acc + term
    return acc


def _sigmoid(x):
    return 0.5 * jnp.tanh(0.5 * x) + 0.5


def _silu(x):
    return x * _sigmoid(x)


def _iota(shape, dim):
    return lax.broadcasted_iota(jnp.int32, shape, dim)


def _mod_kernel(c_ref, w_ref, b_ref, o_ref):
    o_ref[0] = _mm(_silu(c_ref[...]), w_ref[0]) + b_ref[0]


def _modulation(cond, mod_w, mod_b):
    n_chunks = mod_w.shape[-1] // D_MODEL
    return pl.pallas_call(
        _mod_kernel,
        grid=(DEPTH, n_chunks),
        in_specs=[
            pl.BlockSpec((MOD_ROWS, D_MODEL), lambda l, j: (0, 0)),
            pl.BlockSpec((1, D_MODEL, D_MODEL), lambda l, j: (l, 0, j)),
            pl.BlockSpec((1, 1, D_MODEL), lambda l, j: (l, 0, j)),
        ],
        out_specs=pl.BlockSpec((1, MOD_ROWS, D_MODEL), lambda l, j: (l, 0, j)),
        out_shape=jax.ShapeDtypeStruct((DEPTH, MOD_ROWS, n_chunks * D_MODEL), F32),
        compiler_params=_params("parallel", "parallel"),
        name="modulation",
    )(cond, mod_w, mod_b.reshape(DEPTH, 1, -1))


def _mod_spec(chunk, row_of_tile):
    return pl.BlockSpec((None, None, 1, D_MODEL), lambda i, *_: (row_of_tile(i), chunk, 0, 0))


def _rms(x, w):
    return x * lax.rsqrt(jnp.mean(x * x, axis=-1, keepdims=True) + NORM_EPS) * w


def _in_proj_kernel(x_ref, nw_ref, sh_ref, sc_ref, w_ref, o_ref):
    h = _rms(x_ref[...], nw_ref[...]) * (1.0 + sc_ref[...]) + sh_ref[...]
    o_ref[...] = jnp.dot(h.astype(BF16), w_ref[...], preferred_element_type=F32)


def _in_proj(x, norm_w, mod, row_of_tile, w_in):
    row_of_tile = functools.partial(row_of_tile, TOKEN_TILE)
    n_tok = x.shape[0]
    return pl.pallas_call(
        _in_proj_kernel,
        grid=(n_tok // TOKEN_TILE,),
        in_specs=[
            pl.BlockSpec((TOKEN_TILE, D_MODEL), lambda i: (i, 0)),
            pl.BlockSpec((1, D_MODEL), lambda i: (0, 0)),
            _mod_spec(0, row_of_tile),
            _mod_spec(1, row_of_tile),
            pl.BlockSpec((D_MODEL, PROJ_WIDTH), lambda i: (0, 0)),
        ],
        out_specs=pl.BlockSpec((TOKEN_TILE, PROJ_WIDTH), lambda i: (i, 0)),
        out_shape=jax.ShapeDtypeStruct((n_tok, PROJ_WIDTH), F32),
        compiler_params=_params("parallel"),
        name="in_proj",
    )(x, norm_w, mod, mod, w_in)


def _group_sum_matrix(width):
    r = _iota((width, width), 0) // HEAD_DIM
    c = _iota((width, width), 1) // HEAD_DIM
    return jnp.where(r == c, 1.0, 0.0).astype(BF16)


def _head_rms(x, w):
    ms = _mm_split(x * x, _group_sum_matrix(x.shape[-1])) * (1.0 / HEAD_DIM)
    return x * lax.rsqrt(ms + NORM_EPS) * w


def _dup_kv_head(x, g):
    lane = _iota((1, KV_WIDTH), 1)
    rolled = pltpu.roll(x, HEAD_DIM, 1)
    first = lane < HEAD_DIM
    return jnp.where(first, x, rolled) if g == 0 else jnp.where(first, rolled, x)


def _half_mask(j):
    lane = _iota((1, KV_WIDTH), 1)
    return (lane >= j * HEAD_DIM) & (lane < (j + 1) * HEAD_DIM)


def _ctx_attn_kernel(qkv_ref, qw_ref, kw_ref, sink_ref, o_ref, k_ref, v_ref):
    qkv = qkv_ref[...]
    q = _head_rms(qkv[:, :ATTN_WIDTH], qw_ref[...]) * HEAD_DIM ** -0.5
    k = _head_rms(qkv[:, ATTN_WIDTH:ATTN_WIDTH + KV_WIDTH], kw_ref[...])
    v = qkv[:, ATTN_WIDTH + KV_WIDTH:]
    k_ref[...] = k
    v_ref[...] = v
    for g in range(N_KV_HEADS):
        qg = q[:, g * KV_WIDTH:(g + 1) * KV_WIDTH]
        kg = _dup_kv_head(k, g).astype(BF16)
        vg = _dup_kv_head(v, g)
        acc = jnp.zeros(qg.shape, F32)
        for j in range(2):
            mj = _half_mask(j)
            s = sink_ref[2 * g + j]
            logits = _mm_nt(jnp.where(mj, qg, 0.0), kg)
            m = jnp.maximum(jnp.max(logits, axis=-1, keepdims=True), s)
            e = jnp.exp(logits - m)
            den = jnp.sum(e, axis=-1, keepdims=True) + jnp.exp(s - m)
            acc = acc + _mm(e, jnp.where(mj, vg, 0.0)) / den
        o_ref[:, g * KV_WIDTH:(g + 1) * KV_WIDTH] = acc


def _ctx_attention(proj, q_norm, k_norm, sink, batch, seq):
    blk = ATTN_WIDTH + 2 * KV_WIDTH
    return pl.pallas_call(
        _ctx_attn_kernel,
        grid=(batch,),
        in_specs=[
            pl.BlockSpec((seq, blk), lambda b: (b, 0)),
            pl.BlockSpec((1, ATTN_WIDTH), lambda b: (0, 0)),
            pl.BlockSpec((1, KV_WIDTH), lambda b: (0, 0)),
            pl.BlockSpec(memory_space=pltpu.SMEM),
        ],
        out_specs=[
            pl.BlockSpec((seq, ATTN_WIDTH), lambda b: (b, 0)),
            pl.BlockSpec((seq, KV_WIDTH), lambda b: (b, 0)),
            pl.BlockSpec((seq, KV_WIDTH), lambda b: (b, 0)),
        ],
        out_shape=[
            jax.ShapeDtypeStruct((batch * seq, ATTN_WIDTH), F32),
            jax.ShapeDtypeStruct((batch * seq, KV_WIDTH), F32),
            jax.ShapeDtypeStruct((batch * seq, KV_WIDTH), F32),
        ],
        compiler_params=_params("parallel"),
        name="ctx_attention",
    )(proj, q_norm, k_norm, sink)


def _rope(x, cos, sin_signed):
    width = x.shape[-1]
    lane = _iota((1, width), 1)
    first_half = (lane % (HEAD_DIM // 2)) < (HEAD_DIM // 4)
    partner = jnp.where(first_half, pltpu.roll(x, width - HEAD_DIM // 4, 1), pltpu.roll(x, HEAD_DIM // 4, 1))
    return x * cos + partner * sin_signed


def _lat_attn_kernel(seq, past, qkv_ref, kc_ref, vc_ref, qw_ref, kw_ref, cosq_ref, snq_ref, cosk_ref, snk_ref,
                     sink_ref, o_ref, q_s, k_s, v_s):
    qkv = qkv_ref[...]
    q = _rope(_head_rms(qkv[:, :ATTN_WIDTH], qw_ref[...]), cosq_ref[...], snq_ref[...]) * HEAD_DIM ** -0.5
    k = _rope(_head_rms(qkv[:, ATTN_WIDTH:ATTN_WIDTH + KV_WIDTH], kw_ref[...]), cosk_ref[...], snk_ref[...])
    v = qkv[:, ATTN_WIDTH + KV_WIDTH:]
    kc = kc_ref[...]
    vc = vc_ref[...]
    q_s[...] = q
    zero_blk = jnp.zeros((BLOCK, KV_WIDTH), BF16)
    for g in range(N_KV_HEADS):
        k_s[g, 0:BLOCK] = zero_blk
        k_s[g, BLOCK:BLOCK + seq] = _dup_kv_head(k, g).astype(BF16)
        k_s[g, BLOCK + seq:2 * BLOCK + seq] = zero_blk
        k_s[g, 2 * BLOCK + seq:] = _dup_kv_head(kc, g).astype(BF16)
        vg = _dup_kv_head(v, g)
        vcg = _dup_kv_head(vc, g)
        for j in range(2):
            mj = _half_mask(j)
            v_s[2 * g + j, 0:BLOCK] = zero_blk
            v_s[2 * g + j, BLOCK:BLOCK + seq] = jnp.where(mj, vg, 0.0).astype(BF16)
            v_s[2 * g + j, BLOCK + seq:2 * BLOCK + seq] = zero_blk
            v_s[2 * g + j, 2 * BLOCK + seq:] = jnp.where(mj, vcg, 0.0).astype(BF16)

    win = 3 * BLOCK
    ctx0 = 2 * BLOCK + seq

    def q_block(n, carry):
        q0 = pl.multiple_of(n * BLOCK, BLOCK)
        qpos = n * BLOCK + _iota((BLOCK, win), 0)
        kpos = (n - 1) * BLOCK + _iota((BLOCK, win), 1)
        valid = (jnp.abs(qpos - kpos) <= WINDOW) & (kpos >= 0) & (kpos < seq)
        for g in range(N_KV_HEADS):
            qg = q_s[pl.ds(q0, BLOCK), g * KV_WIDTH:(g + 1) * KV_WIDTH]
            kw = k_s[g, pl.ds(q0, win), :]
            kcx = k_s[g, ctx0:ctx0 + past, :]
            acc = jnp.zeros((BLOCK, KV_WIDTH), F32)
            for j in range(2):
                h = 2 * g + j
                s = sink_ref[h]
                qm = jnp.where(_half_mask(j), qg, 0.0)
                lw = jnp.where(valid, _mm_nt(qm, kw), NEG_INF)
                lc = _mm_nt(qm, kcx)
                m = jnp.maximum(jnp.maximum(jnp.max(lw, axis=-1, keepdims=True),
                                            jnp.max(lc, axis=-1, keepdims=True)), s)
                ew = jnp.exp(lw - m)
                ec = jnp.exp(lc - m)
                den = jnp.sum(ew, axis=-1, keepdims=True) + jnp.sum(ec, axis=-1, keepdims=True) + jnp.exp(s - m)
                pv = (jnp.dot(ew.astype(BF16), v_s[h, pl.ds(q0, win), :], preferred_element_type=F32)
                      + jnp.dot(ec.astype(BF16), v_s[h, ctx0:ctx0 + past, :], preferred_element_type=F32))
                acc = acc + pv / den
            o_ref[pl.ds(q0, BLOCK), g * KV_WIDTH:(g + 1) * KV_WIDTH] = acc
        return carry

    lax.fori_loop(0, seq // BLOCK, q_block, 0)


def _lat_attention(proj, tok_block0, cache_k, cache_v, q_norm, k_norm, rope_q, rope_k, sink, batch, seq, past):
    blk = ATTN_WIDTH + 2 * KV_WIDTH
    cos_q, sin_q = rope_q
    cos_k, sin_k = rope_k
    rows = 2 * BLOCK + seq + past
    const = lambda shape: pl.BlockSpec(shape, lambda b: (0, 0))
    return pl.pallas_call(
        functools.partial(_lat_attn_kernel, seq, past),
        grid=(batch,),
        in_specs=[
            pl.BlockSpec((seq, blk), lambda b: (tok_block0 + b, 0)),
            pl.BlockSpec((None, past, KV_WIDTH), lambda b: (b, 0, 0)),
            pl.BlockSpec((None, past, KV_WIDTH), lambda b: (b, 0, 0)),
            const((1, ATTN_WIDTH)),
            const((1, KV_WIDTH)),
            const((seq, ATTN_WIDTH)),
            const((seq, ATTN_WIDTH)),
            const((seq, KV_WIDTH)),
            const((seq, KV_WIDTH)),
            pl.BlockSpec(memory_space=pltpu.SMEM),
        ],
        out_specs=pl.BlockSpec((seq, ATTN_WIDTH), lambda b: (b, 0)),
        out_shape=jax.ShapeDtypeStruct((batch * seq, ATTN_WIDTH), F32),
        scratch_shapes=[
            pltpu.VMEM((seq, ATTN_WIDTH), F32),
            pltpu.VMEM((N_KV_HEADS, rows, KV_WIDTH), BF16),
            pltpu.VMEM((N_Q_HEADS, rows, KV_WIDTH), BF16),
        ],
        compiler_params=_params("parallel"),
        name="latent_attention",
    )(proj, cache_k, cache_v, q_norm, k_norm, cos_q, sin_q, cos_k, sin_k, sink)


def _convfour_kernel(seq, bx_ref, bb_ref, bc_ref, dx_ref, cw_ref, ct_ref, st_ref, cc_ref, sc_ref, conv_ref, four_ref):
    u = bc_ref[...] * bx_ref[...]
    row = _iota((seq, 1), 0)
    prev = jnp.where(row == 0, 0.0, pltpu.roll(u, 1, 0))
    nxt = jnp.where(row == seq - 1, 0.0, pltpu.roll(u, seq - 1, 0))
    cw = cw_ref[...]
    conv_ref[...] = bb_ref[...] * (cw[0:1] * prev + cw[1:2] * u + cw[2:3] * nxt)
    x = dx_ref[...].astype(BF16)
    a = jnp.dot(ct_ref[...], x, preferred_element_type=F32)
    b = jnp.dot(st_ref[...], x, preferred_element_type=F32)
    y = (jnp.dot(a.astype(BF16), cc_ref[...], preferred_element_type=F32)
         - jnp.dot(b.astype(BF16), sc_ref[...], preferred_element_type=F32))
    four_ref[...] = y * (1.0 / math.sqrt(seq * HEAD_DIM))


def _dft_tables(n):
    idx = jnp.arange(n, dtype=jnp.int32)
    ang = ((idx[:, None] * idx[None, :]) % n).astype(F32) * (2.0 * math.pi / n)
    return jnp.cos(ang), jnp.sin(ang)


def _convfour(proj, tok_block0, conv_w, batch, seq):
    ct, st = _dft_tables(seq)
    c64, s64 = _dft_tables(HEAD_DIM)
    eye = jnp.eye(FOURIER_GROUPS, dtype=F32)
    cc, sc = jnp.kron(eye, c64), jnp.kron(eye, s64)
    col = lambda c: pl.BlockSpec((seq, COLBLK), lambda b: (tok_block0 + b, c))
    const = lambda shape: pl.BlockSpec(shape, lambda b: (0, 0))
    out = pl.BlockSpec((seq, COLBLK), lambda b: (b, 0))
    return pl.pallas_call(
        functools.partial(_convfour_kernel, seq),
        grid=(batch,),
        in_specs=[col(2), col(3), col(4), col(9), const((8, CONV_CH)), const((seq, seq)), const((seq, seq)),
                  const((FOURIER_WIDTH, FOURIER_WIDTH)), const((FOURIER_WIDTH, FOURIER_WIDTH))],
        out_specs=[out, out],
        out_shape=[jax.ShapeDtypeStruct((batch * seq, COLBLK), F32)] * 2,
        compiler_params=_params("parallel"),
        name="conv_fourier",
    )(proj, proj, proj, proj, conv_w, ct.astype(BF16), st.astype(BF16), cc.astype(BF16), sc.astype(BF16))


def _rwkv_kernel(latent, seq, *refs):
    if latent:
        (r_ref, k_ref, v_ref, lr_ref, lrw_ref, bias_ref, kk_ref, ka_ref, rk_ref, lnw_ref, lnb_ref, s0_ref,
         o_ref, logw_s, kd_s, b_s, z_s, g_s, y_s, st_s) = refs
    else:
        (r_ref, k_ref, v_ref, lr_ref, lrw_ref, bias_ref, kk_ref, ka_ref, rk_ref, lnw_ref, lnb_ref,
         o_ref, sfin_ref, logw_s, kd_s, b_s, z_s, g_s, y_s, st_s) = refs
    W = RWKV_WIDTH
    C = RWKV_CHUNK
    gsum = _group_sum_matrix(W)
    lane = _iota((1, W), 1)

    def prep(i, carry):
        rows = pl.ds(pl.multiple_of(i * RWKV_PREP_ROWS, RWKV_PREP_ROWS), RWKV_PREP_ROWS)
        lr = lr_ref[rows, :]
        f = jnp.where(lane < 2 * DECAY_RANK, jnp.tanh(lr),
                      jnp.where(lane < 2 * DECAY_RANK + 2 * ICLR_RANK, lr, _sigmoid(lr)))
        low = jnp.dot(f.astype(BF16), lrw_ref[...], preferred_element_type=F32) + bias_ref[...]
        logw_s[rows, :] = DECAY_SCALE * _sigmoid(low[:, :2 * W])
        a = _sigmoid(low[:, 2 * W:4 * W])
        g_s[rows, :] = low[:, 4 * W:]
        k = k_ref[rows, :]
        kk = k * kk_ref[...]
        kk = kk / jnp.maximum(jnp.sqrt(_mm_split(kk * kk, gsum)), 1e-12)
        z_s[rows, :] = -kk
        for j in range(2):
            aj = a[:, j * W:(j + 1) * W]
            kd_s[rows, j * W:(j + 1) * W] = k * (1.0 + (aj - 1.0) * ka_ref[...])
            b_s[rows, j * W:(j + 1) * W] = kk * aj
        return carry

    lax.fori_loop(0, seq // RWKV_PREP_ROWS, prep, 0)

    y_s[...] = jnp.zeros((seq, W), F32)
    st_s[...] = jnp.zeros((2, W, W), F32)
    if latent:
        for j in range(2):
            for h in range(RWKV_HEADS):
                blk = slice(h * HEAD_DIM, (h + 1) * HEAD_DIM)
                st_s[j, blk, blk] = s0_ref[j, h].T

    HC = RWKV_HEADS * C
    wide_t = _iota((C, HC), 0)
    wide_s = _iota((C, HC), 1) % C
    eye_wide = jnp.where(wide_t == wide_s, 1.0, 0.0).astype(F32)
    same_head = (_iota((HC, HC), 0) // C) == (_iota((HC, HC), 1) // C)
    bd = (_iota((W, W), 0) // HEAD_DIM) == (_iota((W, W), 1) // HEAD_DIM)
    eye_w = _iota((W, W), 0) == _iota((W, W), 1)
    tc_i = _iota((C, C), 0)
    sc_i = _iota((C, C), 1)
    head_masks = [(lane >= h * HEAD_DIM) & (lane < (h + 1) * HEAD_DIM) for h in range(RWKV_HEADS)]
    n_chunks = seq // C

    def stack_masked(x):
        return jnp.concatenate([jnp.where(m, x, 0.0) for m in head_masks], axis=0).astype(BF16)

    def block_diag(x, mask):
        return jnp.where(mask, jnp.concatenate([x] * RWKV_HEADS, axis=0), 0.0)

    def bdot(a, b):
        return jnp.dot(a, b, preferred_element_type=F32)

    def chunk(j, c):
        rows = pl.ds(pl.multiple_of(c * C, C), C)
        cols = slice(j * W, (j + 1) * W)
        rc, vc, zc = r_ref[rows, :], v_ref[rows, :], z_s[rows, :]
        lw, kd, bc = logw_s[rows, cols], kd_s[rows, cols], b_s[rows, cols]
        if j == 0:
            strict, incl, tri = wide_s < wide_t, wide_s <= wide_t, sc_i <= tc_i
        else:
            strict, incl, tri = wide_s > wide_t, wide_s >= wide_t, sc_i >= tc_i
        cl = _mm_split(jnp.where(tri, 1.0, 0.0).astype(BF16), lw, split_rhs=True, parts=3)
        yield
        tot = cl[C - 1:C, :] if j == 0 else cl[0:1, :]
        w_in, w_ex, w_inv, w_rest = jnp.exp(cl), jnp.exp(cl - lw), jnp.exp(-cl), jnp.exp(tot - cl)
        rt = rc * w_in
        zs = stack_masked(zc * w_ex)
        vs = stack_masked(vc)
        aa = lax.dot_general(jnp.concatenate([zc * w_ex, rt], axis=0).astype(BF16),
                             jnp.concatenate([stack_masked(bc * w_inv), stack_masked(kd * w_inv)], axis=0),
                             (((1,), (1,)), ((), ())), preferred_element_type=F32)
        yield
        a_zb = jnp.where(strict, aa[:C, :HC], 0.0)
        a_zk = jnp.where(strict, aa[:C, HC:], 0.0).astype(BF16)
        a_rb = jnp.where(incl, aa[C:, :HC], 0.0).astype(BF16)
        a_rk = jnp.where(incl, aa[C:, HC:], 0.0).astype(BF16)
        tinv = eye_wide + a_zb
        apow = bdot(a_zb.astype(BF16), block_diag(a_zb, same_head).astype(BF16))
        av = bdot(a_zk, vs)
        rk = bdot(a_rk, vs)
        yield
        for _ in range(int(math.log2(C)) - 2):
            both = bdot(jnp.concatenate([apow, tinv], axis=0).astype(BF16), block_diag(apow, same_head).astype(BF16))
            yield
            apow = both[:C]
            tinv = tinv + both[C:]
        tinv = tinv + bdot(tinv.astype(BF16), block_diag(apow, same_head).astype(BF16))
        yield
        x = bdot(tinv.astype(BF16), jnp.concatenate([zs, stack_masked(av)], axis=1))
        yield
        ry = bdot(a_rb, jnp.concatenate([stack_masked(x[:, :W]), stack_masked(x[:, W:])], axis=1))
        lhs = jnp.concatenate([bc * w_rest, kd * w_rest], axis=0)
        rhs = jnp.concatenate([x, jnp.concatenate([jnp.zeros((C, W), F32), vc], axis=1)], axis=0)
        mn = _mm_tn(lhs, rhs)
        yield
        rz = rt + ry[:, :W]
        y0 = ry[:, W:] + rk
        mt = jnp.where(bd, mn[:, :W], 0.0) + jnp.where(eye_w, jnp.exp(tot), 0.0)
        mt_wide = mt[0:HEAD_DIM]
        for h in range(1, RWKV_HEADS):
            mt_wide = mt_wide + mt[h * HEAD_DIM:(h + 1) * HEAD_DIM]
        return rows, rz.astype(BF16), y0, mt_wide.astype(BF16), jnp.where(bd, mn[:, W:], 0.0)

    def advance(j, local):
        rows, rz, y0, mt_wide, nt = local
        s_prev = st_s[j].astype(BF16)
        y_s[rows, :] += bdot(rz, s_prev) + y0
        st_s[j] = block_diag(bdot(mt_wide, s_prev), bd) + nt

    def run_interleaved(gens):
        results = [None] * len(gens)
        pending = list(range(len(gens)))
        while pending:
            for idx in list(pending):
                try:
                    next(gens[idx])
                except StopIteration as done:
                    results[idx] = done.value
                    pending.remove(idx)
        return results

    def step(i, carry):
        n = RWKV_CHUNKS_PER_STEP
        local = run_interleaved([chunk(0, n * i + u) for u in range(n)]
                                + [chunk(1, n_chunks - 1 - (n * i + u)) for u in range(n)])
        for u in range(n):
            advance(0, local[u])
            advance(1, local[n + u])
        return carry

    lax.fori_loop(0, n_chunks // RWKV_CHUNKS_PER_STEP, step, 0)

    if not latent:
        for j in range(2):
            for h in range(RWKV_HEADS):
                blk = slice(h * HEAD_DIM, (h + 1) * HEAD_DIM)
                sfin_ref[j, h] = st_s[j, blk, blk].T

    def finish(i, carry):
        rows = pl.ds(pl.multiple_of(i * RWKV_PREP_ROWS, RWKV_PREP_ROWS), RWKV_PREP_ROWS)
        y = y_s[rows, :]
        mu = _mm_split(y, gsum) * (1.0 / HEAD_DIM)
        d = y - mu
        var = _mm_split(d * d, gsum) * (1.0 / HEAD_DIM)
        yn = d * lax.rsqrt(var + GN_EPS) * lnw_ref[...] + lnb_ref[...]
        r, k, v = r_ref[rows, :], k_ref[rows, :], v_ref[rows, :]
        bonus = _mm_split(r * k * rk_ref[...], gsum) * v
        o_ref[rows, :] = (yn + bonus) * g_s[rows, :]
        return carry

    lax.fori_loop(0, seq // RWKV_PREP_ROWS, finish, 0)


def _rwkv(proj, tok_block0, p, batch, seq, s0=None):
    latent = s0 is not None
    W = RWKV_WIDTH
    col = lambda c: pl.BlockSpec((seq, COLBLK), lambda b: (tok_block0 + b, c))
    const = lambda shape: pl.BlockSpec(shape, lambda b: (0,) * len(shape))
    in_specs = [col(5), col(6), col(7), col(8), const((COLBLK, 5 * W)), const((1, 5 * W))] + [const((1, W))] * 5
    args = [proj, proj, proj, proj, p["lrw"], p["lr_bias"], p["k_k"], p["k_a"], p["r_k"], p["ln_w"], p["ln_b"]]
    out_specs = [pl.BlockSpec((seq, W), lambda b: (b, 0))]
    out_shape = [jax.ShapeDtypeStruct((batch * seq, W), F32)]
    if latent:
        in_specs.append(pl.BlockSpec((None, 2, RWKV_HEADS, HEAD_DIM, HEAD_DIM), lambda b: (b, 0, 0, 0, 0)))
        args.append(s0)
    else:
        out_specs.append(pl.BlockSpec((None, 2, RWKV_HEADS, HEAD_DIM, HEAD_DIM), lambda b: (b, 0, 0, 0, 0)))
        out_shape.append(jax.ShapeDtypeStruct((batch, 2, RWKV_HEADS, HEAD_DIM, HEAD_DIM), F32))
    res = pl.pallas_call(
        functools.partial(_rwkv_kernel, latent, seq),
        grid=(batch,),
        in_specs=in_specs,
        out_specs=out_specs,
        out_shape=out_shape,
        scratch_shapes=[
            pltpu.VMEM((seq, 2 * W), F32),
            pltpu.VMEM((seq, 2 * W), F32),
            pltpu.VMEM((seq, 2 * W), F32),
            pltpu.VMEM((seq, W), F32),
            pltpu.VMEM((seq, W), F32),
            pltpu.VMEM((seq, W), F32),
            pltpu.VMEM((2, W, W), F32),
        ],
        compiler_params=_params("parallel"),
        name="rwkv_latent" if latent else "rwkv_ctx",
    )(*args)
    return res if not latent else res[0]


def _post_prologue(rows, x_ref, mix_refs, wout_ref, n2_ref, g1_ref, sh_ref, sc_ref):
    mix = None
    for i, m_ref in enumerate(mix_refs):
        part = jnp.dot(m_ref[rows, :].astype(BF16), wout_ref[i * COLBLK:(i + 1) * COLBLK, :],
                       preferred_element_type=F32)
        mix = part if mix is None else mix + part
    xn = x_ref[rows, :] + g1_ref[...] * mix
    return xn, _rms(xn, n2_ref[...]) * (1.0 + sc_ref[...]) + sh_ref[...]


def _swiglu(hb, w1_ref, w3_ref, w2_ref):
    t = _silu(jnp.dot(hb, w1_ref[...], preferred_element_type=F32)) * jnp.dot(hb, w3_ref[...],
                                                                             preferred_element_type=F32)
    return jnp.dot(t.astype(BF16), w2_ref[...], preferred_element_type=F32)


def _post_dense_kernel(x_ref, a_ref, c_ref, rw_ref, f_ref, wout_ref, n2_ref, g1_ref, sh_ref, sc_ref, g2_ref,
                       w1_ref, w3_ref, w2_ref, o_ref, h_s):
    f = pl.program_id(1)

    @pl.when(f == 0)
    def _():
        xn, h = _post_prologue(slice(None), x_ref, (a_ref, c_ref, rw_ref, f_ref), wout_ref, n2_ref, g1_ref, sh_ref,
                               sc_ref)
        o_ref[...] = xn
        h_s[...] = h.astype(BF16)

    o_ref[...] += g2_ref[...] * _swiglu(h_s[...], w1_ref, w3_ref, w2_ref)


def _post_moe_kernel(x_ref, a_ref, c_ref, rw_ref, f_ref, wout_ref, n2_ref, g1_ref, sh_ref, sc_ref, g2_ref,
                     router_ref, order_ref, w1_ref, w3_ref, w2_ref, o_ref, h_s, gate_s, rank_s, cnt_s):
    e = pl.program_id(1)
    lane = _iota((1, LANES), 1)
    sub = MOE_SUB_TILE
    n_sub = x_ref.shape[0] // sub

    @pl.when(e == 0)
    def _():
        for u in range(n_sub):
            rows = slice(u * sub, (u + 1) * sub)
            xn, h = _post_prologue(rows, x_ref, (a_ref, c_ref, rw_ref, f_ref), wout_ref, n2_ref, g1_ref, sh_ref,
                                   sc_ref)
            o_ref[rows, :] = xn
            h_s[rows, :] = h.astype(BF16)
            logits = jnp.dot(h, router_ref[...], precision=lax.Precision.HIGHEST, preferred_element_type=F32)
            logits = jnp.where(lane < N_EXPERTS, logits, NEG_INF)
            ex = jnp.exp(logits - jnp.max(logits, axis=-1, keepdims=True))
            probs = ex / jnp.sum(ex, axis=-1, keepdims=True)
            p1 = jnp.max(probs, axis=-1, keepdims=True)
            i1 = jnp.min(jnp.where(probs == p1, lane, LANES), axis=-1, keepdims=True)
            rest = jnp.where(lane == i1, -1.0, probs)
            p2 = jnp.max(rest, axis=-1, keepdims=True)
            i2 = jnp.min(jnp.where(rest == p2, lane, LANES), axis=-1, keepdims=True)
            top = p1 + p2
            gate_s[rows, :] = jnp.where(lane == i1, p1 / top, jnp.where(lane == i2, p2 / top, 0.0))
            sel_t = jnp.where((lane == i1) | (lane == i2), 1.0, 0.0).T[:N_EXPERTS]
            rank = jnp.dot(sel_t.astype(BF16), order_ref[...], preferred_element_type=F32)
            rank_s[u] = jnp.where(sel_t > 0.0, rank, -1.0)
            for ex_i in range(N_EXPERTS):
                cnt_s[u * N_EXPERTS + ex_i] = jnp.sum(sel_t[ex_i:ex_i + 1, :]).astype(jnp.int32)

    for u in range(n_sub):
        rows = slice(u * sub, (u + 1) * sub)
        routed = cnt_s[u * N_EXPERTS + e]

        def scale(y, rows=rows):
            gate = jnp.sum(jnp.where(lane == e, gate_s[rows, :], 0.0), axis=-1, keepdims=True)
            return gate * (g2_ref[...] * y)

        def gathered(capacity, u=u, rows=rows, scale=scale):
            slot = _iota((capacity, sub), 0).astype(F32)
            pick = jnp.where(rank_s[u, pl.ds(e, 1), :] == slot, 1.0, 0.0)
            hb = jnp.dot(pick.astype(BF16), h_s[rows, :], preferred_element_type=F32).astype(BF16)
            y = _swiglu(hb, w1_ref, w3_ref, w2_ref)
            o_ref[rows, :] += scale(_mm_split(pick.T.astype(BF16), y, split_rhs=True))

        below = 0
        for capacity in MOE_CAPACITIES:
            pl.when((routed > below) & (routed <= capacity))(functools.partial(gathered, capacity))
            below = capacity

        @pl.when(routed > below)
        def _(rows=rows, scale=scale):
            o_ref[rows, :] += scale(_swiglu(h_s[rows, :], w1_ref, w3_ref, w2_ref))


def _post(x, mixes, w_out, norm_w, mod, row_of_tile, ffn, moe):
    n_tok = x.shape[0]
    tok = MOE_TOKEN_TILE if moe else TOKEN_TILE
    tile = lambda width: pl.BlockSpec((tok, width), lambda i, f: (i, 0))
    const = lambda shape: pl.BlockSpec(shape, lambda i, f: (0, 0))
    mod_spec = lambda chunk: _mod_spec(chunk, functools.partial(row_of_tile, tok))
    in_specs = [tile(D_MODEL)] + [tile(COLBLK)] * 4 + [const((MIX_WIDTH, D_MODEL)), const((1, D_MODEL)),
                                                       mod_spec(2), mod_spec(3), mod_spec(4), mod_spec(5)]
    args = [x, *mixes, w_out, norm_w, mod, mod, mod, mod]
    scratch = [pltpu.VMEM((tok, D_MODEL), BF16)]
    if moe:
        router, w1, w3, w2 = ffn
        sub = MOE_SUB_TILE
        order = jnp.where(_iota((sub, sub), 0) < _iota((sub, sub), 1), 1.0, 0.0).astype(BF16)
        in_specs += [const((D_MODEL, LANES)), const((sub, sub)),
                     pl.BlockSpec((None, D_MODEL, D_FF_EXPERT), lambda i, e: (e, 0, 0)),
                     pl.BlockSpec((None, D_MODEL, D_FF_EXPERT), lambda i, e: (e, 0, 0)),
                     pl.BlockSpec((None, D_FF_EXPERT, D_MODEL), lambda i, e: (e, 0, 0))]
        args += [router, order, w1, w3, w2]
        scratch += [pltpu.VMEM((tok, LANES), F32), pltpu.VMEM((tok // sub, N_EXPERTS, sub), F32),
                    pltpu.SMEM((tok // sub * N_EXPERTS,), jnp.int32)]
        body, steps, name = _post_moe_kernel, N_EXPERTS, "post_moe"
    else:
        w1, w3, w2 = ffn
        in_specs += [pl.BlockSpec((D_MODEL, FF_TILE), lambda i, f: (0, f)),
                     pl.BlockSpec((D_MODEL, FF_TILE), lambda i, f: (0, f)),
                     pl.BlockSpec((FF_TILE, D_MODEL), lambda i, f: (f, 0))]
        args += [w1, w3, w2]
        body, steps, name = _post_dense_kernel, D_FF // FF_TILE, "post_dense"
    return pl.pallas_call(
        body,
        grid=(n_tok // tok, steps),
        in_specs=in_specs,
        out_specs=tile(D_MODEL),
        out_shape=jax.ShapeDtypeStruct((n_tok, D_MODEL), F32),
        scratch_shapes=scratch,
        compiler_params=_params("parallel", "arbitrary"),
        name=name,
    )(*args)


def _rope_tables(n_tokens, n_heads):
    n_rows = n_tokens // GRID_W
    row = jnp.repeat(jnp.arange(n_rows), GRID_W).astype(F32)
    col = jnp.tile(jnp.arange(GRID_W), n_rows).astype(F32)
    quarter = HEAD_DIM // 4
    inv_freq = ROPE_BASE ** (-jnp.arange(quarter, dtype=F32) / quarter)
    ang_r, ang_c = row[:, None] * inv_freq, col[:, None] * inv_freq
    cos = jnp.concatenate([jnp.cos(ang_r)] * 2 + [jnp.cos(ang_c)] * 2, axis=1)
    sin = jnp.concatenate([-jnp.sin(ang_r), jnp.sin(ang_r), -jnp.sin(ang_c), jnp.sin(ang_c)], axis=1)
    return jnp.tile(cos, (1, n_heads)), jnp.tile(sin, (1, n_heads))


def _layout_w_in(w):
    lr_end = ATTN_WIDTH + 2 * KV_WIDTH + 3 * CONV_CH + 3 * RWKV_WIDTH + LOWRANK_WIDTH
    pad = jnp.zeros((w.shape[0], COLBLK - LOWRANK_WIDTH), w.dtype)
    return jnp.concatenate([w[:, :lr_end], pad, w[:, lr_end:]], axis=1).astype(BF16)


def _layout_rwkv(l, w0, w_up, a0, a_up, g_up, k_k, k_a, r_k, ln_w, ln_b):
    W = RWKV_WIDTH
    lrw = jnp.zeros((COLBLK, 5 * W), F32)
    for j in range(2):
        lrw = lrw.at[j * DECAY_RANK:(j + 1) * DECAY_RANK, j * W:(j + 1) * W].set(w_up[l, j])
        r0 = 2 * DECAY_RANK + j * ICLR_RANK
        lrw = lrw.at[r0:r0 + ICLR_RANK, (2 + j) * W:(3 + j) * W].set(a_up[l, j])
    r0 = 2 * DECAY_RANK + 2 * ICLR_RANK
    lrw = lrw.at[r0:r0 + GATE_RANK, 4 * W:].set(g_up[l])
    bias = jnp.concatenate([w0[l, 0], w0[l, 1], a0[l, 0], a0[l, 1], jnp.zeros((W,), F32)])[None, :]
    row = lambda t: t[l].reshape(1, W)
    return {"lrw": lrw.astype(BF16), "lr_bias": bias, "k_k": row(k_k), "k_a": row(k_a), "r_k": row(r_k),
            "ln_w": row(ln_w), "ln_b": row(ln_b)}


def kernel(x_prompt, x_sample, cache_k, cache_v, state_wkv, c, c_ctx, mod_w, mod_b, norm1_w, norm2_w, w_in, q_norm_w, k_norm_w, attn_sink, conv_w, rwkv_w0, rwkv_w_up, rwkv_a0, rwkv_a_up, rwkv_g_up, rwkv_k_k, rwkv_k_a, rwkv_r_k, rwkv_ln_w, rwkv_ln_b, w_out, ffn_w1, ffn_w3, ffn_w2, router_w, moe_w1, moe_w3, moe_w2):
    batch, seq, _ = x_prompt.shape
    dec_batch, dec_seq, _ = x_sample.shape
    past = cache_k.shape[2]
    assert seq % BLOCK == 0 and dec_seq % MOE_TOKEN_TILE == 0 and (batch * seq) % MOE_TOKEN_TILE == 0
    assert 1 + dec_batch <= MOD_ROWS

    cond = jnp.concatenate([c_ctx[None, :], c, jnp.zeros((MOD_ROWS - 1 - dec_batch, D_MODEL), F32)], axis=0)
    mod_all = _modulation(cond, mod_w, mod_b)
    prompt_row = lambda tile, i: 0
    sample_row = lambda tile, i: 1 + (i * tile) // dec_seq

    rope_q = _rope_tables(dec_seq, N_Q_HEADS)
    rope_k = _rope_tables(dec_seq, N_KV_HEADS)

    xp = x_prompt.reshape(batch * seq, D_MODEL)
    xs = x_sample.reshape(dec_batch * dec_seq, D_MODEL)
    new_k, new_v, new_s = [], [], []
    for l in range(DEPTH):
        mod = mod_all[l].reshape(MOD_ROWS, 6, 1, D_MODEL)
        w_in_l = _layout_w_in(w_in[l])
        n1 = norm1_w[l][None, :]
        n2 = norm2_w[l][None, :]
        qn = jnp.tile(q_norm_w[l], N_Q_HEADS)[None, :]
        kn = jnp.tile(k_norm_w[l], N_KV_HEADS)[None, :]
        cw = jnp.concatenate([conv_w[l], jnp.zeros((5, CONV_CH), F32)], axis=0)
        rp = _layout_rwkv(l, rwkv_w0, rwkv_w_up, rwkv_a0, rwkv_a_up, rwkv_g_up, rwkv_k_k, rwkv_k_a, rwkv_r_k,
                          rwkv_ln_w, rwkv_ln_b)
        w_out_l = w_out[l].astype(BF16)
        if l % 2 == 0:
            j = l // 2
            ffn = (ffn_w1[j].astype(BF16), ffn_w3[j].astype(BF16), ffn_w2[j].astype(BF16))
        else:
            j = l // 2
            router = jnp.concatenate([router_w[j], jnp.zeros((D_MODEL, LANES - N_EXPERTS), F32)], axis=1)
            ffn = (router, moe_w1[j].astype(BF16), moe_w3[j].astype(BF16), moe_w2[j].astype(BF16))

        proj_p = _in_proj(xp, n1, mod, prompt_row, w_in_l)
        attn_p, k_p, v_p = _ctx_attention(proj_p, qn, kn, attn_sink[l], batch, seq)
        conv_p, four_p = _convfour(proj_p, 0, cw, batch, seq)
        rw_p, sfin = _rwkv(proj_p, 0, rp, batch, seq)
        xp = _post(xp, (attn_p, conv_p, rw_p, four_p), w_out_l, n2, mod, prompt_row, ffn, l % 2 == 1)
        new_k.append(k_p.reshape(batch, seq, N_KV_HEADS, HEAD_DIM))
        new_v.append(v_p.reshape(batch, seq, N_KV_HEADS, HEAD_DIM))
        new_s.append(sfin)

        proj_s = _in_proj(xs, n1, mod, sample_row, w_in_l)
        kc = cache_k[:, l].reshape(dec_batch, past, KV_WIDTH)
        vc = cache_v[:, l].reshape(dec_batch, past, KV_WIDTH)
        attn_s = _lat_attention(proj_s, 0, kc, vc, qn, kn, rope_q, rope_k, attn_sink[l], dec_batch, dec_seq, past)
        conv_s, four_s = _convfour(proj_s, 0, cw, dec_batch, dec_seq)
        rw_s = _rwkv(proj_s, 0, rp, dec_batch, dec_seq, s0=state_wkv[:, l])
        xs = _post(xs, (attn_s, conv_s, rw_s, four_s), w_out_l, n2, mod, sample_row, ffn, l % 2 == 1)

    return (xp.reshape(batch, seq, D_MODEL), xs.reshape(dec_batch, dec_seq, D_MODEL),
            jnp.stack(new_k, axis=1), jnp.stack(new_v, axis=1), jnp.stack(new_s, axis=1))
```

```python
import functools
import math

import jax
import jax.numpy as jnp
import numpy as np
from jax import lax
from jax.experimental import pallas as pl
from jax.experimental.pallas import tpu as pltpu

F32 = jnp.float32
BF16 = jnp.bfloat16

D_MODEL = 1024
DEPTH = 2
GRID_W = 64
HEAD_DIM = 64
N_Q_HEADS = 4
N_KV_HEADS = 2
WINDOW = 128
BLOCK = 128
ROPE_BASE = 10000.0
ATTN_WIDTH = N_Q_HEADS * HEAD_DIM
KV_WIDTH = N_KV_HEADS * HEAD_DIM
CONV_CH = 256
RWKV_HEADS = 4
RWKV_WIDTH = RWKV_HEADS * HEAD_DIM
DECAY_RANK = 32
ICLR_RANK = 32
GATE_RANK = 64
FOURIER_GROUPS = 4
FOURIER_WIDTH = FOURIER_GROUPS * HEAD_DIM
MIX_WIDTH = ATTN_WIDTH + CONV_CH + RWKV_WIDTH + FOURIER_WIDTH
D_FF = 2816
N_EXPERTS = 8
D_FF_EXPERT = 1024
NORM_EPS = 1e-6
GN_EPS = 64e-5
NEG_INF = -1e30

LANES = 128
VMEM_LIMIT_BYTES = 56 * 1024 * 1024

PROJ_WIDTH = 2560
COLBLK = 256
LOWRANK_WIDTH = 2 * DECAY_RANK + 2 * ICLR_RANK + GATE_RANK
MOD_ROWS = 16
TOKEN_TILE = 512
RWKV_CHUNK = 64
RWKV_CHUNKS_PER_STEP = 4
RWKV_PREP_ROWS = 256
FF_TILE = 1408
MOE_TOKEN_TILE = 1024
MOE_SUB_TILE = 512
MOE_CAPACITIES = (96, 160, 224)
DECAY_SCALE = -math.exp(-0.5)


def _params(*sem):
    return pltpu.CompilerParams(dimension_semantics=sem, vmem_limit_bytes=VMEM_LIMIT_BYTES)


def _mm(a, b):
    return jnp.dot(a.astype(BF16), b.astype(BF16), preferred_element_type=F32)


def _mm_nt(a, b):
    return lax.dot_general(a.astype(BF16), b.astype(BF16), (((1,), (1,)), ((), ())), preferred_element_type=F32)


def _mm_tn(a, b):
    return jnp.dot(a.T.astype(BF16), b.astype(BF16), preferred_element_type=F32)


def _mm_split(a, b, split_rhs=False, parts=2):
    exact, x = (a, b) if split_rhs else (b, a)
    acc = None
    for _ in range(parts):
        piece = x.astype(BF16)
        x = x - piece.astype(F32)
        term = (jnp.dot(exact, piece, preferred_element_type=F32) if split_rhs
                else jnp.dot(piece, exact, preferred_element_type=F32))
        acc = term if acc is None else acc + term
    return acc


def _sigmoid(x):
    return 0.5 * jnp.tanh(0.5 * x) + 0.5


def _mm_split2(a, b):
    a_hi, b_hi = a.astype(BF16), b.astype(BF16)
    a_lo = (a - a_hi.astype(F32)).astype(BF16)
    b_lo = (b - b_hi.astype(F32)).astype(BF16)
    dot = functools.partial(jnp.dot, preferred_element_type=F32)
    return dot(a_hi, b_hi) + dot(a_lo, b_hi) + dot(a_hi, b_lo)


def _silu(x):
    return x * _sigmoid(x)


def _iota(shape, dim):
    return lax.broadcasted_iota(jnp.int32, shape, dim)


def _run_interleaved(gens):
    results = [None] * len(gens)
    pending = list(range(len(gens)))
    while pending:
        for idx in list(pending):
            try:
                next(gens[idx])
            except StopIteration as done:
                results[idx] = done.value
                pending.remove(idx)
    return results


def _mod_kernel(c_ref, w_ref, b_ref, o_ref):
    o_ref[0] = _mm(_silu(c_ref[...]), w_ref[0]) + b_ref[0]


def _modulation(cond, mod_w, mod_b):
    n_chunks = mod_w.shape[-1] // D_MODEL
    return pl.pallas_call(
        _mod_kernel,
        grid=(DEPTH, n_chunks),
        in_specs=[
            pl.BlockSpec((MOD_ROWS, D_MODEL), lambda l, j: (0, 0)),
            pl.BlockSpec((1, D_MODEL, D_MODEL), lambda l, j: (l, 0, j)),
            pl.BlockSpec((1, 1, D_MODEL), lambda l, j: (l, 0, j)),
        ],
        out_specs=pl.BlockSpec((1, MOD_ROWS, D_MODEL), lambda l, j: (l, 0, j)),
        out_shape=jax.ShapeDtypeStruct((DEPTH, MOD_ROWS, n_chunks * D_MODEL), F32),
        compiler_params=_params("parallel", "parallel"),
        name="modulation",
    )(cond, mod_w, mod_b.reshape(DEPTH, 1, -1))


def _mod_spec(chunk, row_of_tile):
    return pl.BlockSpec((None, None, 1, D_MODEL), lambda i, *_: (row_of_tile(i), chunk, 0, 0))


def _rms(x, w):
    return x * lax.rsqrt(jnp.mean(x * x, axis=-1, keepdims=True) + NORM_EPS) * w


def _in_proj_kernel(x_ref, nw_ref, sh_ref, sc_ref, w_ref, o_ref):
    h = _rms(x_ref[...], nw_ref[...]) * (1.0 + sc_ref[...]) + sh_ref[...]
    o_ref[...] = jnp.dot(h.astype(BF16), w_ref[...], preferred_element_type=F32)


def _in_proj(x, norm_w, mod, row_of_tile, w_in):
    row_of_tile = functools.partial(row_of_tile, TOKEN_TILE)
    n_tok = x.shape[0]
    return pl.pallas_call(
        _in_proj_kernel,
        grid=(n_tok // TOKEN_TILE,),
        in_specs=[
            pl.BlockSpec((TOKEN_TILE, D_MODEL), lambda i: (i, 0)),
            pl.BlockSpec((1, D_MODEL), lambda i: (0, 0)),
            _mod_spec(0, row_of_tile),
            _mod_spec(1, row_of_tile),
            pl.BlockSpec((D_MODEL, PROJ_WIDTH), lambda i: (0, 0)),
        ],
        out_specs=pl.BlockSpec((TOKEN_TILE, PROJ_WIDTH), lambda i: (i, 0)),
        out_shape=jax.ShapeDtypeStruct((n_tok, PROJ_WIDTH), F32),
        compiler_params=_params("parallel"),
        name="in_proj",
    )(x, norm_w, mod, mod, w_in)


def _group_sum_matrix(width):
    r = _iota((width, width), 0) // HEAD_DIM
    c = _iota((width, width), 1) // HEAD_DIM
    return jnp.where(r == c, 1.0, 0.0).astype(BF16)


def _head_rms(x, w):
    ms = _mm_split(x * x, _group_sum_matrix(x.shape[-1])) * (1.0 / HEAD_DIM)
    return x * lax.rsqrt(ms + NORM_EPS) * w


def _dup_kv_head(x, g):
    lane = _iota((1, KV_WIDTH), 1)
    rolled = pltpu.roll(x, HEAD_DIM, 1)
    first = lane < HEAD_DIM
    return jnp.where(first, x, rolled) if g == 0 else jnp.where(first, rolled, x)


def _half_mask(j):
    lane = _iota((1, KV_WIDTH), 1)
    return (lane >= j * HEAD_DIM) & (lane < (j + 1) * HEAD_DIM)


def _ctx_attn_kernel(qkv_ref, qw_ref, kw_ref, sink_ref, o_ref, k_ref, v_ref):
    qkv = qkv_ref[...]
    q = _head_rms(qkv[:, :ATTN_WIDTH], qw_ref[...]) * HEAD_DIM ** -0.5
    k = _head_rms(qkv[:, ATTN_WIDTH:ATTN_WIDTH + KV_WIDTH], kw_ref[...])
    v = qkv[:, ATTN_WIDTH + KV_WIDTH:]
    k_ref[...] = k
    v_ref[...] = v
    def head(g, j):
        mj = _half_mask(j)
        s = sink_ref[2 * g + j]
        logits = _mm_nt(jnp.where(mj, q[:, g * KV_WIDTH:(g + 1) * KV_WIDTH], 0.0), _dup_kv_head(k, g))
        yield
        m = jnp.maximum(jnp.max(logits, axis=-1, keepdims=True), s)
        e = jnp.exp(logits - m)
        den = jnp.sum(e, axis=-1, keepdims=True) + jnp.exp(s - m)
        pv = _mm(e, jnp.where(mj, _dup_kv_head(v, g), 0.0))
        yield
        return pv / den

    out = _run_interleaved([head(g, j) for g in range(N_KV_HEADS) for j in range(2)])
    for g in range(N_KV_HEADS):
        o_ref[:, g * KV_WIDTH:(g + 1) * KV_WIDTH] = out[2 * g] + out[2 * g + 1]


def _ctx_attention(proj, q_norm, k_norm, sink, batch, seq):
    blk = ATTN_WIDTH + 2 * KV_WIDTH
    return pl.pallas_call(
        _ctx_attn_kernel,
        grid=(batch,),
        in_specs=[
            pl.BlockSpec((seq, blk), lambda b: (b, 0)),
            pl.BlockSpec((1, ATTN_WIDTH), lambda b: (0, 0)),
            pl.BlockSpec((1, KV_WIDTH), lambda b: (0, 0)),
            pl.BlockSpec(memory_space=pltpu.SMEM),
        ],
        out_specs=[
            pl.BlockSpec((seq, ATTN_WIDTH), lambda b: (b, 0)),
            pl.BlockSpec((seq, KV_WIDTH), lambda b: (b, 0)),
            pl.BlockSpec((seq, KV_WIDTH), lambda b: (b, 0)),
        ],
        out_shape=[
            jax.ShapeDtypeStruct((batch * seq, ATTN_WIDTH), F32),
            jax.ShapeDtypeStruct((batch * seq, KV_WIDTH), F32),
            jax.ShapeDtypeStruct((batch * seq, KV_WIDTH), F32),
        ],
        compiler_params=_params("parallel"),
        name="ctx_attention",
    )(proj, q_norm, k_norm, sink)


def _rope(x, cos, sin_signed):
    width = x.shape[-1]
    lane = _iota((1, width), 1)
    first_half = (lane % (HEAD_DIM // 2)) < (HEAD_DIM // 4)
    partner = jnp.where(first_half, pltpu.roll(x, width - HEAD_DIM // 4, 1), pltpu.roll(x, HEAD_DIM // 4, 1))
    return x * cos + partner * sin_signed


def _lat_attn_kernel(seq, past, qkv_ref, kc_ref, vc_ref, qw_ref, kw_ref, cosq_ref, snq_ref, cosk_ref, snk_ref,
                     sink_ref, o_ref, q_s, k_s, v_s):
    qkv = qkv_ref[...]
    q = _rope(_head_rms(qkv[:, :ATTN_WIDTH], qw_ref[...]), cosq_ref[...], snq_ref[...]) * HEAD_DIM ** -0.5
    k = _rope(_head_rms(qkv[:, ATTN_WIDTH:ATTN_WIDTH + KV_WIDTH], kw_ref[...]), cosk_ref[...], snk_ref[...])
    v = qkv[:, ATTN_WIDTH + KV_WIDTH:]
    kc = kc_ref[...]
    vc = vc_ref[...]
    q_s[...] = q
    zero_blk = jnp.zeros((BLOCK, KV_WIDTH), BF16)
    for g in range(N_KV_HEADS):
        k_s[g, 0:BLOCK] = zero_blk
        k_s[g, BLOCK:BLOCK + seq] = _dup_kv_head(k, g).astype(BF16)
        k_s[g, BLOCK + seq:2 * BLOCK + seq] = zero_blk
        k_s[g, 2 * BLOCK + seq:] = _dup_kv_head(kc, g).astype(BF16)
        vg = _dup_kv_head(v, g)
        vcg = _dup_kv_head(vc, g)
        for j in range(2):
            mj = _half_mask(j)
            v_s[2 * g + j, 0:BLOCK] = zero_blk
            v_s[2 * g + j, BLOCK:BLOCK + seq] = jnp.where(mj, vg, 0.0).astype(BF16)
            v_s[2 * g + j, BLOCK + seq:2 * BLOCK + seq] = zero_blk
            v_s[2 * g + j, 2 * BLOCK + seq:] = jnp.where(mj, vcg, 0.0).astype(BF16)

    win = 3 * BLOCK
    ctx0 = 2 * BLOCK + seq

    def q_block(n, carry):
        q0 = pl.multiple_of(n * BLOCK, BLOCK)
        qpos = n * BLOCK + _iota((BLOCK, win), 0)
        kpos = (n - 1) * BLOCK + _iota((BLOCK, win), 1)
        valid = (jnp.abs(qpos - kpos) <= WINDOW) & (kpos >= 0) & (kpos < seq)

        def head(g, j):
            h = 2 * g + j
            s = sink_ref[h]
            qm = jnp.where(_half_mask(j), q_s[pl.ds(q0, BLOCK), g * KV_WIDTH:(g + 1) * KV_WIDTH], 0.0)
            lw = jnp.where(valid, _mm_nt(qm, k_s[g, pl.ds(q0, win), :]), NEG_INF)
            lc = _mm_nt(qm, k_s[g, ctx0:ctx0 + past, :])
            yield
            m = jnp.maximum(jnp.maximum(jnp.max(lw, axis=-1, keepdims=True),
                                        jnp.max(lc, axis=-1, keepdims=True)), s)
            ew = jnp.exp(lw - m)
            ec = jnp.exp(lc - m)
            den = jnp.sum(ew, axis=-1, keepdims=True) + jnp.sum(ec, axis=-1, keepdims=True) + jnp.exp(s - m)
            pv = (jnp.dot(ew.astype(BF16), v_s[h, pl.ds(q0, win), :], preferred_element_type=F32)
                  + jnp.dot(ec.astype(BF16), v_s[h, ctx0:ctx0 + past, :], preferred_element_type=F32))
            yield
            return pv / den

        out = _run_interleaved([head(g, j) for g in range(N_KV_HEADS) for j in range(2)])
        for g in range(N_KV_HEADS):
            o_ref[pl.ds(q0, BLOCK), g * KV_WIDTH:(g + 1) * KV_WIDTH] = out[2 * g] + out[2 * g + 1]
        return carry

    lax.fori_loop(0, seq // BLOCK, q_block, 0)


def _lat_attention(proj, tok_block0, cache_k, cache_v, q_norm, k_norm, rope_q, rope_k, sink, batch, seq, past):
    blk = ATTN_WIDTH + 2 * KV_WIDTH
    cos_q, sin_q = rope_q
    cos_k, sin_k = rope_k
    rows = 2 * BLOCK + seq + past
    const = lambda shape: pl.BlockSpec(shape, lambda b: (0, 0))
    return pl.pallas_call(
        functools.partial(_lat_attn_kernel, seq, past),
        grid=(batch,),
        in_specs=[
            pl.BlockSpec((seq, blk), lambda b: (tok_block0 + b, 0)),
            pl.BlockSpec((None, past, KV_WIDTH), lambda b: (b, 0, 0)),
            pl.BlockSpec((None, past, KV_WIDTH), lambda b: (b, 0, 0)),
            const((1, ATTN_WIDTH)),
            const((1, KV_WIDTH)),
            const((seq, ATTN_WIDTH)),
            const((seq, ATTN_WIDTH)),
            const((seq, KV_WIDTH)),
            const((seq, KV_WIDTH)),
            pl.BlockSpec(memory_space=pltpu.SMEM),
        ],
        out_specs=pl.BlockSpec((seq, ATTN_WIDTH), lambda b: (b, 0)),
        out_shape=jax.ShapeDtypeStruct((batch * seq, ATTN_WIDTH), F32),
        scratch_shapes=[
            pltpu.VMEM((seq, ATTN_WIDTH), F32),
            pltpu.VMEM((N_KV_HEADS, rows, KV_WIDTH), BF16),
            pltpu.VMEM((N_Q_HEADS, rows, KV_WIDTH), BF16),
        ],
        compiler_params=_params("parallel"),
        name="latent_attention",
    )(proj, cache_k, cache_v, q_norm, k_norm, cos_q, sin_q, cos_k, sin_k, sink)


def _convfour_kernel(seq, bx_ref, bb_ref, bc_ref, dx_ref, cw_ref, ct_ref, st_ref, cc_ref, sc_ref, conv_ref, four_ref):
    u = bc_ref[...] * bx_ref[...]
    row = _iota((seq, 1), 0)
    prev = jnp.where(row == 0, 0.0, pltpu.roll(u, 1, 0))
    nxt = jnp.where(row == seq - 1, 0.0, pltpu.roll(u, seq - 1, 0))
    cw = cw_ref[...]
    conv_ref[...] = bb_ref[...] * (cw[0:1] * prev + cw[1:2] * u + cw[2:3] * nxt)
    x = dx_ref[...].astype(BF16)
    a = jnp.dot(ct_ref[...], x, preferred_element_type=F32)
    b = jnp.dot(st_ref[...], x, preferred_element_type=F32)
    y = (jnp.dot(a.astype(BF16), cc_ref[...], preferred_element_type=F32)
         - jnp.dot(b.astype(BF16), sc_ref[...], preferred_element_type=F32))
    four_ref[...] = y * (1.0 / math.sqrt(seq * HEAD_DIM))


def _dft_tables(n, groups=1):
    idx = np.arange(n, dtype=np.int64)
    ang = ((idx[:, None] * idx[None, :]) % n).astype(np.float64) * (2.0 * math.pi / n)
    eye = np.eye(groups)
    return jnp.asarray(np.kron(eye, np.cos(ang)), F32), jnp.asarray(np.kron(eye, np.sin(ang)), F32)


def _convfour(proj, tok_block0, conv_w, batch, seq):
    ct, st = _dft_tables(seq)
    cc, sc = _dft_tables(HEAD_DIM, FOURIER_GROUPS)
    col = lambda c: pl.BlockSpec((seq, COLBLK), lambda b: (tok_block0 + b, c))
    const = lambda shape: pl.BlockSpec(shape, lambda b: (0, 0))
    out = pl.BlockSpec((seq, COLBLK), lambda b: (b, 0))
    return pl.pallas_call(
        functools.partial(_convfour_kernel, seq),
        grid=(batch,),
        in_specs=[col(2), col(3), col(4), col(9), const((8, CONV_CH)), const((seq, seq)), const((seq, seq)),
                  const((FOURIER_WIDTH, FOURIER_WIDTH)), const((FOURIER_WIDTH, FOURIER_WIDTH))],
        out_specs=[out, out],
        out_shape=[jax.ShapeDtypeStruct((batch * seq, COLBLK), F32)] * 2,
        compiler_params=_params("parallel"),
        name="conv_fourier",
    )(proj, proj, proj, proj, conv_w, ct.astype(BF16), st.astype(BF16), cc.astype(BF16), sc.astype(BF16))


def _rwkv_kernel(latent, seq, *refs):
    if latent:
        (r_ref, k_ref, v_ref, lr_ref, lrw_ref, bias_ref, kk_ref, ka_ref, rk_ref, lnw_ref, lnb_ref, s0_ref,
         o_ref, logw_s, kd_s, b_s, z_s, g_s, y_s, st_s) = refs
    else:
        (r_ref, k_ref, v_ref, lr_ref, lrw_ref, bias_ref, kk_ref, ka_ref, rk_ref, lnw_ref, lnb_ref,
         o_ref, sfin_ref, logw_s, kd_s, b_s, z_s, g_s, y_s, st_s) = refs
    W = RWKV_WIDTH
    C = RWKV_CHUNK
    gsum = _group_sum_matrix(W)
    lane = _iota((1, W), 1)

    def prep(i, carry):
        rows = pl.ds(pl.multiple_of(i * RWKV_PREP_ROWS, RWKV_PREP_ROWS), RWKV_PREP_ROWS)
        lr = lr_ref[rows, :]
        f = jnp.where(lane < 2 * DECAY_RANK, jnp.tanh(lr),
                      jnp.where(lane < 2 * DECAY_RANK + 2 * ICLR_RANK, lr, _sigmoid(lr)))
        low = jnp.dot(f.astype(BF16), lrw_ref[...], preferred_element_type=F32) + bias_ref[...]
        logw_s[rows, :] = DECAY_SCALE * _sigmoid(low[:, :2 * W])
        a = _sigmoid(low[:, 2 * W:4 * W])
        g_s[rows, :] = low[:, 4 * W:]
        k = k_ref[rows, :]
        kk = k * kk_ref[...]
        kk = kk / jnp.maximum(jnp.sqrt(_mm_split(kk * kk, gsum)), 1e-12)
        z_s[rows, :] = -kk
        for j in range(2):
            aj = a[:, j * W:(j + 1) * W]
            kd_s[rows, j * W:(j + 1) * W] = k * (1.0 + (aj - 1.0) * ka_ref[...])
            b_s[rows, j * W:(j + 1) * W] = kk * aj
        return carry

    lax.fori_loop(0, seq // RWKV_PREP_ROWS, prep, 0)

    y_s[...] = jnp.zeros((seq, W), F32)
    st_s[...] = jnp.zeros((2, W, W), F32)
    if latent:
        for j in range(2):
            for h in range(RWKV_HEADS):
                blk = slice(h * HEAD_DIM, (h + 1) * HEAD_DIM)
                st_s[j, blk, blk] = s0_ref[j, h].T

    HC = RWKV_HEADS * C
    wide_t = _iota((C, HC), 0)
    wide_s = _iota((C, HC), 1) % C
    eye_wide = jnp.where(wide_t == wide_s, 1.0, 0.0).astype(F32)
    same_head = (_iota((HC, HC), 0) // C) == (_iota((HC, HC), 1) // C)
    bd = (_iota((W, W), 0) // HEAD_DIM) == (_iota((W, W), 1) // HEAD_DIM)
    eye_w = _iota((W, W), 0) == _iota((W, W), 1)
    tc_i = _iota((C, C), 0)
    sc_i = _iota((C, C), 1)
    head_masks = [(lane >= h * HEAD_DIM) & (lane < (h + 1) * HEAD_DIM) for h in range(RWKV_HEADS)]
    n_chunks = seq // C

    def stack_masked(x):
        return jnp.concatenate([jnp.where(m, x, 0.0) for m in head_masks], axis=0).astype(BF16)

    def block_diag(x, mask):
        return jnp.where(mask, jnp.concatenate([x] * RWKV_HEADS, axis=0), 0.0)

    def bdot(a, b):
        return jnp.dot(a, b, preferred_element_type=F32)

    def chunk(j, c):
        rows = pl.ds(pl.multiple_of(c * C, C), C)
        cols = slice(j * W, (j + 1) * W)
        rc, vc, zc = r_ref[rows, :], v_ref[rows, :], z_s[rows, :]
        lw, kd, bc = logw_s[rows, cols], kd_s[rows, cols], b_s[rows, cols]
        if j == 0:
            strict, incl, tri = wide_s < wide_t, wide_s <= wide_t, sc_i <= tc_i
        else:
            strict, incl, tri = wide_s > wide_t, wide_s >= wide_t, sc_i >= tc_i
        cl = _mm_split(jnp.where(tri, 1.0, 0.0).astype(BF16), lw, split_rhs=True, parts=3)
        yield
        tot = cl[C - 1:C, :] if j == 0 else cl[0:1, :]
        w_in, w_ex, w_inv, w_rest = jnp.exp(cl), jnp.exp(cl - lw), jnp.exp(-cl), jnp.exp(tot - cl)
        rt = rc * w_in
        zs = stack_masked(zc * w_ex)
        vs = stack_masked(vc)
        aa = lax.dot_general(jnp.concatenate([zc * w_ex, rt], axis=0).astype(BF16),
                             jnp.concatenate([stack_masked(bc * w_inv), stack_masked(kd * w_inv)], axis=0),
                             (((1,), (1,)), ((), ())), preferred_element_type=F32)
        yield
        a_zb = jnp.where(strict, aa[:C, :HC], 0.0)
        a_zk = jnp.where(strict, aa[:C, HC:], 0.0).astype(BF16)
        a_rb = jnp.where(incl, aa[C:, :HC], 0.0).astype(BF16)
        a_rk = jnp.where(incl, aa[C:, HC:], 0.0).astype(BF16)
        tinv = eye_wide + a_zb
        apow = bdot(a_zb.astype(BF16), block_diag(a_zb, same_head).astype(BF16))
        av = bdot(a_zk, vs)
        rk = bdot(a_rk, vs)
        yield
        for _ in range(int(math.log2(C)) - 2):
            both = bdot(jnp.concatenate([apow, tinv], axis=0).astype(BF16), block_diag(apow, same_head).astype(BF16))
            yield
            apow = both[:C]
            tinv = tinv + both[C:]
        tinv = tinv + bdot(tinv.astype(BF16), block_diag(apow, same_head).astype(BF16))
        yield
        x = bdot(tinv.astype(BF16), jnp.concatenate([zs, stack_masked(av)], axis=1))
        yield
        ry = bdot(a_rb, jnp.concatenate([stack_masked(x[:, :W]), stack_masked(x[:, W:])], axis=1))
        lhs = jnp.concatenate([bc * w_rest, kd * w_rest], axis=0)
        rhs = jnp.concatenate([x, jnp.concatenate([jnp.zeros((C, W), F32), vc], axis=1)], axis=0)
        mn = _mm_tn(lhs, rhs)
        yield
        rz = rt + ry[:, :W]
        y0 = ry[:, W:] + rk
        mt = jnp.where(bd, mn[:, :W], 0.0) + jnp.where(eye_w, jnp.exp(tot), 0.0)
        mt_wide = mt[0:HEAD_DIM]
        for h in range(1, RWKV_HEADS):
            mt_wide = mt_wide + mt[h * HEAD_DIM:(h + 1) * HEAD_DIM]
        return rows, rz.astype(BF16), y0, mt_wide.astype(BF16), jnp.where(bd, mn[:, W:], 0.0)

    def advance(j, local):
        rows, rz, y0, mt_wide, nt = local
        s_prev = st_s[j].astype(BF16)
        y_s[rows, :] += bdot(rz, s_prev) + y0
        st_s[j] = block_diag(bdot(mt_wide, s_prev), bd) + nt

    def step(i, carry):
        n = RWKV_CHUNKS_PER_STEP
        local = _run_interleaved([chunk(0, n * i + u) for u in range(n)]
                                 + [chunk(1, n_chunks - 1 - (n * i + u)) for u in range(n)])
        for u in range(n):
            advance(0, local[u])
            advance(1, local[n + u])
        return carry

    lax.fori_loop(0, n_chunks // RWKV_CHUNKS_PER_STEP, step, 0)

    if not latent:
        for j in range(2):
            for h in range(RWKV_HEADS):
                blk = slice(h * HEAD_DIM, (h + 1) * HEAD_DIM)
                sfin_ref[j, h] = st_s[j, blk, blk].T

    def finish(i, carry):
        rows = pl.ds(pl.multiple_of(i * RWKV_PREP_ROWS, RWKV_PREP_ROWS), RWKV_PREP_ROWS)
        y = y_s[rows, :]
        mu = _mm_split(y, gsum) * (1.0 / HEAD_DIM)
        d = y - mu
        var = _mm_split(d * d, gsum) * (1.0 / HEAD_DIM)
        yn = d * lax.rsqrt(var + GN_EPS) * lnw_ref[...] + lnb_ref[...]
        r, k, v = r_ref[rows, :], k_ref[rows, :], v_ref[rows, :]
        bonus = _mm_split(r * k * rk_ref[...], gsum) * v
        o_ref[rows, :] = (yn + bonus) * g_s[rows, :]
        return carry

    lax.fori_loop(0, seq // RWKV_PREP_ROWS, finish, 0)


def _rwkv(proj, tok_block0, p, batch, seq, s0=None):
    latent = s0 is not None
    W = RWKV_WIDTH
    col = lambda c: pl.BlockSpec((seq, COLBLK), lambda b: (tok_block0 + b, c))
    const = lambda shape: pl.BlockSpec(shape, lambda b: (0,) * len(shape))
    in_specs = [col(5), col(6), col(7), col(8), const((COLBLK, 5 * W)), const((1, 5 * W))] + [const((1, W))] * 5
    args = [proj, proj, proj, proj, p["lrw"], p["lr_bias"], p["k_k"], p["k_a"], p["r_k"], p["ln_w"], p["ln_b"]]
    out_specs = [pl.BlockSpec((seq, W), lambda b: (b, 0))]
    out_shape = [jax.ShapeDtypeStruct((batch * seq, W), F32)]
    if latent:
        in_specs.append(pl.BlockSpec((None, 2, RWKV_HEADS, HEAD_DIM, HEAD_DIM), lambda b: (b, 0, 0, 0, 0)))
        args.append(s0)
    else:
        out_specs.append(pl.BlockSpec((None, 2, RWKV_HEADS, HEAD_DIM, HEAD_DIM), lambda b: (b, 0, 0, 0, 0)))
        out_shape.append(jax.ShapeDtypeStruct((batch, 2, RWKV_HEADS, HEAD_DIM, HEAD_DIM), F32))
    res = pl.pallas_call(
        functools.partial(_rwkv_kernel, latent, seq),
        grid=(batch,),
        in_specs=in_specs,
        out_specs=out_specs,
        out_shape=out_shape,
        scratch_shapes=[
            pltpu.VMEM((seq, 2 * W), F32),
            pltpu.VMEM((seq, 2 * W), F32),
            pltpu.VMEM((seq, 2 * W), F32),
            pltpu.VMEM((seq, W), F32),
            pltpu.VMEM((seq, W), F32),
            pltpu.VMEM((seq, W), F32),
            pltpu.VMEM((2, W, W), F32),
        ],
        compiler_params=_params("parallel"),
        name="rwkv_latent" if latent else "rwkv_ctx",
    )(*args)
    return res if not latent else res[0]


def _post_prologue(rows, x_ref, mix_refs, wout_ref, n2_ref, g1_ref, sh_ref, sc_ref):
    mix = None
    for i, m_ref in enumerate(mix_refs):
        part = jnp.dot(m_ref[rows, :].astype(BF16), wout_ref[i * COLBLK:(i + 1) * COLBLK, :],
                       preferred_element_type=F32)
        mix = part if mix is None else mix + part
    xn = x_ref[rows, :] + g1_ref[...] * mix
    return xn, _rms(xn, n2_ref[...]) * (1.0 + sc_ref[...]) + sh_ref[...]


def _swiglu(hb, w1_ref, w3_ref, w2_ref):
    t = _silu(jnp.dot(hb, w1_ref[...], preferred_element_type=F32)) * jnp.dot(hb, w3_ref[...],
                                                                             preferred_element_type=F32)
    return jnp.dot(t.astype(BF16), w2_ref[...], preferred_element_type=F32)


def _post_dense_kernel(x_ref, a_ref, c_ref, rw_ref, f_ref, wout_ref, n2_ref, g1_ref, sh_ref, sc_ref, g2_ref,
                       w1_ref, w3_ref, w2_ref, o_ref, h_s):
    f = pl.program_id(1)

    @pl.when(f == 0)
    def _():
        xn, h = _post_prologue(slice(None), x_ref, (a_ref, c_ref, rw_ref, f_ref), wout_ref, n2_ref, g1_ref, sh_ref,
                               sc_ref)
        o_ref[...] = xn
        h_s[...] = h.astype(BF16)

    o_ref[...] += g2_ref[...] * _swiglu(h_s[...], w1_ref, w3_ref, w2_ref)


def _post_moe_kernel(x_ref, a_ref, c_ref, rw_ref, f_ref, wout_ref, n2_ref, g1_ref, sh_ref, sc_ref, g2_ref,
                     router_ref, order_ref, w1_ref, w3_ref, w2_ref, o_ref, h_s, gate_s, rank_s, cnt_s):
    e = pl.program_id(1)
    lane = _iota((1, LANES), 1)
    sub = MOE_SUB_TILE
    n_sub = x_ref.shape[0] // sub

    @pl.when(e == 0)
    def _():
        for u in range(n_sub):
            rows = slice(u * sub, (u + 1) * sub)
            xn, h = _post_prologue(rows, x_ref, (a_ref, c_ref, rw_ref, f_ref), wout_ref, n2_ref, g1_ref, sh_ref,
                                   sc_ref)
            o_ref[rows, :] = xn
            h_s[rows, :] = h.astype(BF16)
            logits = _mm_split2(h, router_ref[...]).T[:N_EXPERTS]
            ex = jnp.exp(logits - jnp.max(logits, axis=0, keepdims=True))
            probs = ex / jnp.sum(ex, axis=0, keepdims=True)
            expert = _iota(probs.shape, 0)
            p1 = jnp.max(probs, axis=0, keepdims=True)
            i1 = jnp.min(jnp.where(probs == p1, expert, N_EXPERTS), axis=0, keepdims=True)
            rest = jnp.where(expert == i1, -1.0, probs)
            p2 = jnp.max(rest, axis=0, keepdims=True)
            i2 = jnp.min(jnp.where(rest == p2, expert, N_EXPERTS), axis=0, keepdims=True)
            top = p1 + p2
            gates_t = jnp.where(expert == i1, p1 / top, jnp.where(expert == i2, p2 / top, 0.0))
            gate_s[rows, :] = jnp.concatenate([gates_t, jnp.zeros((LANES - N_EXPERTS, sub), F32)], axis=0).T
            sel_t = jnp.where((expert == i1) | (expert == i2), 1.0, 0.0)
            rank = jnp.dot(sel_t.astype(BF16), order_ref[...], preferred_element_type=F32)
            rank_s[u] = jnp.where(sel_t > 0.0, rank, -1.0)
            for ex_i in range(N_EXPERTS):
                cnt_s[u * N_EXPERTS + ex_i] = jnp.sum(sel_t[ex_i:ex_i + 1, :]).astype(jnp.int32)

    for u in range(n_sub):
        rows = slice(u * sub, (u + 1) * sub)
        routed = cnt_s[u * N_EXPERTS + e]

        def scale(y, rows=rows):
            gate = jnp.sum(jnp.where(lane == e, gate_s[rows, :], 0.0), axis=-1, keepdims=True)
            return gate * (g2_ref[...] * y)

        def gathered(capacity, u=u, rows=rows, scale=scale):
            slot = _iota((capacity, sub), 0).astype(F32)
            pick = jnp.where(rank_s[u, pl.ds(e, 1), :] == slot, 1.0, 0.0)
            hb = jnp.dot(pick.astype(BF16), h_s[rows, :], preferred_element_type=F32).astype(BF16)
            y = _swiglu(hb, w1_ref, w3_ref, w2_ref)
            o_ref[rows, :] += scale(_mm(pick.T, y))

        below = 0
        for capacity in MOE_CAPACITIES:
            pl.when((routed > below) & (routed <= capacity))(functools.partial(gathered, capacity))
            below = capacity

        @pl.when(routed > below)
        def _(rows=rows, scale=scale):
            o_ref[rows, :] += scale(_swiglu(h_s[rows, :], w1_ref, w3_ref, w2_ref))


def _post(x, mixes, w_out, norm_w, mod, row_of_tile, ffn, moe):
    n_tok = x.shape[0]
    tok = MOE_TOKEN_TILE if moe else TOKEN_TILE
    tile = lambda width: pl.BlockSpec((tok, width), lambda i, f: (i, 0))
    const = lambda shape: pl.BlockSpec(shape, lambda i, f: (0, 0))
    mod_spec = lambda chunk: _mod_spec(chunk, functools.partial(row_of_tile, tok))
    in_specs = [tile(D_MODEL)] + [tile(COLBLK)] * 4 + [const((MIX_WIDTH, D_MODEL)), const((1, D_MODEL)),
                                                       mod_spec(2), mod_spec(3), mod_spec(4), mod_spec(5)]
    args = [x, *mixes, w_out, norm_w, mod, mod, mod, mod]
    scratch = [pltpu.VMEM((tok, D_MODEL), BF16)]
    if moe:
        router, w1, w3, w2 = ffn
        sub = MOE_SUB_TILE
        order = jnp.where(_iota((sub, sub), 0) < _iota((sub, sub), 1), 1.0, 0.0).astype(BF16)
        in_specs += [const((D_MODEL, LANES)), const((sub, sub)),
                     pl.BlockSpec((None, D_MODEL, D_FF_EXPERT), lambda i, e: (e, 0, 0)),
                     pl.BlockSpec((None, D_MODEL, D_FF_EXPERT), lambda i, e: (e, 0, 0)),
                     pl.BlockSpec((None, D_FF_EXPERT, D_MODEL), lambda i, e: (e, 0, 0))]
        args += [router, order, w1, w3, w2]
        scratch += [pltpu.VMEM((tok, LANES), F32), pltpu.VMEM((tok // sub, N_EXPERTS, sub), F32),
                    pltpu.SMEM((tok // sub * N_EXPERTS,), jnp.int32)]
        body, steps, name = _post_moe_kernel, N_EXPERTS, "post_moe"
    else:
        w1, w3, w2 = ffn
        in_specs += [pl.BlockSpec((D_MODEL, FF_TILE), lambda i, f: (0, f)),
                     pl.BlockSpec((D_MODEL, FF_TILE), lambda i, f: (0, f)),
                     pl.BlockSpec((FF_TILE, D_MODEL), lambda i, f: (f, 0))]
        args += [w1, w3, w2]
        body, steps, name = _post_dense_kernel, D_FF // FF_TILE, "post_dense"
    return pl.pallas_call(
        body,
        grid=(n_tok // tok, steps),
        in_specs=in_specs,
        out_specs=tile(D_MODEL),
        out_shape=jax.ShapeDtypeStruct((n_tok, D_MODEL), F32),
        scratch_shapes=scratch,
        compiler_params=_params("parallel", "arbitrary"),
        name=name,
    )(*args)


def _rope_tables(n_tokens, n_heads):
    n_rows = n_tokens // GRID_W
    row = np.repeat(np.arange(n_rows), GRID_W).astype(np.float32)
    col = np.tile(np.arange(GRID_W), n_rows).astype(np.float32)
    quarter = HEAD_DIM // 4
    inv_freq = np.float32(ROPE_BASE) ** (-np.arange(quarter, dtype=np.float32) / np.float32(quarter))
    ang_r, ang_c = row[:, None] * inv_freq, col[:, None] * inv_freq
    cos = np.concatenate([np.cos(ang_r)] * 2 + [np.cos(ang_c)] * 2, axis=1)
    sin = np.concatenate([-np.sin(ang_r), np.sin(ang_r), -np.sin(ang_c), np.sin(ang_c)], axis=1)
    return jnp.asarray(np.tile(cos, (1, n_heads)), F32), jnp.asarray(np.tile(sin, (1, n_heads)), F32)


def _layout_w_in(w):
    lr_end = ATTN_WIDTH + 2 * KV_WIDTH + 3 * CONV_CH + 3 * RWKV_WIDTH + LOWRANK_WIDTH
    pad = jnp.zeros((w.shape[0], COLBLK - LOWRANK_WIDTH), w.dtype)
    return jnp.concatenate([w[:, :lr_end], pad, w[:, lr_end:]], axis=1).astype(BF16)


def _layout_rwkv(l, w0, w_up, a0, a_up, g_up, k_k, k_a, r_k, ln_w, ln_b):
    W = RWKV_WIDTH
    lrw = jnp.zeros((COLBLK, 5 * W), F32)
    for j in range(2):
        lrw = lrw.at[j * DECAY_RANK:(j + 1) * DECAY_RANK, j * W:(j + 1) * W].set(w_up[l, j])
        r0 = 2 * DECAY_RANK + j * ICLR_RANK
        lrw = lrw.at[r0:r0 + ICLR_RANK, (2 + j) * W:(3 + j) * W].set(a_up[l, j])
    r0 = 2 * DECAY_RANK + 2 * ICLR_RANK
    lrw = lrw.at[r0:r0 + GATE_RANK, 4 * W:].set(g_up[l])
    bias = jnp.concatenate([w0[l, 0], w0[l, 1], a0[l, 0], a0[l, 1], jnp.zeros((W,), F32)])[None, :]
    row = lambda t: t[l].reshape(1, W)
    return {"lrw": lrw.astype(BF16), "lr_bias": bias, "k_k": row(k_k), "k_a": row(k_a), "r_k": row(r_k),
            "ln_w": row(ln_w), "ln_b": row(ln_b)}


def kernel(x_prompt, x_sample, cache_k, cache_v, state_wkv, c, c_ctx, mod_w, mod_b, norm1_w, norm2_w, w_in, q_norm_w, k_norm_w, attn_sink, conv_w, rwkv_w0, rwkv_w_up, rwkv_a0, rwkv_a_up, rwkv_g_up, rwkv_k_k, rwkv_k_a, rwkv_r_k, rwkv_ln_w, rwkv_ln_b, w_out, ffn_w1, ffn_w3, ffn_w2, router_w, moe_w1, moe_w3, moe_w2):
    batch, seq, _ = x_prompt.shape
    dec_batch, dec_seq, _ = x_sample.shape
    past = cache_k.shape[2]
    assert seq % BLOCK == 0 and dec_seq % MOE_TOKEN_TILE == 0 and (batch * seq) % MOE_TOKEN_TILE == 0
    assert 1 + dec_batch <= MOD_ROWS

    cond = jnp.concatenate([c_ctx[None, :], c, jnp.zeros((MOD_ROWS - 1 - dec_batch, D_MODEL), F32)], axis=0)
    mod_all = _modulation(cond, mod_w, mod_b)
    prompt_row = lambda tile, i: 0
    sample_row = lambda tile, i: 1 + (i * tile) // dec_seq

    rope_q = _rope_tables(dec_seq, N_Q_HEADS)
    rope_k = _rope_tables(dec_seq, N_KV_HEADS)

    xp = x_prompt.reshape(batch * seq, D_MODEL)
    xs = x_sample.reshape(dec_batch * dec_seq, D_MODEL)
    new_k, new_v, new_s = [], [], []
    for l in range(DEPTH):
        mod = mod_all[l].reshape(MOD_ROWS, 6, 1, D_MODEL)
        w_in_l = _layout_w_in(w_in[l])
        n1 = norm1_w[l][None, :]
        n2 = norm2_w[l][None, :]
        qn = jnp.tile(q_norm_w[l], N_Q_HEADS)[None, :]
        kn = jnp.tile(k_norm_w[l], N_KV_HEADS)[None, :]
        cw = jnp.concatenate([conv_w[l], jnp.zeros((5, CONV_CH), F32)], axis=0)
        rp = _layout_rwkv(l, rwkv_w0, rwkv_w_up, rwkv_a0, rwkv_a_up, rwkv_g_up, rwkv_k_k, rwkv_k_a, rwkv_r_k,
                          rwkv_ln_w, rwkv_ln_b)
        w_out_l = w_out[l].astype(BF16)
        if l % 2 == 0:
            j = l // 2
            ffn = (ffn_w1[j].astype(BF16), ffn_w3[j].astype(BF16), ffn_w2[j].astype(BF16))
        else:
            j = l // 2
            router = jnp.concatenate([router_w[j], jnp.zeros((D_MODEL, LANES - N_EXPERTS), F32)], axis=1)
            ffn = (router, moe_w1[j].astype(BF16), moe_w3[j].astype(BF16), moe_w2[j].astype(BF16))

        proj_p = _in_proj(xp, n1, mod, prompt_row, w_in_l)
        attn_p, k_p, v_p = _ctx_attention(proj_p, qn, kn, attn_sink[l], batch, seq)
        conv_p, four_p = _convfour(proj_p, 0, cw, batch, seq)
        rw_p, sfin = _rwkv(proj_p, 0, rp, batch, seq)
        xp = _post(xp, (attn_p, conv_p, rw_p, four_p), w_out_l, n2, mod, prompt_row, ffn, l % 2 == 1)
        new_k.append(k_p.reshape(batch, seq, N_KV_HEADS, HEAD_DIM))
        new_v.append(v_p.reshape(batch, seq, N_KV_HEADS, HEAD_DIM))
        new_s.append(sfin)

        proj_s = _in_proj(xs, n1, mod, sample_row, w_in_l)
        kc = cache_k[:, l].reshape(dec_batch, past, KV_WIDTH)
        vc = cache_v[:, l].reshape(dec_batch, past, KV_WIDTH)
        attn_s = _lat_attention(proj_s, 0, kc, vc, qn, kn, rope_q, rope_k, attn_sink[l], dec_batch, dec_seq, past)
        conv_s, four_s = _convfour(proj_s, 0, cw, dec_batch, dec_seq)
        rw_s = _rwkv(proj_s, 0, rp, dec_batch, dec_seq, s0=state_wkv[:, l])
        xs = _post(xs, (attn_s, conv_s, rw_s, four_s), w_out_l, n2, mod, sample_row, ffn, l % 2 == 1)

    return (xp.reshape(batch, seq, D_MODEL), xs.reshape(dec_batch, dec_seq, D_MODEL),
            jnp.stack(new_k, axis=1), jnp.stack(new_v, axis=1), jnp.stack(new_s, axis=1))
```

```python
import functools
import math

import jax
import jax.numpy as jnp
import numpy as np
from jax import lax
from jax.experimental import pallas as pl
from jax.experimental.pallas import tpu as pltpu

F32 = jnp.float32
BF16 = jnp.bfloat16

D_MODEL = 1024
DEPTH = 2
GRID_W = 64
HEAD_DIM = 64
N_Q_HEADS = 4
N_KV_HEADS = 2
WINDOW = 128
BLOCK = 128
ROPE_BASE = 10000.0
ATTN_WIDTH = N_Q_HEADS * HEAD_DIM
KV_WIDTH = N_KV_HEADS * HEAD_DIM
CONV_CH = 256
RWKV_HEADS = 4
RWKV_WIDTH = RWKV_HEADS * HEAD_DIM
DECAY_RANK = 32
ICLR_RANK = 32
GATE_RANK = 64
FOURIER_GROUPS = 4
FOURIER_WIDTH = FOURIER_GROUPS * HEAD_DIM
MIX_WIDTH = ATTN_WIDTH + CONV_CH + RWKV_WIDTH + FOURIER_WIDTH
D_FF = 2816
N_EXPERTS = 8
D_FF_EXPERT = 1024
NORM_EPS = 1e-6
GN_EPS = 64e-5
NEG_INF = -1e30

LANES = 128
VMEM_LIMIT_BYTES = 56 * 1024 * 1024

PROJ_WIDTH = 2560
COLBLK = 256
LOWRANK_WIDTH = 2 * DECAY_RANK + 2 * ICLR_RANK + GATE_RANK
MOD_ROWS = 16
TOKEN_TILE = 512
RWKV_CHUNK = 64
RWKV_CHUNKS_PER_STEP = 4
RWKV_PREP_ROWS = 256
FF_TILE = 1408
MOE_TOKEN_TILE = 1024
MOE_SUB_TILE = 512
MOE_CAPACITIES = (128, 160, 192, 224)
DECAY_SCALE = -math.exp(-0.5)


def _params(*sem):
    return pltpu.CompilerParams(dimension_semantics=sem, vmem_limit_bytes=VMEM_LIMIT_BYTES)


def _mm(a, b):
    return jnp.dot(a.astype(BF16), b.astype(BF16), preferred_element_type=F32)


def _mm_nt(a, b):
    return lax.dot_general(a.astype(BF16), b.astype(BF16), (((1,), (1,)), ((), ())), preferred_element_type=F32)


def _mm_split(a, b, split_rhs=False, parts=2):
    exact, x = (a, b) if split_rhs else (b, a)
    acc = None
    for _ in range(parts):
        piece = x.astype(BF16)
        x = x - piece.astype(F32)
        term = (jnp.dot(exact, piece, preferred_element_type=F32) if split_rhs
                else jnp.dot(piece, exact, preferred_element_type=F32))
        acc = term if acc is None else acc + term
    return acc


def _sigmoid(x):
    return 0.5 * jnp.tanh(0.5 * x) + 0.5


def _mm_split2(a, b):
    a_hi, b_hi = a.astype(BF16), b.astype(BF16)
    a_lo = (a - a_hi.astype(F32)).astype(BF16)
    b_lo = (b - b_hi.astype(F32)).astype(BF16)
    dot = functools.partial(jnp.dot, preferred_element_type=F32)
    return dot(a_hi, b_hi) + dot(a_lo, b_hi) + dot(a_hi, b_lo)


def _silu(x):
    return x * _sigmoid(x)


def _iota(shape, dim):
    return lax.broadcasted_iota(jnp.int32, shape, dim)


def _run_interleaved(gens):
    results = [None] * len(gens)
    pending = list(range(len(gens)))
    while pending:
        for idx in list(pending):
            try:
                next(gens[idx])
            except StopIteration as done:
                results[idx] = done.value
                pending.remove(idx)
    return results


def _mod_kernel(c_ref, w_ref, b_ref, o_ref):
    o_ref[0] = _mm(_silu(c_ref[...]), w_ref[0]) + b_ref[0]


def _modulation(cond, mod_w, mod_b):
    n_chunks = mod_w.shape[-1] // D_MODEL
    return pl.pallas_call(
        _mod_kernel,
        grid=(DEPTH, n_chunks),
        in_specs=[
            pl.BlockSpec((MOD_ROWS, D_MODEL), lambda l, j: (0, 0)),
            pl.BlockSpec((1, D_MODEL, D_MODEL), lambda l, j: (l, 0, j)),
            pl.BlockSpec((1, 1, D_MODEL), lambda l, j: (l, 0, j)),
        ],
        out_specs=pl.BlockSpec((1, MOD_ROWS, D_MODEL), lambda l, j: (l, 0, j)),
        out_shape=jax.ShapeDtypeStruct((DEPTH, MOD_ROWS, n_chunks * D_MODEL), F32),
        compiler_params=_params("parallel", "parallel"),
        name="modulation",
    )(cond, mod_w, mod_b.reshape(DEPTH, 1, -1))


def _mod_spec(chunk, row_of_tile):
    return pl.BlockSpec((None, None, 1, D_MODEL), lambda i, *_: (row_of_tile(i), chunk, 0, 0))


def _rms(x, w):
    return x * lax.rsqrt(jnp.mean(x * x, axis=-1, keepdims=True) + NORM_EPS) * w


def _in_proj_kernel(x_ref, nw_ref, sh_ref, sc_ref, w_main_ref, w_four_ref, o_ref):
    h = (_rms(x_ref[...], nw_ref[...]) * (1.0 + sc_ref[...]) + sh_ref[...]).astype(BF16)
    n_main = w_main_ref.shape[1]
    gap_end = PROJ_WIDTH - FOURIER_WIDTH
    o_ref[:, :n_main] = jnp.dot(h, w_main_ref[...], preferred_element_type=F32)
    o_ref[:, n_main:gap_end] = jnp.zeros((h.shape[0], gap_end - n_main), F32)
    o_ref[:, gap_end:] = jnp.dot(h, w_four_ref[...], preferred_element_type=F32)


def _in_proj(x, norm_w, mod, row_of_tile, w_in):
    row_of_tile = functools.partial(row_of_tile, TOKEN_TILE)
    n_tok = x.shape[0]
    return pl.pallas_call(
        _in_proj_kernel,
        grid=(n_tok // TOKEN_TILE,),
        in_specs=[
            pl.BlockSpec((TOKEN_TILE, D_MODEL), lambda i: (i, 0)),
            pl.BlockSpec((1, D_MODEL), lambda i: (0, 0)),
            _mod_spec(0, row_of_tile),
            _mod_spec(1, row_of_tile),
            pl.BlockSpec(w_in[0].shape, lambda i: (0, 0)),
            pl.BlockSpec(w_in[1].shape, lambda i: (0, 0)),
        ],
        out_specs=pl.BlockSpec((TOKEN_TILE, PROJ_WIDTH), lambda i: (i, 0)),
        out_shape=jax.ShapeDtypeStruct((n_tok, PROJ_WIDTH), F32),
        compiler_params=_params("parallel"),
        name="in_proj",
    )(x, norm_w, mod, mod, *w_in)


def _group_sum_matrix(width):
    r = _iota((width, width), 0) // HEAD_DIM
    c = _iota((width, width), 1) // HEAD_DIM
    return jnp.where(r == c, 1.0, 0.0).astype(BF16)


def _head_rms(x, w):
    ms = _mm_split(x * x, _group_sum_matrix(x.shape[-1])) * (1.0 / HEAD_DIM)
    return x * lax.rsqrt(ms + NORM_EPS) * w


def _dup_kv_head(x, g):
    lane = _iota((1, KV_WIDTH), 1)
    rolled = pltpu.roll(x, HEAD_DIM, 1)
    first = lane < HEAD_DIM
    return jnp.where(first, x, rolled) if g == 0 else jnp.where(first, rolled, x)


def _half_mask(j):
    lane = _iota((1, KV_WIDTH), 1)
    return (lane >= j * HEAD_DIM) & (lane < (j + 1) * HEAD_DIM)


def _ctx_attn_kernel(qkv_ref, qw_ref, kw_ref, sink_ref, o_ref, k_ref, v_ref):
    qkv = qkv_ref[...]
    q = _head_rms(qkv[:, :ATTN_WIDTH], qw_ref[...]) * HEAD_DIM ** -0.5
    k = _head_rms(qkv[:, ATTN_WIDTH:ATTN_WIDTH + KV_WIDTH], kw_ref[...])
    v = qkv[:, ATTN_WIDTH + KV_WIDTH:]
    k_ref[...] = k
    v_ref[...] = v
    def head(g, j):
        mj = _half_mask(j)
        s = sink_ref[2 * g + j]
        logits = _mm_nt(jnp.where(mj, q[:, g * KV_WIDTH:(g + 1) * KV_WIDTH], 0.0), _dup_kv_head(k, g))
        yield
        m = jnp.maximum(jnp.max(logits, axis=-1, keepdims=True), s)
        e = jnp.exp(logits - m)
        den = jnp.sum(e, axis=-1, keepdims=True) + jnp.exp(s - m)
        pv = _mm(e, jnp.where(mj, _dup_kv_head(v, g), 0.0))
        yield
        return pv / den

    out = _run_interleaved([head(g, j) for g in range(N_KV_HEADS) for j in range(2)])
    for g in range(N_KV_HEADS):
        o_ref[:, g * KV_WIDTH:(g + 1) * KV_WIDTH] = out[2 * g] + out[2 * g + 1]


def _ctx_attention(proj, q_norm, k_norm, sink, batch, seq):
    blk = ATTN_WIDTH + 2 * KV_WIDTH
    return pl.pallas_call(
        _ctx_attn_kernel,
        grid=(batch,),
        in_specs=[
            pl.BlockSpec((seq, blk), lambda b: (b, 0)),
            pl.BlockSpec((1, ATTN_WIDTH), lambda b: (0, 0)),
            pl.BlockSpec((1, KV_WIDTH), lambda b: (0, 0)),
            pl.BlockSpec(memory_space=pltpu.SMEM),
        ],
        out_specs=[
            pl.BlockSpec((seq, ATTN_WIDTH), lambda b: (b, 0)),
            pl.BlockSpec((seq, KV_WIDTH), lambda b: (b, 0)),
            pl.BlockSpec((seq, KV_WIDTH), lambda b: (b, 0)),
        ],
        out_shape=[
            jax.ShapeDtypeStruct((batch * seq, ATTN_WIDTH), F32),
            jax.ShapeDtypeStruct((batch * seq, KV_WIDTH), F32),
            jax.ShapeDtypeStruct((batch * seq, KV_WIDTH), F32),
        ],
        compiler_params=_params("parallel"),
        name="ctx_attention",
    )(proj, q_norm, k_norm, sink)


def _rope(x, cos, sin_signed):
    width = x.shape[-1]
    lane = _iota((1, width), 1)
    first_half = (lane % (HEAD_DIM // 2)) < (HEAD_DIM // 4)
    partner = jnp.where(first_half, pltpu.roll(x, width - HEAD_DIM // 4, 1), pltpu.roll(x, HEAD_DIM // 4, 1))
    return x * cos + partner * sin_signed


def _lat_attn_kernel(seq, past, qkv_ref, kc_ref, vc_ref, qw_ref, kw_ref, cosq_ref, snq_ref, cosk_ref, snk_ref,
                     sink_ref, o_ref, q_s, k_s, v_s):
    qkv = qkv_ref[...]
    q = _rope(_head_rms(qkv[:, :ATTN_WIDTH], qw_ref[...]), cosq_ref[...], snq_ref[...]) * HEAD_DIM ** -0.5
    k = _rope(_head_rms(qkv[:, ATTN_WIDTH:ATTN_WIDTH + KV_WIDTH], kw_ref[...]), cosk_ref[...], snk_ref[...])
    v = qkv[:, ATTN_WIDTH + KV_WIDTH:]
    kc = kc_ref[...]
    vc = vc_ref[...]
    q_s[...] = q
    zero_blk = jnp.zeros((BLOCK, KV_WIDTH), BF16)
    for g in range(N_KV_HEADS):
        k_s[g, 0:BLOCK] = zero_blk
        k_s[g, BLOCK:BLOCK + seq] = _dup_kv_head(k, g).astype(BF16)
        k_s[g, BLOCK + seq:2 * BLOCK + seq] = zero_blk
        k_s[g, 2 * BLOCK + seq:] = _dup_kv_head(kc, g).astype(BF16)
        vg = _dup_kv_head(v, g)
        vcg = _dup_kv_head(vc, g)
        for j in range(2):
            mj = _half_mask(j)
            v_s[2 * g + j, 0:BLOCK] = zero_blk
            v_s[2 * g + j, BLOCK:BLOCK + seq] = jnp.where(mj, vg, 0.0).astype(BF16)
            v_s[2 * g + j, BLOCK + seq:2 * BLOCK + seq] = zero_blk
            v_s[2 * g + j, 2 * BLOCK + seq:] = jnp.where(mj, vcg, 0.0).astype(BF16)

    win = 3 * BLOCK
    ctx0 = 2 * BLOCK + seq

    def q_block(n, carry):
        q0 = pl.multiple_of(n * BLOCK, BLOCK)
        qpos = n * BLOCK + _iota((BLOCK, win), 0)
        kpos = (n - 1) * BLOCK + _iota((BLOCK, win), 1)
        valid = (jnp.abs(qpos - kpos) <= WINDOW) & (kpos >= 0) & (kpos < seq)

        def head(g, j):
            h = 2 * g + j
            s = sink_ref[h]
            qm = jnp.where(_half_mask(j), q_s[pl.ds(q0, BLOCK), g * KV_WIDTH:(g + 1) * KV_WIDTH], 0.0)
            lw = jnp.where(valid, _mm_nt(qm, k_s[g, pl.ds(q0, win), :]), NEG_INF)
            lc = _mm_nt(qm, k_s[g, ctx0:ctx0 + past, :])
            yield
            m = jnp.maximum(jnp.maximum(jnp.max(lw, axis=-1, keepdims=True),
                                        jnp.max(lc, axis=-1, keepdims=True)), s)
            ew = jnp.exp(lw - m)
            ec = jnp.exp(lc - m)
            den = jnp.sum(ew, axis=-1, keepdims=True) + jnp.sum(ec, axis=-1, keepdims=True) + jnp.exp(s - m)
            pv = (jnp.dot(ew.astype(BF16), v_s[h, pl.ds(q0, win), :], preferred_element_type=F32)
                  + jnp.dot(ec.astype(BF16), v_s[h, ctx0:ctx0 + past, :], preferred_element_type=F32))
            yield
            return pv / den

        out = _run_interleaved([head(g, j) for g in range(N_KV_HEADS) for j in range(2)])
        for g in range(N_KV_HEADS):
            o_ref[pl.ds(q0, BLOCK), g * KV_WIDTH:(g + 1) * KV_WIDTH] = out[2 * g] + out[2 * g + 1]
        return carry

    lax.fori_loop(0, seq // BLOCK, q_block, 0)


def _lat_attention(proj, tok_block0, cache_k, cache_v, q_norm, k_norm, rope_q, rope_k, sink, batch, seq, past):
    blk = ATTN_WIDTH + 2 * KV_WIDTH
    cos_q, sin_q = rope_q
    cos_k, sin_k = rope_k
    rows = 2 * BLOCK + seq + past
    const = lambda shape: pl.BlockSpec(shape, lambda b: (0, 0))
    return pl.pallas_call(
        functools.partial(_lat_attn_kernel, seq, past),
        grid=(batch,),
        in_specs=[
            pl.BlockSpec((seq, blk), lambda b: (tok_block0 + b, 0)),
            pl.BlockSpec((None, past, KV_WIDTH), lambda b: (b, 0, 0)),
            pl.BlockSpec((None, past, KV_WIDTH), lambda b: (b, 0, 0)),
            const((1, ATTN_WIDTH)),
            const((1, KV_WIDTH)),
            const((seq, ATTN_WIDTH)),
            const((seq, ATTN_WIDTH)),
            const((seq, KV_WIDTH)),
            const((seq, KV_WIDTH)),
            pl.BlockSpec(memory_space=pltpu.SMEM),
        ],
        out_specs=pl.BlockSpec((seq, ATTN_WIDTH), lambda b: (b, 0)),
        out_shape=jax.ShapeDtypeStruct((batch * seq, ATTN_WIDTH), F32),
        scratch_shapes=[
            pltpu.VMEM((seq, ATTN_WIDTH), F32),
            pltpu.VMEM((N_KV_HEADS, rows, KV_WIDTH), BF16),
            pltpu.VMEM((N_Q_HEADS, rows, KV_WIDTH), BF16),
        ],
        compiler_params=_params("parallel"),
        name="latent_attention",
    )(proj, cache_k, cache_v, q_norm, k_norm, cos_q, sin_q, cos_k, sin_k, sink)


def _convfour_kernel(seq, bx_ref, bb_ref, bc_ref, dx_ref, cw_ref, ct_ref, st_ref, cc_ref, sc_ref, conv_ref, four_ref):
    u = bc_ref[...] * bx_ref[...]
    row = _iota((seq, 1), 0)
    prev = jnp.where(row == 0, 0.0, pltpu.roll(u, 1, 0))
    nxt = jnp.where(row == seq - 1, 0.0, pltpu.roll(u, seq - 1, 0))
    cw = cw_ref[...]
    conv_ref[...] = bb_ref[...] * (cw[0:1] * prev + cw[1:2] * u + cw[2:3] * nxt)
    x = dx_ref[...].astype(BF16)
    a = jnp.dot(ct_ref[...], x, preferred_element_type=F32)
    b = jnp.dot(st_ref[...], x, preferred_element_type=F32)
    y = (jnp.dot(a.astype(BF16), cc_ref[...], preferred_element_type=F32)
         - jnp.dot(b.astype(BF16), sc_ref[...], preferred_element_type=F32))
    four_ref[...] = y * (1.0 / math.sqrt(seq * HEAD_DIM))


def _dft_tables(n, groups=1):
    idx = np.arange(n, dtype=np.int64)
    ang = ((idx[:, None] * idx[None, :]) % n).astype(np.float64) * (2.0 * math.pi / n)
    eye = np.eye(groups)
    return jnp.asarray(np.kron(eye, np.cos(ang)), F32), jnp.asarray(np.kron(eye, np.sin(ang)), F32)


def _convfour(proj, tok_block0, conv_w, batch, seq):
    ct, st = _dft_tables(seq)
    cc, sc = _dft_tables(HEAD_DIM, FOURIER_GROUPS)
    col = lambda c: pl.BlockSpec((seq, COLBLK), lambda b: (tok_block0 + b, c))
    const = lambda shape: pl.BlockSpec(shape, lambda b: (0, 0))
    out = pl.BlockSpec((seq, COLBLK), lambda b: (b, 0))
    return pl.pallas_call(
        functools.partial(_convfour_kernel, seq),
        grid=(batch,),
        in_specs=[col(2), col(3), col(4), col(9), const((8, CONV_CH)), const((seq, seq)), const((seq, seq)),
                  const((FOURIER_WIDTH, FOURIER_WIDTH)), const((FOURIER_WIDTH, FOURIER_WIDTH))],
        out_specs=[out, out],
        out_shape=[jax.ShapeDtypeStruct((batch * seq, COLBLK), F32)] * 2,
        compiler_params=_params("parallel"),
        name="conv_fourier",
    )(proj, proj, proj, proj, conv_w, ct.astype(BF16), st.astype(BF16), cc.astype(BF16), sc.astype(BF16))


def _rwkv_kernel(latent, seq, *refs):
    if latent:
        (r_ref, k_ref, v_ref, lr_ref, lrw_ref, bias_ref, kk_ref, ka_ref, rk_ref, lnw_ref, lnb_ref, s0_ref,
         o_ref, logw_s, kd_s, b_s, z_s, g_s, y_s, st_s) = refs
    else:
        (r_ref, k_ref, v_ref, lr_ref, lrw_ref, bias_ref, kk_ref, ka_ref, rk_ref, lnw_ref, lnb_ref,
         o_ref, sfin_ref, logw_s, kd_s, b_s, z_s, g_s, y_s, st_s) = refs
    W = RWKV_WIDTH
    C = RWKV_CHUNK
    gsum = _group_sum_matrix(W)
    lane = _iota((1, W), 1)

    def prep(i, carry):
        rows = pl.ds(pl.multiple_of(i * RWKV_PREP_ROWS, RWKV_PREP_ROWS), RWKV_PREP_ROWS)
        lr = lr_ref[rows, :]
        f = jnp.where(lane < 2 * DECAY_RANK, jnp.tanh(lr),
                      jnp.where(lane < 2 * DECAY_RANK + 2 * ICLR_RANK, lr, _sigmoid(lr)))
        low = jnp.dot(f.astype(BF16), lrw_ref[...], preferred_element_type=F32) + bias_ref[...]
        logw_s[rows, :] = DECAY_SCALE * _sigmoid(low[:, :2 * W])
        a = _sigmoid(low[:, 2 * W:4 * W])
        g_s[rows, :] = low[:, 4 * W:]
        k = k_ref[rows, :]
        kk = k * kk_ref[...]
        kk = kk / jnp.maximum(jnp.sqrt(_mm_split(kk * kk, gsum)), 1e-12)
        z_s[rows, :] = -kk
        for j in range(2):
            aj = a[:, j * W:(j + 1) * W]
            kd_s[rows, j * W:(j + 1) * W] = k * (1.0 + (aj - 1.0) * ka_ref[...])
            b_s[rows, j * W:(j + 1) * W] = kk * aj
        return carry

    lax.fori_loop(0, seq // RWKV_PREP_ROWS, prep, 0)

    y_s[...] = jnp.zeros((seq, W), F32)
    st_s[...] = jnp.zeros((2, W, W), F32)
    if latent:
        for j in range(2):
            for h in range(RWKV_HEADS):
                blk = slice(h * HEAD_DIM, (h + 1) * HEAD_DIM)
                st_s[j, blk, blk] = s0_ref[j, h].T

    HC = RWKV_HEADS * C
    wide_t = _iota((C, HC), 0)
    wide_s = _iota((C, HC), 1) % C
    eye_wide = jnp.where(wide_t == wide_s, 1.0, 0.0).astype(F32)
    same_head = (_iota((HC, HC), 0) // C) == (_iota((HC, HC), 1) // C)
    bd = (_iota((W, W), 0) // HEAD_DIM) == (_iota((W, W), 1) // HEAD_DIM)
    eye_head = _iota((HEAD_DIM, W), 0) == _iota((HEAD_DIM, W), 1) % HEAD_DIM
    tc_i = _iota((C, C), 0)
    sc_i = _iota((C, C), 1)
    head_masks = [(lane >= h * HEAD_DIM) & (lane < (h + 1) * HEAD_DIM) for h in range(RWKV_HEADS)]
    n_chunks = seq // C

    def stack_masked(x):
        return jnp.concatenate([jnp.where(m, x, 0.0) for m in head_masks], axis=0).astype(BF16)

    def heads_transposed(x):
        xt = x.T
        return jnp.concatenate([xt[h * HEAD_DIM:(h + 1) * HEAD_DIM] for h in range(RWKV_HEADS)],
                               axis=1).astype(BF16)

    def block_diag(x, mask):
        return jnp.where(mask, jnp.concatenate([x] * RWKV_HEADS, axis=0), 0.0)

    def bdot(a, b):
        return jnp.dot(a, b, preferred_element_type=F32)

    def chunk(j, c):
        rows = pl.ds(pl.multiple_of(c * C, C), C)
        cols = slice(j * W, (j + 1) * W)
        rc, vc, zc = r_ref[rows, :], v_ref[rows, :], z_s[rows, :]
        lw, kd, bc = logw_s[rows, cols], kd_s[rows, cols], b_s[rows, cols]
        if j == 0:
            strict, incl, tri = wide_s < wide_t, wide_s <= wide_t, sc_i <= tc_i
        else:
            strict, incl, tri = wide_s > wide_t, wide_s >= wide_t, sc_i >= tc_i
        cl = _mm_split(jnp.where(tri, 1.0, 0.0).astype(BF16), lw, split_rhs=True)
        yield
        tot = cl[C - 1:C, :] if j == 0 else cl[0:1, :]
        w_in, w_ex, w_inv, w_rest = jnp.exp(cl), jnp.exp(cl - lw), jnp.exp(-cl), jnp.exp(tot - cl)
        rt = rc * w_in
        zs = stack_masked(zc * w_ex)
        vs = stack_masked(vc)
        aa = lax.dot_general(jnp.concatenate([zc * w_ex, rt], axis=0).astype(BF16),
                             jnp.concatenate([stack_masked(bc * w_inv), stack_masked(kd * w_inv)], axis=0),
                             (((1,), (1,)), ((), ())), preferred_element_type=F32)
        yield
        a_zb = jnp.where(strict, aa[:C, :HC], 0.0)
        a_zk = jnp.where(strict, aa[:C, HC:], 0.0).astype(BF16)
        a_rb = jnp.where(incl, aa[C:, :HC], 0.0).astype(BF16)
        a_rk = jnp.where(incl, aa[C:, HC:], 0.0).astype(BF16)
        tinv = eye_wide + a_zb
        apow = bdot(a_zb.astype(BF16), block_diag(a_zb, same_head).astype(BF16))
        av = bdot(a_zk, vs)
        rk = bdot(a_rk, vs)
        yield
        for _ in range(int(math.log2(C)) - 2):
            both = bdot(jnp.concatenate([apow, tinv], axis=0).astype(BF16), block_diag(apow, same_head).astype(BF16))
            yield
            apow = both[:C]
            tinv = tinv + both[C:]
        tinv = tinv + bdot(tinv.astype(BF16), block_diag(apow, same_head).astype(BF16))
        yield
        x = bdot(tinv.astype(BF16), jnp.concatenate([zs, stack_masked(av)], axis=1))
        yield
        xs = jnp.concatenate([stack_masked(x[:, :W]), stack_masked(x[:, W:])], axis=1)
        ry = bdot(a_rb, xs)
        bh_t = heads_transposed(bc * w_rest)
        mn = bdot(bh_t, xs)
        nk = bdot(heads_transposed(kd * w_rest), vs)
        yield
        rz = rt + ry[:, :W]
        y0 = ry[:, W:] + rk
        mt_wide = mn[:, :W] + jnp.where(eye_head, jnp.exp(tot), 0.0)
        return rows, rz.astype(BF16), y0, mt_wide.astype(BF16), mn[:, W:] + nk

    def advance(j, local):
        rows, rz, y0, mt_wide, nt_wide = local
        s_prev = st_s[j].astype(BF16)
        y_s[rows, :] += bdot(rz, s_prev) + y0
        st_s[j] = block_diag(bdot(mt_wide, s_prev) + nt_wide, bd)

    def step(i, carry):
        n = RWKV_CHUNKS_PER_STEP
        local = _run_interleaved([chunk(0, n * i + u) for u in range(n)]
                                 + [chunk(1, n_chunks - 1 - (n * i + u)) for u in range(n)])
        for u in range(n):
            advance(0, local[u])
            advance(1, local[n + u])
        return carry

    lax.fori_loop(0, n_chunks // RWKV_CHUNKS_PER_STEP, step, 0)

    if not latent:
        for j in range(2):
            for h in range(RWKV_HEADS):
                blk = slice(h * HEAD_DIM, (h + 1) * HEAD_DIM)
                sfin_ref[j, h] = st_s[j, blk, blk].T

    def finish(i, carry):
        rows = pl.ds(pl.multiple_of(i * RWKV_PREP_ROWS, RWKV_PREP_ROWS), RWKV_PREP_ROWS)
        y = y_s[rows, :]
        mu = _mm_split(y, gsum) * (1.0 / HEAD_DIM)
        d = y - mu
        var = _mm_split(d * d, gsum) * (1.0 / HEAD_DIM)
        yn = d * lax.rsqrt(var + GN_EPS) * lnw_ref[...] + lnb_ref[...]
        r, k, v = r_ref[rows, :], k_ref[rows, :], v_ref[rows, :]
        bonus = _mm_split(r * k * rk_ref[...], gsum) * v
        o_ref[rows, :] = (yn + bonus) * g_s[rows, :]
        return carry

    lax.fori_loop(0, seq // RWKV_PREP_ROWS, finish, 0)


def _rwkv(proj, tok_block0, p, batch, seq, s0=None):
    latent = s0 is not None
    W = RWKV_WIDTH
    col = lambda c: pl.BlockSpec((seq, COLBLK), lambda b: (tok_block0 + b, c))
    const = lambda shape: pl.BlockSpec(shape, lambda b: (0,) * len(shape))
    in_specs = [col(5), col(6), col(7), col(8), const((COLBLK, 5 * W)), const((1, 5 * W))] + [const((1, W))] * 5
    args = [proj, proj, proj, proj, p["lrw"], p["lr_bias"], p["k_k"], p["k_a"], p["r_k"], p["ln_w"], p["ln_b"]]
    out_specs = [pl.BlockSpec((seq, W), lambda b: (b, 0))]
    out_shape = [jax.ShapeDtypeStruct((batch * seq, W), F32)]
    if latent:
        in_specs.append(pl.BlockSpec((None, 2, RWKV_HEADS, HEAD_DIM, HEAD_DIM), lambda b: (b, 0, 0, 0, 0)))
        args.append(s0)
    else:
        out_specs.append(pl.BlockSpec((None, 2, RWKV_HEADS, HEAD_DIM, HEAD_DIM), lambda b: (b, 0, 0, 0, 0)))
        out_shape.append(jax.ShapeDtypeStruct((batch, 2, RWKV_HEADS, HEAD_DIM, HEAD_DIM), F32))
    res = pl.pallas_call(
        functools.partial(_rwkv_kernel, latent, seq),
        grid=(batch,),
        in_specs=in_specs,
        out_specs=out_specs,
        out_shape=out_shape,
        scratch_shapes=[
            pltpu.VMEM((seq, 2 * W), F32),
            pltpu.VMEM((seq, 2 * W), F32),
            pltpu.VMEM((seq, 2 * W), F32),
            pltpu.VMEM((seq, W), F32),
            pltpu.VMEM((seq, W), F32),
            pltpu.VMEM((seq, W), F32),
            pltpu.VMEM((2, W, W), F32),
        ],
        compiler_params=_params("parallel"),
        name="rwkv_latent" if latent else "rwkv_ctx",
    )(*args)
    return res if not latent else res[0]


def _post_prologue(rows, x_ref, mix_refs, wout_ref, n2_ref, g1_ref, sh_ref, sc_ref):
    mix = None
    for i, m_ref in enumerate(mix_refs):
        part = jnp.dot(m_ref[rows, :].astype(BF16), wout_ref[i * COLBLK:(i + 1) * COLBLK, :],
                       preferred_element_type=F32)
        mix = part if mix is None else mix + part
    xn = x_ref[rows, :] + g1_ref[...] * mix
    return xn, _rms(xn, n2_ref[...]) * (1.0 + sc_ref[...]) + sh_ref[...]


def _swiglu(hb, w1_ref, w3_ref, w2_ref):
    t = _silu(jnp.dot(hb, w1_ref[...], preferred_element_type=F32)) * jnp.dot(hb, w3_ref[...],
                                                                             preferred_element_type=F32)
    return jnp.dot(t.astype(BF16), w2_ref[...], preferred_element_type=F32)


def _post_dense_kernel(x_ref, a_ref, c_ref, rw_ref, f_ref, wout_ref, n2_ref, g1_ref, sh_ref, sc_ref, g2_ref,
                       w1_ref, w3_ref, w2_ref, o_ref, h_s):
    f = pl.program_id(1)

    @pl.when(f == 0)
    def _():
        xn, h = _post_prologue(slice(None), x_ref, (a_ref, c_ref, rw_ref, f_ref), wout_ref, n2_ref, g1_ref, sh_ref,
                               sc_ref)
        o_ref[...] = xn
        h_s[...] = h.astype(BF16)

    o_ref[...] += g2_ref[...] * _swiglu(h_s[...], w1_ref, w3_ref, w2_ref)


def _post_moe_kernel(x_ref, a_ref, c_ref, rw_ref, f_ref, wout_ref, n2_ref, g1_ref, sh_ref, sc_ref, g2_ref,
                     router_ref, order_ref, w1_ref, w3_ref, w2_ref, o_ref, h_s, gate_s, rank_s, cnt_s):
    e = pl.program_id(1)
    lane = _iota((1, LANES), 1)
    sub = MOE_SUB_TILE
    n_sub = x_ref.shape[0] // sub

    @pl.when(e == 0)
    def _():
        for u in range(n_sub):
            rows = slice(u * sub, (u + 1) * sub)
            xn, h = _post_prologue(rows, x_ref, (a_ref, c_ref, rw_ref, f_ref), wout_ref, n2_ref, g1_ref, sh_ref,
                                   sc_ref)
            o_ref[rows, :] = xn
            h_s[rows, :] = h.astype(BF16)
            logits = _mm_split2(h, router_ref[...]).T[:N_EXPERTS]
            ex = jnp.exp(logits - jnp.max(logits, axis=0, keepdims=True))
            probs = ex / jnp.sum(ex, axis=0, keepdims=True)
            expert = _iota(probs.shape, 0)
            p1 = jnp.max(probs, axis=0, keepdims=True)
            i1 = jnp.min(jnp.where(probs == p1, expert, N_EXPERTS), axis=0, keepdims=True)
            rest = jnp.where(expert == i1, -1.0, probs)
            p2 = jnp.max(rest, axis=0, keepdims=True)
            i2 = jnp.min(jnp.where(rest == p2, expert, N_EXPERTS), axis=0, keepdims=True)
            top = p1 + p2
            gates_t = jnp.where(expert == i1, p1 / top, jnp.where(expert == i2, p2 / top, 0.0))
            gate_s[rows, :] = jnp.concatenate([gates_t, jnp.zeros((LANES - N_EXPERTS, sub), F32)], axis=0).T
            sel_t = jnp.where((expert == i1) | (expert == i2), 1.0, 0.0)
            rank = jnp.dot(sel_t.astype(BF16), order_ref[...], preferred_element_type=F32)
            rank_s[u] = jnp.where(sel_t > 0.0, rank, -1.0)
            for ex_i in range(N_EXPERTS):
                cnt_s[u * N_EXPERTS + ex_i] = jnp.sum(sel_t[ex_i:ex_i + 1, :]).astype(jnp.int32)

    routed = cnt_s[e]
    for u in range(1, n_sub):
        routed = jnp.maximum(routed, cnt_s[u * N_EXPERTS + e])

    def expert_rows(u, capacity):
        rows = slice(u * sub, (u + 1) * sub)
        if capacity is None:
            hb = h_s[rows, :]
        else:
            slot = _iota((capacity, sub), 0).astype(F32)
            pick = jnp.where(rank_s[u, pl.ds(e, 1), :] == slot, 1.0, 0.0)
            hb = jnp.dot(pick.astype(BF16), h_s[rows, :], preferred_element_type=F32).astype(BF16)
            yield
        t1 = jnp.dot(hb, w1_ref[...], preferred_element_type=F32)
        t3 = jnp.dot(hb, w3_ref[...], preferred_element_type=F32)
        yield
        y = jnp.dot((_silu(t1) * t3).astype(BF16), w2_ref[...], preferred_element_type=F32)
        yield
        if capacity is not None:
            y = _mm(pick.T, y)
            yield
        gate = jnp.sum(jnp.where(lane == e, gate_s[rows, :], 0.0), axis=-1, keepdims=True)
        o_ref[rows, :] += gate * (g2_ref[...] * y)

    def run(capacity):
        _run_interleaved([expert_rows(u, capacity) for u in range(n_sub)])

    below = 0
    for capacity in MOE_CAPACITIES:
        pl.when((routed > below) & (routed <= capacity))(functools.partial(run, capacity))
        below = capacity
    pl.when(routed > below)(functools.partial(run, None))


def _post(x, mixes, w_out, norm_w, mod, row_of_tile, ffn, moe):
    n_tok = x.shape[0]
    tok = MOE_TOKEN_TILE if moe else TOKEN_TILE
    tile = lambda width: pl.BlockSpec((tok, width), lambda i, f: (i, 0))
    const = lambda shape: pl.BlockSpec(shape, lambda i, f: (0, 0))
    mod_spec = lambda chunk: _mod_spec(chunk, functools.partial(row_of_tile, tok))
    in_specs = [tile(D_MODEL)] + [tile(COLBLK)] * 4 + [const((MIX_WIDTH, D_MODEL)), const((1, D_MODEL)),
                                                       mod_spec(2), mod_spec(3), mod_spec(4), mod_spec(5)]
    args = [x, *mixes, w_out, norm_w, mod, mod, mod, mod]
    scratch = [pltpu.VMEM((tok, D_MODEL), BF16)]
    if moe:
        router, w1, w3, w2 = ffn
        sub = MOE_SUB_TILE
        order = jnp.where(_iota((sub, sub), 0) < _iota((sub, sub), 1), 1.0, 0.0).astype(BF16)
        in_specs += [const((D_MODEL, LANES)), const((sub, sub)),
                     pl.BlockSpec((None, D_MODEL, D_FF_EXPERT), lambda i, e: (e, 0, 0)),
                     pl.BlockSpec((None, D_MODEL, D_FF_EXPERT), lambda i, e: (e, 0, 0)),
                     pl.BlockSpec((None, D_FF_EXPERT, D_MODEL), lambda i, e: (e, 0, 0))]
        args += [router, order, w1, w3, w2]
        scratch += [pltpu.VMEM((tok, LANES), F32), pltpu.VMEM((tok // sub, N_EXPERTS, sub), F32),
                    pltpu.SMEM((tok // sub * N_EXPERTS,), jnp.int32)]
        body, steps, name = _post_moe_kernel, N_EXPERTS, "post_moe"
    else:
        w1, w3, w2 = ffn
        in_specs += [pl.BlockSpec((D_MODEL, FF_TILE), lambda i, f: (0, f)),
                     pl.BlockSpec((D_MODEL, FF_TILE), lambda i, f: (0, f)),
                     pl.BlockSpec((FF_TILE, D_MODEL), lambda i, f: (f, 0))]
        args += [w1, w3, w2]
        body, steps, name = _post_dense_kernel, D_FF // FF_TILE, "post_dense"
    return pl.pallas_call(
        body,
        grid=(n_tok // tok, steps),
        in_specs=in_specs,
        out_specs=tile(D_MODEL),
        out_shape=jax.ShapeDtypeStruct((n_tok, D_MODEL), F32),
        scratch_shapes=scratch,
        compiler_params=_params("parallel", "arbitrary"),
        name=name,
    )(*args)


def _rope_tables(n_tokens, n_heads):
    n_rows = n_tokens // GRID_W
    row = np.repeat(np.arange(n_rows), GRID_W).astype(np.float32)
    col = np.tile(np.arange(GRID_W), n_rows).astype(np.float32)
    quarter = HEAD_DIM // 4
    inv_freq = np.float32(ROPE_BASE) ** (-np.arange(quarter, dtype=np.float32) / np.float32(quarter))
    ang_r, ang_c = row[:, None] * inv_freq, col[:, None] * inv_freq
    cos = np.concatenate([np.cos(ang_r)] * 2 + [np.cos(ang_c)] * 2, axis=1)
    sin = np.concatenate([-np.sin(ang_r), np.sin(ang_r), -np.sin(ang_c), np.sin(ang_c)], axis=1)
    return jnp.asarray(np.tile(cos, (1, n_heads)), F32), jnp.asarray(np.tile(sin, (1, n_heads)), F32)


def _layout_w_in(w):
    n_main = w.shape[1] - FOURIER_WIDTH
    return w[:, :n_main].astype(BF16), w[:, n_main:].astype(BF16)


def _layout_rwkv(l, w0, w_up, a0, a_up, g_up, k_k, k_a, r_k, ln_w, ln_b):
    W = RWKV_WIDTH
    lrw = jnp.zeros((COLBLK, 5 * W), F32)
    for j in range(2):
        lrw = lrw.at[j * DECAY_RANK:(j + 1) * DECAY_RANK, j * W:(j + 1) * W].set(w_up[l, j])
        r0 = 2 * DECAY_RANK + j * ICLR_RANK
        lrw = lrw.at[r0:r0 + ICLR_RANK, (2 + j) * W:(3 + j) * W].set(a_up[l, j])
    r0 = 2 * DECAY_RANK + 2 * ICLR_RANK
    lrw = lrw.at[r0:r0 + GATE_RANK, 4 * W:].set(g_up[l])
    bias = jnp.concatenate([w0[l, 0], w0[l, 1], a0[l, 0], a0[l, 1], jnp.zeros((W,), F32)])[None, :]
    row = lambda t: t[l].reshape(1, W)
    return {"lrw": lrw.astype(BF16), "lr_bias": bias, "k_k": row(k_k), "k_a": row(k_a), "r_k": row(r_k),
            "ln_w": row(ln_w), "ln_b": row(ln_b)}


def kernel(x_prompt, x_sample, cache_k, cache_v, state_wkv, c, c_ctx, mod_w, mod_b, norm1_w, norm2_w, w_in, q_norm_w, k_norm_w, attn_sink, conv_w, rwkv_w0, rwkv_w_up, rwkv_a0, rwkv_a_up, rwkv_g_up, rwkv_k_k, rwkv_k_a, rwkv_r_k, rwkv_ln_w, rwkv_ln_b, w_out, ffn_w1, ffn_w3, ffn_w2, router_w, moe_w1, moe_w3, moe_w2):
    batch, seq, _ = x_prompt.shape
    dec_batch, dec_seq, _ = x_sample.shape
    past = cache_k.shape[2]
    assert seq % BLOCK == 0 and dec_seq % MOE_TOKEN_TILE == 0 and (batch * seq) % MOE_TOKEN_TILE == 0
    assert 1 + dec_batch <= MOD_ROWS

    cond = jnp.concatenate([c_ctx[None, :], c, jnp.zeros((MOD_ROWS - 1 - dec_batch, D_MODEL), F32)], axis=0)
    mod_all = _modulation(cond, mod_w, mod_b)
    prompt_row = lambda tile, i: 0
    sample_row = lambda tile, i: 1 + (i * tile) // dec_seq

    rope_q = _rope_tables(dec_seq, N_Q_HEADS)
    rope_k = _rope_tables(dec_seq, N_KV_HEADS)

    xp = x_prompt.reshape(batch * seq, D_MODEL)
    xs = x_sample.reshape(dec_batch * dec_seq, D_MODEL)
    new_k, new_v, new_s = [], [], []
    for l in range(DEPTH):
        mod = mod_all[l].reshape(MOD_ROWS, 6, 1, D_MODEL)
        w_in_l = _layout_w_in(w_in[l])
        n1 = norm1_w[l][None, :]
        n2 = norm2_w[l][None, :]
        qn = jnp.tile(q_norm_w[l], N_Q_HEADS)[None, :]
        kn = jnp.tile(k_norm_w[l], N_KV_HEADS)[None, :]
        cw = jnp.concatenate([conv_w[l], jnp.zeros((5, CONV_CH), F32)], axis=0)
        rp = _layout_rwkv(l, rwkv_w0, rwkv_w_up, rwkv_a0, rwkv_a_up, rwkv_g_up, rwkv_k_k, rwkv_k_a, rwkv_r_k,
                          rwkv_ln_w, rwkv_ln_b)
        w_out_l = w_out[l].astype(BF16)
        if l % 2 == 0:
            j = l // 2
            ffn = (ffn_w1[j].astype(BF16), ffn_w3[j].astype(BF16), ffn_w2[j].astype(BF16))
        else:
            j = l // 2
            router = jnp.concatenate([router_w[j], jnp.zeros((D_MODEL, LANES - N_EXPERTS), F32)], axis=1)
            ffn = (router, moe_w1[j].astype(BF16), moe_w3[j].astype(BF16), moe_w2[j].astype(BF16))

        proj_p = _in_proj(xp, n1, mod, prompt_row, w_in_l)
        attn_p, k_p, v_p = _ctx_attention(proj_p, qn, kn, attn_sink[l], batch, seq)
        conv_p, four_p = _convfour(proj_p, 0, cw, batch, seq)
        rw_p, sfin = _rwkv(proj_p, 0, rp, batch, seq)
        xp = _post(xp, (attn_p, conv_p, rw_p, four_p), w_out_l, n2, mod, prompt_row, ffn, l % 2 == 1)
        new_k.append(k_p.reshape(batch, seq, N_KV_HEADS, HEAD_DIM))
        new_v.append(v_p.reshape(batch, seq, N_KV_HEADS, HEAD_DIM))
        new_s.append(sfin)

        proj_s = _in_proj(xs, n1, mod, sample_row, w_in_l)
        kc = cache_k[:, l].reshape(dec_batch, past, KV_WIDTH)
        vc = cache_v[:, l].reshape(dec_batch, past, KV_WIDTH)
        attn_s = _lat_attention(proj_s, 0, kc, vc, qn, kn, rope_q, rope_k, attn_sink[l], dec_batch, dec_seq, past)
        conv_s, four_s = _convfour(proj_s, 0, cw, dec_batch, dec_seq)
        rw_s = _rwkv(proj_s, 0, rp, dec_batch, dec_seq, s0=state_wkv[:, l])
        xs = _post(xs, (attn_s, conv_s, rw_s, four_s), w_out_l, n2, mod, sample_row, ffn, l % 2 == 1)

    return (xp.reshape(batch, seq, D_MODEL), xs.reshape(dec_batch, dec_seq, D_MODEL),
            jnp.stack(new_k, axis=1), jnp.stack(new_v, axis=1), jnp.stack(new_s, axis=1))
```

```python
import functools
import math

import jax
import jax.numpy as jnp
import numpy as np
from jax import lax
from jax.experimental import pallas as pl
from jax.experimental.pallas import tpu as pltpu

F32 = jnp.float32
BF16 = jnp.bfloat16

D_MODEL = 1024
DEPTH = 2
GRID_W = 64
HEAD_DIM = 64
N_Q_HEADS = 4
N_KV_HEADS = 2
WINDOW = 128
BLOCK = 128
ROPE_BASE = 10000.0
ATTN_WIDTH = N_Q_HEADS * HEAD_DIM
KV_WIDTH = N_KV_HEADS * HEAD_DIM
CONV_CH = 256
RWKV_HEADS = 4
RWKV_WIDTH = RWKV_HEADS * HEAD_DIM
DECAY_RANK = 32
ICLR_RANK = 32
GATE_RANK = 64
FOURIER_GROUPS = 4
FOURIER_WIDTH = FOURIER_GROUPS * HEAD_DIM
MIX_WIDTH = ATTN_WIDTH + CONV_CH + RWKV_WIDTH + FOURIER_WIDTH
D_FF = 2816
N_EXPERTS = 8
D_FF_EXPERT = 1024
NORM_EPS = 1e-6
GN_EPS = 64e-5
NEG_INF = -1e30

LANES = 128
VMEM_LIMIT_BYTES = 56 * 1024 * 1024

PROJ_WIDTH = 2560
COLBLK = 256
LOWRANK_WIDTH = 2 * DECAY_RANK + 2 * ICLR_RANK + GATE_RANK
MOD_ROWS = 16
TOKEN_TILE = 512
RWKV_CHUNK = 64
RWKV_CHUNKS_PER_STEP = 4
RWKV_PREP_ROWS = 256
FF_TILE = 1408
MOE_TOKEN_TILE = 1024
MOE_SUB_TILE = 512
MOE_CAPACITIES = (128, 160, 192, 224)
DECAY_SCALE = -math.exp(-0.5)


def _params(*sem):
    return pltpu.CompilerParams(dimension_semantics=sem, vmem_limit_bytes=VMEM_LIMIT_BYTES)


def _mm(a, b):
    return jnp.dot(a.astype(BF16), b.astype(BF16), preferred_element_type=F32)


def _mm_nt(a, b):
    return lax.dot_general(a.astype(BF16), b.astype(BF16), (((1,), (1,)), ((), ())), preferred_element_type=F32)


def _mm_split(a, b, split_rhs=False, parts=2):
    exact, x = (a, b) if split_rhs else (b, a)
    acc = None
    for _ in range(parts):
        piece = x.astype(BF16)
        x = x - piece.astype(F32)
        term = (jnp.dot(exact, piece, preferred_element_type=F32) if split_rhs
                else jnp.dot(piece, exact, preferred_element_type=F32))
        acc = term if acc is None else acc + term
    return acc


def _sigmoid(x):
    return 0.5 * jnp.tanh(0.5 * x) + 0.5


def _mm_split2(a, b):
    a_hi, b_hi = a.astype(BF16), b.astype(BF16)
    a_lo = (a - a_hi.astype(F32)).astype(BF16)
    b_lo = (b - b_hi.astype(F32)).astype(BF16)
    dot = functools.partial(jnp.dot, preferred_element_type=F32)
    return dot(a_hi, b_hi) + dot(a_lo, b_hi) + dot(a_hi, b_lo)


def _silu(x):
    return x * _sigmoid(x)


def _iota(shape, dim):
    return lax.broadcasted_iota(jnp.int32, shape, dim)


def _run_interleaved(gens):
    results = [None] * len(gens)
    pending = list(range(len(gens)))
    while pending:
        for idx in list(pending):
            try:
                next(gens[idx])
            except StopIteration as done:
                results[idx] = done.value
                pending.remove(idx)
    return results


def _mod_kernel(c_ref, w_ref, b_ref, o_ref):
    o_ref[0] = _mm(_silu(c_ref[...]), w_ref[0]) + b_ref[0]


def _modulation(cond, mod_w, mod_b):
    n_chunks = mod_w.shape[-1] // D_MODEL
    return pl.pallas_call(
        _mod_kernel,
        grid=(DEPTH, n_chunks),
        in_specs=[
            pl.BlockSpec((MOD_ROWS, D_MODEL), lambda l, j: (0, 0)),
            pl.BlockSpec((1, D_MODEL, D_MODEL), lambda l, j: (l, 0, j)),
            pl.BlockSpec((1, 1, D_MODEL), lambda l, j: (l, 0, j)),
        ],
        out_specs=pl.BlockSpec((1, MOD_ROWS, D_MODEL), lambda l, j: (l, 0, j)),
        out_shape=jax.ShapeDtypeStruct((DEPTH, MOD_ROWS, n_chunks * D_MODEL), F32),
        compiler_params=_params("parallel", "parallel"),
        name="modulation",
    )(cond, mod_w, mod_b.reshape(DEPTH, 1, -1))


def _mod_spec(chunk, row_of_tile):
    return pl.BlockSpec((None, None, 1, D_MODEL), lambda i, *_: (row_of_tile(i), chunk, 0, 0))


def _rms(x, w):
    return x * lax.rsqrt(jnp.mean(x * x, axis=-1, keepdims=True) + NORM_EPS) * w


def _in_proj_kernel(x_ref, nw_ref, sh_ref, sc_ref, w_main_ref, w_four_ref, o_ref):
    h = (_rms(x_ref[...], nw_ref[...]) * (1.0 + sc_ref[...]) + sh_ref[...]).astype(BF16)
    n_main = w_main_ref.shape[1]
    gap_end = PROJ_WIDTH - FOURIER_WIDTH
    o_ref[:, :n_main] = jnp.dot(h, w_main_ref[...], preferred_element_type=F32)
    o_ref[:, n_main:gap_end] = jnp.zeros((h.shape[0], gap_end - n_main), F32)
    o_ref[:, gap_end:] = jnp.dot(h, w_four_ref[...], preferred_element_type=F32)


def _in_proj(x, norm_w, mod, row_of_tile, w_in):
    row_of_tile = functools.partial(row_of_tile, TOKEN_TILE)
    n_tok = x.shape[0]
    return pl.pallas_call(
        _in_proj_kernel,
        grid=(n_tok // TOKEN_TILE,),
        in_specs=[
            pl.BlockSpec((TOKEN_TILE, D_MODEL), lambda i: (i, 0)),
            pl.BlockSpec((1, D_MODEL), lambda i: (0, 0)),
            _mod_spec(0, row_of_tile),
            _mod_spec(1, row_of_tile),
            pl.BlockSpec(w_in[0].shape, lambda i: (0, 0)),
            pl.BlockSpec(w_in[1].shape, lambda i: (0, 0)),
        ],
        out_specs=pl.BlockSpec((TOKEN_TILE, PROJ_WIDTH), lambda i: (i, 0)),
        out_shape=jax.ShapeDtypeStruct((n_tok, PROJ_WIDTH), F32),
        compiler_params=_params("parallel"),
        name="in_proj",
    )(x, norm_w, mod, mod, *w_in)


def _group_sum_matrix(width):
    r = _iota((width, width), 0) // HEAD_DIM
    c = _iota((width, width), 1) // HEAD_DIM
    return jnp.where(r == c, 1.0, 0.0).astype(BF16)


def _head_rms(x, w):
    ms = _mm_split(x * x, _group_sum_matrix(x.shape[-1])) * (1.0 / HEAD_DIM)
    return x * lax.rsqrt(ms + NORM_EPS) * w


def _dup_kv_head(x, g):
    lane = _iota((1, KV_WIDTH), 1)
    rolled = pltpu.roll(x, HEAD_DIM, 1)
    first = lane < HEAD_DIM
    return jnp.where(first, x, rolled) if g == 0 else jnp.where(first, rolled, x)


def _half_mask(j):
    lane = _iota((1, KV_WIDTH), 1)
    return (lane >= j * HEAD_DIM) & (lane < (j + 1) * HEAD_DIM)


def _ctx_attn_kernel(qkv_ref, qw_ref, kw_ref, sink_ref, o_ref, k_ref, v_ref):
    qkv = qkv_ref[...]
    q = _head_rms(qkv[:, :ATTN_WIDTH], qw_ref[...]) * HEAD_DIM ** -0.5
    k = _head_rms(qkv[:, ATTN_WIDTH:ATTN_WIDTH + KV_WIDTH], kw_ref[...])
    v = qkv[:, ATTN_WIDTH + KV_WIDTH:]
    k_ref[...] = k
    v_ref[...] = v
    def head(g, j):
        mj = _half_mask(j)
        s = sink_ref[2 * g + j]
        logits = _mm_nt(jnp.where(mj, q[:, g * KV_WIDTH:(g + 1) * KV_WIDTH], 0.0), _dup_kv_head(k, g))
        yield
        m = jnp.maximum(jnp.max(logits, axis=-1, keepdims=True), s)
        e = jnp.exp(logits - m)
        den = jnp.sum(e, axis=-1, keepdims=True) + jnp.exp(s - m)
        pv = _mm(e, jnp.where(mj, _dup_kv_head(v, g), 0.0))
        yield
        return pv / den

    out = _run_interleaved([head(g, j) for g in range(N_KV_HEADS) for j in range(2)])
    for g in range(N_KV_HEADS):
        o_ref[:, g * KV_WIDTH:(g + 1) * KV_WIDTH] = out[2 * g] + out[2 * g + 1]


def _ctx_attention(proj, q_norm, k_norm, sink, batch, seq):
    blk = ATTN_WIDTH + 2 * KV_WIDTH
    return pl.pallas_call(
        _ctx_attn_kernel,
        grid=(batch,),
        in_specs=[
            pl.BlockSpec((seq, blk), lambda b: (b, 0)),
            pl.BlockSpec((1, ATTN_WIDTH), lambda b: (0, 0)),
            pl.BlockSpec((1, KV_WIDTH), lambda b: (0, 0)),
            pl.BlockSpec(memory_space=pltpu.SMEM),
        ],
        out_specs=[
            pl.BlockSpec((seq, ATTN_WIDTH), lambda b: (b, 0)),
            pl.BlockSpec((seq, KV_WIDTH), lambda b: (b, 0)),
            pl.BlockSpec((seq, KV_WIDTH), lambda b: (b, 0)),
        ],
        out_shape=[
            jax.ShapeDtypeStruct((batch * seq, ATTN_WIDTH), F32),
            jax.ShapeDtypeStruct((batch * seq, KV_WIDTH), F32),
            jax.ShapeDtypeStruct((batch * seq, KV_WIDTH), F32),
        ],
        compiler_params=_params("parallel"),
        name="ctx_attention",
    )(proj, q_norm, k_norm, sink)


def _rope(x, cos, sin_signed):
    width = x.shape[-1]
    lane = _iota((1, width), 1)
    first_half = (lane % (HEAD_DIM // 2)) < (HEAD_DIM // 4)
    partner = jnp.where(first_half, pltpu.roll(x, width - HEAD_DIM // 4, 1), pltpu.roll(x, HEAD_DIM // 4, 1))
    return x * cos + partner * sin_signed


def _lat_attn_kernel(seq, past, qkv_ref, kc_ref, vc_ref, qw_ref, kw_ref, cosq_ref, snq_ref, cosk_ref, snk_ref,
                     sink_ref, o_ref, q_s, k_s, v_s):
    qkv = qkv_ref[...]
    q = _rope(_head_rms(qkv[:, :ATTN_WIDTH], qw_ref[...]), cosq_ref[...], snq_ref[...]) * HEAD_DIM ** -0.5
    k = _rope(_head_rms(qkv[:, ATTN_WIDTH:ATTN_WIDTH + KV_WIDTH], kw_ref[...]), cosk_ref[...], snk_ref[...])
    v = qkv[:, ATTN_WIDTH + KV_WIDTH:]
    kc = kc_ref[...]
    vc = vc_ref[...]
    q_s[...] = q
    zero_blk = jnp.zeros((BLOCK, KV_WIDTH), BF16)
    for g in range(N_KV_HEADS):
        k_s[g, 0:BLOCK] = zero_blk
        k_s[g, BLOCK:BLOCK + seq] = _dup_kv_head(k, g).astype(BF16)
        k_s[g, BLOCK + seq:2 * BLOCK + seq] = zero_blk
        k_s[g, 2 * BLOCK + seq:] = _dup_kv_head(kc, g).astype(BF16)
        vg = _dup_kv_head(v, g)
        vcg = _dup_kv_head(vc, g)
        for j in range(2):
            mj = _half_mask(j)
            v_s[2 * g + j, 0:BLOCK] = zero_blk
            v_s[2 * g + j, BLOCK:BLOCK + seq] = jnp.where(mj, vg, 0.0).astype(BF16)
            v_s[2 * g + j, BLOCK + seq:2 * BLOCK + seq] = zero_blk
            v_s[2 * g + j, 2 * BLOCK + seq:] = jnp.where(mj, vcg, 0.0).astype(BF16)

    win = 3 * BLOCK
    ctx0 = 2 * BLOCK + seq

    def q_block(n, carry):
        q0 = pl.multiple_of(n * BLOCK, BLOCK)
        qpos = n * BLOCK + _iota((BLOCK, win), 0)
        kpos = (n - 1) * BLOCK + _iota((BLOCK, win), 1)
        valid = (jnp.abs(qpos - kpos) <= WINDOW) & (kpos >= 0) & (kpos < seq)

        def head(g, j):
            h = 2 * g + j
            s = sink_ref[h]
            qm = jnp.where(_half_mask(j), q_s[pl.ds(q0, BLOCK), g * KV_WIDTH:(g + 1) * KV_WIDTH], 0.0)
            lw = jnp.where(valid, _mm_nt(qm, k_s[g, pl.ds(q0, win), :]), NEG_INF)
            lc = _mm_nt(qm, k_s[g, ctx0:ctx0 + past, :])
            yield
            m = jnp.maximum(jnp.maximum(jnp.max(lw, axis=-1, keepdims=True),
                                        jnp.max(lc, axis=-1, keepdims=True)), s)
            ew = jnp.exp(lw - m)
            ec = jnp.exp(lc - m)
            den = jnp.sum(ew, axis=-1, keepdims=True) + jnp.sum(ec, axis=-1, keepdims=True) + jnp.exp(s - m)
            pv = (jnp.dot(ew.astype(BF16), v_s[h, pl.ds(q0, win), :], preferred_element_type=F32)
                  + jnp.dot(ec.astype(BF16), v_s[h, ctx0:ctx0 + past, :], preferred_element_type=F32))
            yield
            return pv / den

        out = _run_interleaved([head(g, j) for g in range(N_KV_HEADS) for j in range(2)])
        for g in range(N_KV_HEADS):
            o_ref[pl.ds(q0, BLOCK), g * KV_WIDTH:(g + 1) * KV_WIDTH] = out[2 * g] + out[2 * g + 1]
        return carry

    lax.fori_loop(0, seq // BLOCK, q_block, 0)


def _lat_attention(proj, tok_block0, cache_k, cache_v, q_norm, k_norm, rope_q, rope_k, sink, batch, seq, past):
    blk = ATTN_WIDTH + 2 * KV_WIDTH
    cos_q, sin_q = rope_q
    cos_k, sin_k = rope_k
    rows = 2 * BLOCK + seq + past
    const = lambda shape: pl.BlockSpec(shape, lambda b: (0, 0))
    return pl.pallas_call(
        functools.partial(_lat_attn_kernel, seq, past),
        grid=(batch,),
        in_specs=[
            pl.BlockSpec((seq, blk), lambda b: (tok_block0 + b, 0)),
            pl.BlockSpec((None, past, KV_WIDTH), lambda b: (b, 0, 0)),
            pl.BlockSpec((None, past, KV_WIDTH), lambda b: (b, 0, 0)),
            const((1, ATTN_WIDTH)),
            const((1, KV_WIDTH)),
            const((seq, ATTN_WIDTH)),
            const((seq, ATTN_WIDTH)),
            const((seq, KV_WIDTH)),
            const((seq, KV_WIDTH)),
            pl.BlockSpec(memory_space=pltpu.SMEM),
        ],
        out_specs=pl.BlockSpec((seq, ATTN_WIDTH), lambda b: (b, 0)),
        out_shape=jax.ShapeDtypeStruct((batch * seq, ATTN_WIDTH), F32),
        scratch_shapes=[
            pltpu.VMEM((seq, ATTN_WIDTH), F32),
            pltpu.VMEM((N_KV_HEADS, rows, KV_WIDTH), BF16),
            pltpu.VMEM((N_Q_HEADS, rows, KV_WIDTH), BF16),
        ],
        compiler_params=_params("parallel"),
        name="latent_attention",
    )(proj, cache_k, cache_v, q_norm, k_norm, cos_q, sin_q, cos_k, sin_k, sink)


def _convfour_kernel(seq, bx_ref, bb_ref, bc_ref, dx_ref, cw_ref, ct_ref, st_ref, cc_ref, sc_ref, conv_ref, four_ref):
    u = bc_ref[...] * bx_ref[...]
    row = _iota((seq, 1), 0)
    prev = jnp.where(row == 0, 0.0, pltpu.roll(u, 1, 0))
    nxt = jnp.where(row == seq - 1, 0.0, pltpu.roll(u, seq - 1, 0))
    cw = cw_ref[...]
    conv_ref[...] = bb_ref[...] * (cw[0:1] * prev + cw[1:2] * u + cw[2:3] * nxt)
    x = dx_ref[...].astype(BF16)
    a = jnp.dot(ct_ref[...], x, preferred_element_type=F32)
    b = jnp.dot(st_ref[...], x, preferred_element_type=F32)
    y = (jnp.dot(a.astype(BF16), cc_ref[...], preferred_element_type=F32)
         - jnp.dot(b.astype(BF16), sc_ref[...], preferred_element_type=F32))
    four_ref[...] = y * (1.0 / math.sqrt(seq * HEAD_DIM))


def _dft_tables(n, groups=1):
    idx = np.arange(n, dtype=np.int64)
    ang = ((idx[:, None] * idx[None, :]) % n).astype(np.float64) * (2.0 * math.pi / n)
    eye = np.eye(groups)
    return jnp.asarray(np.kron(eye, np.cos(ang)), F32), jnp.asarray(np.kron(eye, np.sin(ang)), F32)


def _convfour(proj, tok_block0, conv_w, batch, seq):
    ct, st = _dft_tables(seq)
    cc, sc = _dft_tables(HEAD_DIM, FOURIER_GROUPS)
    col = lambda c: pl.BlockSpec((seq, COLBLK), lambda b: (tok_block0 + b, c))
    const = lambda shape: pl.BlockSpec(shape, lambda b: (0, 0))
    out = pl.BlockSpec((seq, COLBLK), lambda b: (b, 0))
    return pl.pallas_call(
        functools.partial(_convfour_kernel, seq),
        grid=(batch,),
        in_specs=[col(2), col(3), col(4), col(9), const((8, CONV_CH)), const((seq, seq)), const((seq, seq)),
                  const((FOURIER_WIDTH, FOURIER_WIDTH)), const((FOURIER_WIDTH, FOURIER_WIDTH))],
        out_specs=[out, out],
        out_shape=[jax.ShapeDtypeStruct((batch * seq, COLBLK), F32)] * 2,
        compiler_params=_params("parallel"),
        name="conv_fourier",
    )(proj, proj, proj, proj, conv_w, ct.astype(BF16), st.astype(BF16), cc.astype(BF16), sc.astype(BF16))


def _rwkv_kernel(latent, seq, *refs):
    if latent:
        (r_ref, k_ref, v_ref, lr_ref, lrw_ref, bias_ref, kk_ref, ka_ref, rk_ref, lnw_ref, lnb_ref, s0_ref,
         o_ref, logw_s, kd_s, b_s, z_s, g_s, y_s, st_s) = refs
    else:
        (r_ref, k_ref, v_ref, lr_ref, lrw_ref, bias_ref, kk_ref, ka_ref, rk_ref, lnw_ref, lnb_ref,
         o_ref, sfin_ref, logw_s, kd_s, b_s, z_s, g_s, y_s, st_s) = refs
    W = RWKV_WIDTH
    C = RWKV_CHUNK
    gsum = _group_sum_matrix(W)
    lane = _iota((1, W), 1)

    def prep(i, carry):
        rows = pl.ds(pl.multiple_of(i * RWKV_PREP_ROWS, RWKV_PREP_ROWS), RWKV_PREP_ROWS)
        lr = lr_ref[rows, :]
        f = jnp.where(lane < 2 * DECAY_RANK, jnp.tanh(lr),
                      jnp.where(lane < 2 * DECAY_RANK + 2 * ICLR_RANK, lr, _sigmoid(lr)))
        low = jnp.dot(f.astype(BF16), lrw_ref[...], preferred_element_type=F32) + bias_ref[...]
        logw_s[rows, :] = DECAY_SCALE * _sigmoid(low[:, :2 * W])
        a = _sigmoid(low[:, 2 * W:4 * W])
        g_s[rows, :] = low[:, 4 * W:]
        k = k_ref[rows, :]
        kk = k * kk_ref[...]
        kk = kk * jnp.minimum(lax.rsqrt(_mm_split(kk * kk, gsum)), 1e12)
        z_s[rows, :] = -kk
        k_scaled = k * ka_ref[...]
        k_rest = k - k_scaled
        for j in range(2):
            aj = a[:, j * W:(j + 1) * W]
            kd_s[rows, j * W:(j + 1) * W] = k_rest + k_scaled * aj
            b_s[rows, j * W:(j + 1) * W] = kk * aj
        return carry

    lax.fori_loop(0, seq // RWKV_PREP_ROWS, prep, 0)

    y_s[...] = jnp.zeros((seq, W), F32)
    st_s[...] = jnp.zeros((2, W, W), F32)
    if latent:
        for j in range(2):
            for h in range(RWKV_HEADS):
                blk = slice(h * HEAD_DIM, (h + 1) * HEAD_DIM)
                st_s[j, blk, blk] = s0_ref[j, h].T

    HC = RWKV_HEADS * C
    wide_t = _iota((C, HC), 0)
    wide_s = _iota((C, HC), 1) % C
    eye_wide = jnp.where(wide_t == wide_s, 1.0, 0.0).astype(F32)
    same_head = (_iota((HC, HC), 0) // C) == (_iota((HC, HC), 1) // C)
    bd = (_iota((W, W), 0) // HEAD_DIM) == (_iota((W, W), 1) // HEAD_DIM)
    eye_head = _iota((HEAD_DIM, W), 0) == _iota((HEAD_DIM, W), 1) % HEAD_DIM
    tc_i = _iota((C, C), 0)
    sc_i = _iota((C, C), 1)
    head_masks = [(lane >= h * HEAD_DIM) & (lane < (h + 1) * HEAD_DIM) for h in range(RWKV_HEADS)]
    n_chunks = seq // C

    def stack_masked(x):
        return jnp.concatenate([jnp.where(m, x, 0.0) for m in head_masks], axis=0).astype(BF16)

    def heads_transposed(x):
        xt = x.T
        return jnp.concatenate([xt[h * HEAD_DIM:(h + 1) * HEAD_DIM] for h in range(RWKV_HEADS)],
                               axis=1).astype(BF16)

    def block_diag(x, mask):
        return jnp.where(mask, jnp.concatenate([x] * RWKV_HEADS, axis=0), 0.0)

    def bdot(a, b):
        return jnp.dot(a, b, preferred_element_type=F32)

    def chunk(j, c):
        rows = pl.ds(pl.multiple_of(c * C, C), C)
        cols = slice(j * W, (j + 1) * W)
        rc, vc, zc = r_ref[rows, :], v_ref[rows, :], z_s[rows, :]
        lw, kd, bc = logw_s[rows, cols], kd_s[rows, cols], b_s[rows, cols]
        if j == 0:
            strict, incl, tri = wide_s < wide_t, wide_s <= wide_t, sc_i <= tc_i
        else:
            strict, incl, tri = wide_s > wide_t, wide_s >= wide_t, sc_i >= tc_i
        cl = _mm_split(jnp.where(tri, 1.0, 0.0).astype(BF16), lw, split_rhs=True)
        yield
        tot = cl[C - 1:C, :] if j == 0 else cl[0:1, :]
        w_in, w_ex, w_inv, w_rest = jnp.exp(cl), jnp.exp(cl - lw), jnp.exp(-cl), jnp.exp(tot - cl)
        rt = rc * w_in
        zs = stack_masked(zc * w_ex)
        vs = stack_masked(vc)
        aa = lax.dot_general(jnp.concatenate([zc * w_ex, rt], axis=0).astype(BF16),
                             jnp.concatenate([stack_masked(bc * w_inv), stack_masked(kd * w_inv)], axis=0),
                             (((1,), (1,)), ((), ())), preferred_element_type=F32)
        yield
        a_zb = jnp.where(strict, aa[:C, :HC], 0.0)
        a_zk = jnp.where(strict, aa[:C, HC:], 0.0).astype(BF16)
        a_rb = jnp.where(incl, aa[C:, :HC], 0.0).astype(BF16)
        a_rk = jnp.where(incl, aa[C:, HC:], 0.0).astype(BF16)
        tinv = eye_wide + a_zb
        apow = bdot(a_zb.astype(BF16), block_diag(a_zb, same_head).astype(BF16))
        av = bdot(a_zk, vs)
        rk = bdot(a_rk, vs)
        yield
        for _ in range(int(math.log2(C)) - 2):
            both = bdot(jnp.concatenate([apow, tinv], axis=0).astype(BF16), block_diag(apow, same_head).astype(BF16))
            yield
            apow = both[:C]
            tinv = tinv + both[C:]
        tinv = tinv + bdot(tinv.astype(BF16), block_diag(apow, same_head).astype(BF16))
        yield
        x = bdot(tinv.astype(BF16), jnp.concatenate([zs, stack_masked(av)], axis=1))
        yield
        xs = jnp.concatenate([stack_masked(x[:, :W]), stack_masked(x[:, W:])], axis=1)
        ry = bdot(a_rb, xs)
        bh_t = heads_transposed(bc * w_rest)
        mn = bdot(bh_t, xs)
        nk = bdot(heads_transposed(kd * w_rest), vs)
        yield
        rz = rt + ry[:, :W]
        y0 = ry[:, W:] + rk
        mt_wide = mn[:, :W] + jnp.where(eye_head, jnp.exp(tot), 0.0)
        return rows, rz.astype(BF16), y0, mt_wide.astype(BF16), mn[:, W:] + nk

    def advance(j, local):
        rows, rz, y0, mt_wide, nt_wide = local
        s_prev = st_s[j].astype(BF16)
        y_s[rows, :] += bdot(rz, s_prev) + y0
        st_s[j] = block_diag(bdot(mt_wide, s_prev) + nt_wide, bd)

    def step(i, carry):
        n = RWKV_CHUNKS_PER_STEP
        local = _run_interleaved([chunk(0, n * i + u) for u in range(n)]
                                 + [chunk(1, n_chunks - 1 - (n * i + u)) for u in range(n)])
        for u in range(n):
            advance(0, local[u])
            advance(1, local[n + u])
        return carry

    lax.fori_loop(0, n_chunks // RWKV_CHUNKS_PER_STEP, step, 0)

    if not latent:
        for j in range(2):
            for h in range(RWKV_HEADS):
                blk = slice(h * HEAD_DIM, (h + 1) * HEAD_DIM)
                sfin_ref[j, h] = st_s[j, blk, blk].T

    def finish(i, carry):
        rows = pl.ds(pl.multiple_of(i * RWKV_PREP_ROWS, RWKV_PREP_ROWS), RWKV_PREP_ROWS)
        y = y_s[rows, :]
        mu = _mm_split(y, gsum) * (1.0 / HEAD_DIM)
        d = y - mu
        var = _mm_split(d * d, gsum) * (1.0 / HEAD_DIM)
        yn = d * lax.rsqrt(var + GN_EPS) * lnw_ref[...] + lnb_ref[...]
        r, k, v = r_ref[rows, :], k_ref[rows, :], v_ref[rows, :]
        bonus = _mm_split(r * k * rk_ref[...], gsum) * v
        o_ref[rows, :] = (yn + bonus) * g_s[rows, :]
        return carry

    lax.fori_loop(0, seq // RWKV_PREP_ROWS, finish, 0)


def _rwkv(proj, tok_block0, p, batch, seq, s0=None):
    latent = s0 is not None
    W = RWKV_WIDTH
    col = lambda c: pl.BlockSpec((seq, COLBLK), lambda b: (tok_block0 + b, c))
    const = lambda shape: pl.BlockSpec(shape, lambda b: (0,) * len(shape))
    in_specs = [col(5), col(6), col(7), col(8), const((COLBLK, 5 * W)), const((1, 5 * W))] + [const((1, W))] * 5
    args = [proj, proj, proj, proj, p["lrw"], p["lr_bias"], p["k_k"], p["k_a"], p["r_k"], p["ln_w"], p["ln_b"]]
    out_specs = [pl.BlockSpec((seq, W), lambda b: (b, 0))]
    out_shape = [jax.ShapeDtypeStruct((batch * seq, W), F32)]
    if latent:
        in_specs.append(pl.BlockSpec((None, 2, RWKV_HEADS, HEAD_DIM, HEAD_DIM), lambda b: (b, 0, 0, 0, 0)))
        args.append(s0)
    else:
        out_specs.append(pl.BlockSpec((None, 2, RWKV_HEADS, HEAD_DIM, HEAD_DIM), lambda b: (b, 0, 0, 0, 0)))
        out_shape.append(jax.ShapeDtypeStruct((batch, 2, RWKV_HEADS, HEAD_DIM, HEAD_DIM), F32))
    res = pl.pallas_call(
        functools.partial(_rwkv_kernel, latent, seq),
        grid=(batch,),
        in_specs=in_specs,
        out_specs=out_specs,
        out_shape=out_shape,
        scratch_shapes=[
            pltpu.VMEM((seq, 2 * W), F32),
            pltpu.VMEM((seq, 2 * W), F32),
            pltpu.VMEM((seq, 2 * W), F32),
            pltpu.VMEM((seq, W), F32),
            pltpu.VMEM((seq, W), F32),
            pltpu.VMEM((seq, W), F32),
            pltpu.VMEM((2, W, W), F32),
        ],
        compiler_params=_params("parallel"),
        name="rwkv_latent" if latent else "rwkv_ctx",
    )(*args)
    return res if not latent else res[0]


def _post_prologue(rows, x_ref, mix_refs, wout_ref, n2_ref, g1_ref, sh_ref, sc_ref):
    mix = None
    for i, m_ref in enumerate(mix_refs):
        part = jnp.dot(m_ref[rows, :].astype(BF16), wout_ref[i * COLBLK:(i + 1) * COLBLK, :],
                       preferred_element_type=F32)
        mix = part if mix is None else mix + part
    yield
    xn = x_ref[rows, :] + g1_ref[...] * mix
    return xn, _rms(xn, n2_ref[...]) * (1.0 + sc_ref[...]) + sh_ref[...]


def _swiglu(hb, w1_ref, w3_ref, w2_ref):
    t = _silu(jnp.dot(hb, w1_ref[...], preferred_element_type=F32)) * jnp.dot(hb, w3_ref[...],
                                                                             preferred_element_type=F32)
    return jnp.dot(t.astype(BF16), w2_ref[...], preferred_element_type=F32)


def _post_dense_kernel(x_ref, a_ref, c_ref, rw_ref, f_ref, wout_ref, n2_ref, g1_ref, sh_ref, sc_ref, g2_ref,
                       w1_ref, w3_ref, w2_ref, o_ref, h_s):
    f = pl.program_id(1)

    @pl.when(f == 0)
    def _():
        half = x_ref.shape[0] // 2

        def prologue(r):
            rows = slice(r * half, (r + 1) * half)
            xn, h = yield from _post_prologue(rows, x_ref, (a_ref, c_ref, rw_ref, f_ref), wout_ref, n2_ref, g1_ref,
                                              sh_ref, sc_ref)
            o_ref[rows, :] = xn
            h_s[rows, :] = h.astype(BF16)

        _run_interleaved([prologue(0), prologue(1)])

    o_ref[...] += g2_ref[...] * _swiglu(h_s[...], w1_ref, w3_ref, w2_ref)


def _post_moe_kernel(x_ref, a_ref, c_ref, rw_ref, f_ref, wout_ref, n2_ref, g1_ref, sh_ref, sc_ref, g2_ref,
                     router_ref, order_ref, w1_ref, w3_ref, w2_ref, o_ref, h_s, gate_s, rank_s, cnt_s):
    e = pl.program_id(1)
    lane = _iota((1, LANES), 1)
    sub = MOE_SUB_TILE
    n_sub = x_ref.shape[0] // sub

    @pl.when(e == 0)
    def _():
        def prologue(u):
            rows = slice(u * sub, (u + 1) * sub)
            xn, h = yield from _post_prologue(rows, x_ref, (a_ref, c_ref, rw_ref, f_ref), wout_ref, n2_ref, g1_ref,
                                              sh_ref, sc_ref)
            o_ref[rows, :] = xn
            h_s[rows, :] = h.astype(BF16)
            logits = _mm_split2(h, router_ref[...])
            yield
            logits = logits.T[:N_EXPERTS]
            ex = jnp.exp(logits - jnp.max(logits, axis=0, keepdims=True))
            probs = ex / jnp.sum(ex, axis=0, keepdims=True)
            expert = _iota(probs.shape, 0)
            p1 = jnp.max(probs, axis=0, keepdims=True)
            i1 = jnp.min(jnp.where(probs == p1, expert, N_EXPERTS), axis=0, keepdims=True)
            rest = jnp.where(expert == i1, -1.0, probs)
            p2 = jnp.max(rest, axis=0, keepdims=True)
            i2 = jnp.min(jnp.where(rest == p2, expert, N_EXPERTS), axis=0, keepdims=True)
            top = p1 + p2
            gates_t = jnp.where(expert == i1, p1 / top, jnp.where(expert == i2, p2 / top, 0.0))
            gate_s[rows, :] = jnp.concatenate([gates_t, jnp.zeros((LANES - N_EXPERTS, sub), F32)], axis=0).T
            sel_t = jnp.where((expert == i1) | (expert == i2), 1.0, 0.0)
            rank = jnp.dot(sel_t.astype(BF16), order_ref[...], preferred_element_type=F32)
            rank_s[u] = jnp.where(sel_t > 0.0, rank, -1.0)
            for ex_i in range(N_EXPERTS):
                cnt_s[u * N_EXPERTS + ex_i] = jnp.sum(sel_t[ex_i:ex_i + 1, :]).astype(jnp.int32)

        _run_interleaved([prologue(u) for u in range(n_sub)])

    routed = cnt_s[e]
    for u in range(1, n_sub):
        routed = jnp.maximum(routed, cnt_s[u * N_EXPERTS + e])

    def expert_rows(u, capacity):
        rows = slice(u * sub, (u + 1) * sub)
        if capacity is None:
            hb = h_s[rows, :]
        else:
            slot = _iota((capacity, sub), 0).astype(F32)
            pick = jnp.where(rank_s[u, pl.ds(e, 1), :] == slot, 1.0, 0.0)
            hb = jnp.dot(pick.astype(BF16), h_s[rows, :], preferred_element_type=F32).astype(BF16)
            yield
        t1 = jnp.dot(hb, w1_ref[...], preferred_element_type=F32)
        t3 = jnp.dot(hb, w3_ref[...], preferred_element_type=F32)
        yield
        y = jnp.dot((_silu(t1) * t3).astype(BF16), w2_ref[...], preferred_element_type=F32)
        yield
        if capacity is not None:
            y = _mm(pick.T, y)
            yield
        gate = jnp.sum(jnp.where(lane == e, gate_s[rows, :], 0.0), axis=-1, keepdims=True)
        o_ref[rows, :] += gate * (g2_ref[...] * y)

    def run(capacity):
        _run_interleaved([expert_rows(u, capacity) for u in range(n_sub)])

    below = 0
    for capacity in MOE_CAPACITIES:
        pl.when((routed > below) & (routed <= capacity))(functools.partial(run, capacity))
        below = capacity
    pl.when(routed > below)(functools.partial(run, None))


def _post(x, mixes, w_out, norm_w, mod, row_of_tile, ffn, moe):
    n_tok = x.shape[0]
    tok = MOE_TOKEN_TILE if moe else TOKEN_TILE
    tile = lambda width: pl.BlockSpec((tok, width), lambda i, f: (i, 0))
    const = lambda shape: pl.BlockSpec(shape, lambda i, f: (0, 0))
    mod_spec = lambda chunk: _mod_spec(chunk, functools.partial(row_of_tile, tok))
    in_specs = [tile(D_MODEL)] + [tile(COLBLK)] * 4 + [const((MIX_WIDTH, D_MODEL)), const((1, D_MODEL)),
                                                       mod_spec(2), mod_spec(3), mod_spec(4), mod_spec(5)]
    args = [x, *mixes, w_out, norm_w, mod, mod, mod, mod]
    scratch = [pltpu.VMEM((tok, D_MODEL), BF16)]
    if moe:
        router, w1, w3, w2 = ffn
        sub = MOE_SUB_TILE
        order = jnp.where(_iota((sub, sub), 0) < _iota((sub, sub), 1), 1.0, 0.0).astype(BF16)
        in_specs += [const((D_MODEL, LANES)), const((sub, sub)),
                     pl.BlockSpec((None, D_MODEL, D_FF_EXPERT), lambda i, e: (e, 0, 0)),
                     pl.BlockSpec((None, D_MODEL, D_FF_EXPERT), lambda i, e: (e, 0, 0)),
                     pl.BlockSpec((None, D_FF_EXPERT, D_MODEL), lambda i, e: (e, 0, 0))]
        args += [router, order, w1, w3, w2]
        scratch += [pltpu.VMEM((tok, LANES), F32), pltpu.VMEM((tok // sub, N_EXPERTS, sub), F32),
                    pltpu.SMEM((tok // sub * N_EXPERTS,), jnp.int32)]
        body, steps, name = _post_moe_kernel, N_EXPERTS, "post_moe"
    else:
        w1, w3, w2 = ffn
        in_specs += [pl.BlockSpec((D_MODEL, FF_TILE), lambda i, f: (0, f)),
                     pl.BlockSpec((D_MODEL, FF_TILE), lambda i, f: (0, f)),
                     pl.BlockSpec((FF_TILE, D_MODEL), lambda i, f: (f, 0))]
        args += [w1, w3, w2]
        body, steps, name = _post_dense_kernel, D_FF // FF_TILE, "post_dense"
    return pl.pallas_call(
        body,
        grid=(n_tok // tok, steps),
        in_specs=in_specs,
        out_specs=tile(D_MODEL),
        out_shape=jax.ShapeDtypeStruct((n_tok, D_MODEL), F32),
        scratch_shapes=scratch,
        compiler_params=_params("parallel", "arbitrary"),
        name=name,
    )(*args)


def _rope_tables(n_tokens, n_heads):
    n_rows = n_tokens // GRID_W
    row = np.repeat(np.arange(n_rows), GRID_W).astype(np.float32)
    col = np.tile(np.arange(GRID_W), n_rows).astype(np.float32)
    quarter = HEAD_DIM // 4
    inv_freq = np.float32(ROPE_BASE) ** (-np.arange(quarter, dtype=np.float32) / np.float32(quarter))
    ang_r, ang_c = row[:, None] * inv_freq, col[:, None] * inv_freq
    cos = np.concatenate([np.cos(ang_r)] * 2 + [np.cos(ang_c)] * 2, axis=1)
    sin = np.concatenate([-np.sin(ang_r), np.sin(ang_r), -np.sin(ang_c), np.sin(ang_c)], axis=1)
    return jnp.asarray(np.tile(cos, (1, n_heads)), F32), jnp.asarray(np.tile(sin, (1, n_heads)), F32)


def _layout_w_in(w):
    n_main = w.shape[1] - FOURIER_WIDTH
    return w[:, :n_main].astype(BF16), w[:, n_main:].astype(BF16)


def _layout_rwkv(l, w0, w_up, a0, a_up, g_up, k_k, k_a, r_k, ln_w, ln_b):
    W = RWKV_WIDTH
    lrw = jnp.zeros((COLBLK, 5 * W), F32)
    for j in range(2):
        lrw = lrw.at[j * DECAY_RANK:(j + 1) * DECAY_RANK, j * W:(j + 1) * W].set(w_up[l, j])
        r0 = 2 * DECAY_RANK + j * ICLR_RANK
        lrw = lrw.at[r0:r0 + ICLR_RANK, (2 + j) * W:(3 + j) * W].set(a_up[l, j])
    r0 = 2 * DECAY_RANK + 2 * ICLR_RANK
    lrw = lrw.at[r0:r0 + GATE_RANK, 4 * W:].set(g_up[l])
    bias = jnp.concatenate([w0[l, 0], w0[l, 1], a0[l, 0], a0[l, 1], jnp.zeros((W,), F32)])[None, :]
    row = lambda t: t[l].reshape(1, W)
    return {"lrw": lrw.astype(BF16), "lr_bias": bias, "k_k": row(k_k), "k_a": row(k_a), "r_k": row(r_k),
            "ln_w": row(ln_w), "ln_b": row(ln_b)}


def kernel(x_prompt, x_sample, cache_k, cache_v, state_wkv, c, c_ctx, mod_w, mod_b, norm1_w, norm2_w, w_in, q_norm_w, k_norm_w, attn_sink, conv_w, rwkv_w0, rwkv_w_up, rwkv_a0, rwkv_a_up, rwkv_g_up, rwkv_k_k, rwkv_k_a, rwkv_r_k, rwkv_ln_w, rwkv_ln_b, w_out, ffn_w1, ffn_w3, ffn_w2, router_w, moe_w1, moe_w3, moe_w2):
    batch, seq, _ = x_prompt.shape
    dec_batch, dec_seq, _ = x_sample.shape
    past = cache_k.shape[2]
    assert seq % BLOCK == 0 and dec_seq % MOE_TOKEN_TILE == 0 and (batch * seq) % MOE_TOKEN_TILE == 0
    assert 1 + dec_batch <= MOD_ROWS

    cond = jnp.concatenate([c_ctx[None, :], c, jnp.zeros((MOD_ROWS - 1 - dec_batch, D_MODEL), F32)], axis=0)
    mod_all = _modulation(cond, mod_w, mod_b)
    prompt_row = lambda tile, i: 0
    sample_row = lambda tile, i: 1 + (i * tile) // dec_seq

    rope_q = _rope_tables(dec_seq, N_Q_HEADS)
    rope_k = _rope_tables(dec_seq, N_KV_HEADS)

    xp = x_prompt.reshape(batch * seq, D_MODEL)
    xs = x_sample.reshape(dec_batch * dec_seq, D_MODEL)
    new_k, new_v, new_s = [], [], []
    for l in range(DEPTH):
        mod = mod_all[l].reshape(MOD_ROWS, 6, 1, D_MODEL)
        w_in_l = _layout_w_in(w_in[l])
        n1 = norm1_w[l][None, :]
        n2 = norm2_w[l][None, :]
        qn = jnp.tile(q_norm_w[l], N_Q_HEADS)[None, :]
        kn = jnp.tile(k_norm_w[l], N_KV_HEADS)[None, :]
        cw = jnp.concatenate([conv_w[l], jnp.zeros((5, CONV_CH), F32)], axis=0)
        rp = _layout_rwkv(l, rwkv_w0, rwkv_w_up, rwkv_a0, rwkv_a_up, rwkv_g_up, rwkv_k_k, rwkv_k_a, rwkv_r_k,
                          rwkv_ln_w, rwkv_ln_b)
        w_out_l = w_out[l].astype(BF16)
        if l % 2 == 0:
            j = l // 2
            ffn = (ffn_w1[j].astype(BF16), ffn_w3[j].astype(BF16), ffn_w2[j].astype(BF16))
        else:
            j = l // 2
            router = jnp.concatenate([router_w[j], jnp.zeros((D_MODEL, LANES - N_EXPERTS), F32)], axis=1)
            ffn = (router, moe_w1[j].astype(BF16), moe_w3[j].astype(BF16), moe_w2[j].astype(BF16))

        proj_p = _in_proj(xp, n1, mod, prompt_row, w_in_l)
        attn_p, k_p, v_p = _ctx_attention(proj_p, qn, kn, attn_sink[l], batch, seq)
        conv_p, four_p = _convfour(proj_p, 0, cw, batch, seq)
        rw_p, sfin = _rwkv(proj_p, 0, rp, batch, seq)
        xp = _post(xp, (attn_p, conv_p, rw_p, four_p), w_out_l, n2, mod, prompt_row, ffn, l % 2 == 1)
        new_k.append(k_p.reshape(batch, seq, N_KV_HEADS, HEAD_DIM))
        new_v.append(v_p.reshape(batch, seq, N_KV_HEADS, HEAD_DIM))
        new_s.append(sfin)

        proj_s = _in_proj(xs, n1, mod, sample_row, w_in_l)
        kc = cache_k[:, l].reshape(dec_batch, past, KV_WIDTH)
        vc = cache_v[:, l].reshape(dec_batch, past, KV_WIDTH)
        attn_s = _lat_attention(proj_s, 0, kc, vc, qn, kn, rope_q, rope_k, attn_sink[l], dec_batch, dec_seq, past)
        conv_s, four_s = _convfour(proj_s, 0, cw, dec_batch, dec_seq)
        rw_s = _rwkv(proj_s, 0, rp, dec_batch, dec_seq, s0=state_wkv[:, l])
        xs = _post(xs, (attn_s, conv_s, rw_s, four_s), w_out_l, n2, mod, sample_row, ffn, l % 2 == 1)

    return (xp.reshape(batch, seq, D_MODEL), xs.reshape(dec_batch, dec_seq, D_MODEL),
            jnp.stack(new_k, axis=1), jnp.stack(new_v, axis=1), jnp.stack(new_s, axis=1))
```

```python
import functools
import math

import jax
import jax.numpy as jnp
import numpy as np
from jax import lax
from jax.experimental import pallas as pl
from jax.experimental.pallas import tpu as pltpu

F32 = jnp.float32
BF16 = jnp.bfloat16

D_MODEL = 1024
DEPTH = 2
GRID_W = 64
HEAD_DIM = 64
N_Q_HEADS = 4
N_KV_HEADS = 2
WINDOW = 128
BLOCK = 128
ROPE_BASE = 10000.0
ATTN_WIDTH = N_Q_HEADS * HEAD_DIM
KV_WIDTH = N_KV_HEADS * HEAD_DIM
CONV_CH = 256
RWKV_HEADS = 4
RWKV_WIDTH = RWKV_HEADS * HEAD_DIM
DECAY_RANK = 32
ICLR_RANK = 32
GATE_RANK = 64
FOURIER_GROUPS = 4
FOURIER_WIDTH = FOURIER_GROUPS * HEAD_DIM
MIX_WIDTH = ATTN_WIDTH + CONV_CH + RWKV_WIDTH + FOURIER_WIDTH
D_FF = 2816
N_EXPERTS = 8
D_FF_EXPERT = 1024
NORM_EPS = 1e-6
GN_EPS = 64e-5
NEG_INF = -1e30

LANES = 128
VMEM_LIMIT_BYTES = 56 * 1024 * 1024

PROJ_WIDTH = 2560
COLBLK = 256
LOWRANK_WIDTH = 2 * DECAY_RANK + 2 * ICLR_RANK + GATE_RANK
MOD_ROWS = 16
TOKEN_TILE = 512
RWKV_CHUNK = 64
RWKV_CHUNKS_PER_STEP = 4
RWKV_PREP_ROWS = 256
FF_TILE = 1408
MOE_TOKEN_TILE = 1024
MOE_SUB_TILE = 512
MOE_CAPACITIES = (128, 144, 160, 176, 192, 208, 224, 256)
DECAY_SCALE = -math.exp(-0.5)


def _params(*sem):
    return pltpu.CompilerParams(dimension_semantics=sem, vmem_limit_bytes=VMEM_LIMIT_BYTES)


def _mm(a, b):
    return jnp.dot(a.astype(BF16), b.astype(BF16), preferred_element_type=F32)


def _mm_nt(a, b):
    return lax.dot_general(a.astype(BF16), b.astype(BF16), (((1,), (1,)), ((), ())), preferred_element_type=F32)


def _mm_split(a, b, split_rhs=False, parts=2):
    exact, x = (a, b) if split_rhs else (b, a)
    acc = None
    for _ in range(parts):
        piece = x.astype(BF16)
        x = x - piece.astype(F32)
        term = (jnp.dot(exact, piece, preferred_element_type=F32) if split_rhs
                else jnp.dot(piece, exact, preferred_element_type=F32))
        acc = term if acc is None else acc + term
    return acc


def _sigmoid(x):
    return 0.5 * jnp.tanh(0.5 * x) + 0.5


def _mm_split2(a, b):
    a_hi, b_hi = a.astype(BF16), b.astype(BF16)
    a_lo = (a - a_hi.astype(F32)).astype(BF16)
    b_lo = (b - b_hi.astype(F32)).astype(BF16)
    dot = functools.partial(jnp.dot, preferred_element_type=F32)
    return dot(a_hi, b_hi) + dot(a_lo, b_hi) + dot(a_hi, b_lo)


def _silu(x):
    return x * _sigmoid(x)


def _iota(shape, dim):
    return lax.broadcasted_iota(jnp.int32, shape, dim)


def _run_interleaved(gens):
    results = [None] * len(gens)
    pending = list(range(len(gens)))
    while pending:
        for idx in list(pending):
            try:
                next(gens[idx])
            except StopIteration as done:
                results[idx] = done.value
                pending.remove(idx)
    return results


def _mod_kernel(c_ref, w_ref, b_ref, o_ref):
    o_ref[0] = _mm(_silu(c_ref[...]), w_ref[0]) + b_ref[0]


def _modulation(cond, mod_w, mod_b):
    n_chunks = mod_w.shape[-1] // D_MODEL
    return pl.pallas_call(
        _mod_kernel,
        grid=(DEPTH, n_chunks),
        in_specs=[
            pl.BlockSpec((MOD_ROWS, D_MODEL), lambda l, j: (0, 0)),
            pl.BlockSpec((1, D_MODEL, D_MODEL), lambda l, j: (l, 0, j)),
            pl.BlockSpec((1, 1, D_MODEL), lambda l, j: (l, 0, j)),
        ],
        out_specs=pl.BlockSpec((1, MOD_ROWS, D_MODEL), lambda l, j: (l, 0, j)),
        out_shape=jax.ShapeDtypeStruct((DEPTH, MOD_ROWS, n_chunks * D_MODEL), F32),
        compiler_params=_params("parallel", "parallel"),
        name="modulation",
    )(cond, mod_w, mod_b.reshape(DEPTH, 1, -1))


def _mod_spec(chunk, row_of_tile):
    return pl.BlockSpec((None, None, 1, D_MODEL), lambda i, *_: (row_of_tile(i), chunk, 0, 0))


def _rms(x, w):
    return x * lax.rsqrt(jnp.mean(x * x, axis=-1, keepdims=True) + NORM_EPS) * w


def _in_proj_kernel(x_ref, nw_ref, sh_ref, sc_ref, w_main_ref, w_four_ref, o_ref):
    h = (_rms(x_ref[...], nw_ref[...]) * (1.0 + sc_ref[...]) + sh_ref[...]).astype(BF16)
    n_main = w_main_ref.shape[1]
    gap_end = PROJ_WIDTH - FOURIER_WIDTH
    o_ref[:, :n_main] = jnp.dot(h, w_main_ref[...], preferred_element_type=F32)
    o_ref[:, n_main:gap_end] = jnp.zeros((h.shape[0], gap_end - n_main), F32)
    o_ref[:, gap_end:] = jnp.dot(h, w_four_ref[...], preferred_element_type=F32)


def _in_proj(x, norm_w, mod, row_of_tile, w_in):
    row_of_tile = functools.partial(row_of_tile, TOKEN_TILE)
    n_tok = x.shape[0]
    return pl.pallas_call(
        _in_proj_kernel,
        grid=(n_tok // TOKEN_TILE,),
        in_specs=[
            pl.BlockSpec((TOKEN_TILE, D_MODEL), lambda i: (i, 0)),
            pl.BlockSpec((1, D_MODEL), lambda i: (0, 0)),
            _mod_spec(0, row_of_tile),
            _mod_spec(1, row_of_tile),
            pl.BlockSpec(w_in[0].shape, lambda i: (0, 0)),
            pl.BlockSpec(w_in[1].shape, lambda i: (0, 0)),
        ],
        out_specs=pl.BlockSpec((TOKEN_TILE, PROJ_WIDTH), lambda i: (i, 0)),
        out_shape=jax.ShapeDtypeStruct((n_tok, PROJ_WIDTH), F32),
        compiler_params=_params("parallel"),
        name="in_proj",
    )(x, norm_w, mod, mod, *w_in)


def _group_sum_matrix(width):
    r = _iota((width, width), 0) // HEAD_DIM
    c = _iota((width, width), 1) // HEAD_DIM
    return jnp.where(r == c, 1.0, 0.0).astype(BF16)


def _head_rms(x, w):
    ms = _mm_split(x * x, _group_sum_matrix(x.shape[-1])) * (1.0 / HEAD_DIM)
    return x * lax.rsqrt(ms + NORM_EPS) * w


def _dup_kv_head(x, g):
    lane = _iota((1, KV_WIDTH), 1)
    rolled = pltpu.roll(x, HEAD_DIM, 1)
    first = lane < HEAD_DIM
    return jnp.where(first, x, rolled) if g == 0 else jnp.where(first, rolled, x)


def _half_mask(j):
    lane = _iota((1, KV_WIDTH), 1)
    return (lane >= j * HEAD_DIM) & (lane < (j + 1) * HEAD_DIM)


def _ctx_attn_parts(qkv_ref, qw_ref, kw_ref, sink_ref, o_ref, k_ref, v_ref):
    qkv = qkv_ref[...]
    q = _head_rms(qkv[:, :ATTN_WIDTH], qw_ref[...]) * HEAD_DIM ** -0.5
    k = _head_rms(qkv[:, ATTN_WIDTH:ATTN_WIDTH + KV_WIDTH], kw_ref[...])
    v = qkv[:, ATTN_WIDTH + KV_WIDTH:]
    k_ref[...] = k
    v_ref[...] = v

    def head(g, j):
        mj = _half_mask(j)
        s = sink_ref[2 * g + j]
        logits = _mm_nt(jnp.where(mj, q[:, g * KV_WIDTH:(g + 1) * KV_WIDTH], 0.0), _dup_kv_head(k, g))
        yield
        m = jnp.maximum(jnp.max(logits, axis=-1, keepdims=True), s)
        e = jnp.exp(logits - m)
        den = jnp.sum(e, axis=-1, keepdims=True) + jnp.exp(s - m)
        pv = _mm(e, jnp.where(mj, _dup_kv_head(v, g), 0.0))
        yield
        return pv / den

    def finish(out):
        for g in range(N_KV_HEADS):
            o_ref[:, g * KV_WIDTH:(g + 1) * KV_WIDTH] = out[2 * g] + out[2 * g + 1]

    return [head(g, j) for g in range(N_KV_HEADS) for j in range(2)], finish


def _ctx_attention_io(proj, q_norm, k_norm, sink, batch, seq):
    blk = ATTN_WIDTH + 2 * KV_WIDTH
    in_specs = [
        pl.BlockSpec((seq, blk), lambda b: (b, 0)),
        pl.BlockSpec((1, ATTN_WIDTH), lambda b: (0, 0)),
        pl.BlockSpec((1, KV_WIDTH), lambda b: (0, 0)),
        pl.BlockSpec(memory_space=pltpu.SMEM),
    ]
    out_specs = [
        pl.BlockSpec((seq, ATTN_WIDTH), lambda b: (b, 0)),
        pl.BlockSpec((seq, KV_WIDTH), lambda b: (b, 0)),
        pl.BlockSpec((seq, KV_WIDTH), lambda b: (b, 0)),
    ]
    out_shape = [
        jax.ShapeDtypeStruct((batch * seq, ATTN_WIDTH), F32),
        jax.ShapeDtypeStruct((batch * seq, KV_WIDTH), F32),
        jax.ShapeDtypeStruct((batch * seq, KV_WIDTH), F32),
    ]
    return in_specs, [proj, q_norm, k_norm, sink], out_specs, out_shape


def _rope(x, cos, sin_signed):
    width = x.shape[-1]
    lane = _iota((1, width), 1)
    first_half = (lane % (HEAD_DIM // 2)) < (HEAD_DIM // 4)
    partner = jnp.where(first_half, pltpu.roll(x, width - HEAD_DIM // 4, 1), pltpu.roll(x, HEAD_DIM // 4, 1))
    return x * cos + partner * sin_signed


def _lat_attn_kernel(seq, past, qkv_ref, kc_ref, vc_ref, qw_ref, kw_ref, cosq_ref, snq_ref, cosk_ref, snk_ref,
                     sink_ref, o_ref, q_s, k_s, v_s):
    qkv = qkv_ref[...]
    q = _rope(_head_rms(qkv[:, :ATTN_WIDTH], qw_ref[...]), cosq_ref[...], snq_ref[...]) * HEAD_DIM ** -0.5
    k = _rope(_head_rms(qkv[:, ATTN_WIDTH:ATTN_WIDTH + KV_WIDTH], kw_ref[...]), cosk_ref[...], snk_ref[...])
    v = qkv[:, ATTN_WIDTH + KV_WIDTH:]
    kc = kc_ref[...]
    vc = vc_ref[...]
    q_s[...] = q
    zero_blk = jnp.zeros((BLOCK, KV_WIDTH), BF16)
    for g in range(N_KV_HEADS):
        k_s[g, 0:BLOCK] = zero_blk
        k_s[g, BLOCK:BLOCK + seq] = _dup_kv_head(k, g).astype(BF16)
        k_s[g, BLOCK + seq:2 * BLOCK + seq] = zero_blk
        k_s[g, 2 * BLOCK + seq:] = _dup_kv_head(kc, g).astype(BF16)
        vg = _dup_kv_head(v, g)
        vcg = _dup_kv_head(vc, g)
        for j in range(2):
            mj = _half_mask(j)
            v_s[2 * g + j, 0:BLOCK] = zero_blk
            v_s[2 * g + j, BLOCK:BLOCK + seq] = jnp.where(mj, vg, 0.0).astype(BF16)
            v_s[2 * g + j, BLOCK + seq:2 * BLOCK + seq] = zero_blk
            v_s[2 * g + j, 2 * BLOCK + seq:] = jnp.where(mj, vcg, 0.0).astype(BF16)

    win = 3 * BLOCK
    ctx0 = 2 * BLOCK + seq

    def q_block(n, carry):
        q0 = pl.multiple_of(n * BLOCK, BLOCK)
        qpos = n * BLOCK + _iota((BLOCK, win), 0)
        kpos = (n - 1) * BLOCK + _iota((BLOCK, win), 1)
        valid = (jnp.abs(qpos - kpos) <= WINDOW) & (kpos >= 0) & (kpos < seq)

        def head(g, j):
            h = 2 * g + j
            s = sink_ref[h]
            qm = jnp.where(_half_mask(j), q_s[pl.ds(q0, BLOCK), g * KV_WIDTH:(g + 1) * KV_WIDTH], 0.0)
            lw = jnp.where(valid, _mm_nt(qm, k_s[g, pl.ds(q0, win), :]), NEG_INF)
            lc = _mm_nt(qm, k_s[g, ctx0:ctx0 + past, :])
            yield
            m = jnp.maximum(jnp.maximum(jnp.max(lw, axis=-1, keepdims=True),
                                        jnp.max(lc, axis=-1, keepdims=True)), s)
            ew = jnp.exp(lw - m)
            ec = jnp.exp(lc - m)
            den = jnp.sum(ew, axis=-1, keepdims=True) + jnp.sum(ec, axis=-1, keepdims=True) + jnp.exp(s - m)
            pv = (jnp.dot(ew.astype(BF16), v_s[h, pl.ds(q0, win), :], preferred_element_type=F32)
                  + jnp.dot(ec.astype(BF16), v_s[h, ctx0:ctx0 + past, :], preferred_element_type=F32))
            yield
            return pv / den

        out = _run_interleaved([head(g, j) for g in range(N_KV_HEADS) for j in range(2)])
        for g in range(N_KV_HEADS):
            o_ref[pl.ds(q0, BLOCK), g * KV_WIDTH:(g + 1) * KV_WIDTH] = out[2 * g] + out[2 * g + 1]
        return carry

    lax.fori_loop(0, seq // BLOCK, q_block, 0)


def _lat_attention(proj, tok_block0, cache_k, cache_v, q_norm, k_norm, rope_q, rope_k, sink, batch, seq, past):
    blk = ATTN_WIDTH + 2 * KV_WIDTH
    cos_q, sin_q = rope_q
    cos_k, sin_k = rope_k
    rows = 2 * BLOCK + seq + past
    const = lambda shape: pl.BlockSpec(shape, lambda b: (0, 0))
    return pl.pallas_call(
        functools.partial(_lat_attn_kernel, seq, past),
        grid=(batch,),
        in_specs=[
            pl.BlockSpec((seq, blk), lambda b: (tok_block0 + b, 0)),
            pl.BlockSpec((None, past, KV_WIDTH), lambda b: (b, 0, 0)),
            pl.BlockSpec((None, past, KV_WIDTH), lambda b: (b, 0, 0)),
            const((1, ATTN_WIDTH)),
            const((1, KV_WIDTH)),
            const((seq, ATTN_WIDTH)),
            const((seq, ATTN_WIDTH)),
            const((seq, KV_WIDTH)),
            const((seq, KV_WIDTH)),
            pl.BlockSpec(memory_space=pltpu.SMEM),
        ],
        out_specs=pl.BlockSpec((seq, ATTN_WIDTH), lambda b: (b, 0)),
        out_shape=jax.ShapeDtypeStruct((batch * seq, ATTN_WIDTH), F32),
        scratch_shapes=[
            pltpu.VMEM((seq, ATTN_WIDTH), F32),
            pltpu.VMEM((N_KV_HEADS, rows, KV_WIDTH), BF16),
            pltpu.VMEM((N_Q_HEADS, rows, KV_WIDTH), BF16),
        ],
        compiler_params=_params("parallel"),
        name="latent_attention",
    )(proj, cache_k, cache_v, q_norm, k_norm, cos_q, sin_q, cos_k, sin_k, sink)


def _convfour_gen(seq, bx_ref, bb_ref, bc_ref, dx_ref, cw_ref, ct_ref, st_ref, cc_ref, sc_ref, conv_ref, four_ref):
    u = bc_ref[...] * bx_ref[...]
    row = _iota((seq, 1), 0)
    prev = jnp.where(row == 0, 0.0, pltpu.roll(u, 1, 0))
    nxt = jnp.where(row == seq - 1, 0.0, pltpu.roll(u, seq - 1, 0))
    cw = cw_ref[...]
    conv_ref[...] = bb_ref[...] * (cw[0:1] * prev + cw[1:2] * u + cw[2:3] * nxt)
    x = dx_ref[...].astype(BF16)
    a = jnp.dot(ct_ref[...], x, preferred_element_type=F32)
    b = jnp.dot(st_ref[...], x, preferred_element_type=F32)
    yield
    y = (jnp.dot(a.astype(BF16), cc_ref[...], preferred_element_type=F32)
         - jnp.dot(b.astype(BF16), sc_ref[...], preferred_element_type=F32))
    yield
    four_ref[...] = y * (1.0 / math.sqrt(seq * HEAD_DIM))


def _convfour_kernel(seq, *refs):
    _run_interleaved([_convfour_gen(seq, *refs)])


def _dft_tables(n, groups=1):
    idx = np.arange(n, dtype=np.int64)
    ang = ((idx[:, None] * idx[None, :]) % n).astype(np.float64) * (2.0 * math.pi / n)
    eye = np.eye(groups)
    return jnp.asarray(np.kron(eye, np.cos(ang)), F32), jnp.asarray(np.kron(eye, np.sin(ang)), F32)


def _convfour_io(proj, tok_block0, conv_w, batch, seq):
    ct, st = _dft_tables(seq)
    cc, sc = _dft_tables(HEAD_DIM, FOURIER_GROUPS)
    col = lambda c: pl.BlockSpec((seq, COLBLK), lambda b: (tok_block0 + b, c))
    const = lambda shape: pl.BlockSpec(shape, lambda b: (0, 0))
    out = pl.BlockSpec((seq, COLBLK), lambda b: (b, 0))
    in_specs = [col(2), col(3), col(4), col(9), const((8, CONV_CH)), const((seq, seq)), const((seq, seq)),
                const((FOURIER_WIDTH, FOURIER_WIDTH)), const((FOURIER_WIDTH, FOURIER_WIDTH))]
    args = [proj, proj, proj, proj, conv_w, ct.astype(BF16), st.astype(BF16), cc.astype(BF16), sc.astype(BF16)]
    return in_specs, args, [out, out], [jax.ShapeDtypeStruct((batch * seq, COLBLK), F32)] * 2


def _convfour(proj, tok_block0, conv_w, batch, seq):
    in_specs, args, out_specs, out_shape = _convfour_io(proj, tok_block0, conv_w, batch, seq)
    return pl.pallas_call(
        functools.partial(_convfour_kernel, seq),
        grid=(batch,),
        in_specs=in_specs,
        out_specs=out_specs,
        out_shape=out_shape,
        compiler_params=_params("parallel"),
        name="conv_fourier",
    )(*args)


def _rwkv_kernel(latent, seq, *refs, companions=None):
    if latent:
        (r_ref, k_ref, v_ref, lr_ref, lrw_ref, bias_ref, kk_ref, ka_ref, rk_ref, lnw_ref, lnb_ref, s0_ref,
         o_ref, logw_s, kd_s, b_s, z_s, g_s, y_s, st_s) = refs
    else:
        (r_ref, k_ref, v_ref, lr_ref, lrw_ref, bias_ref, kk_ref, ka_ref, rk_ref, lnw_ref, lnb_ref,
         o_ref, sfin_ref, logw_s, kd_s, b_s, z_s, g_s, y_s, st_s) = refs
    W = RWKV_WIDTH
    C = RWKV_CHUNK
    gsum = _group_sum_matrix(W)
    lane = _iota((1, W), 1)

    def prep(i, carry):
        rows = pl.ds(pl.multiple_of(i * RWKV_PREP_ROWS, RWKV_PREP_ROWS), RWKV_PREP_ROWS)
        lr = lr_ref[rows, :]
        f = jnp.where(lane < 2 * DECAY_RANK, jnp.tanh(lr),
                      jnp.where(lane < 2 * DECAY_RANK + 2 * ICLR_RANK, lr, _sigmoid(lr)))
        low = jnp.dot(f.astype(BF16), lrw_ref[...], preferred_element_type=F32) + bias_ref[...]
        logw_s[rows, :] = DECAY_SCALE * _sigmoid(low[:, :2 * W])
        a = _sigmoid(low[:, 2 * W:4 * W])
        g_s[rows, :] = low[:, 4 * W:]
        k = k_ref[rows, :]
        kk = k * kk_ref[...]
        kk = kk * jnp.minimum(lax.rsqrt(_mm_split(kk * kk, gsum)), 1e12)
        z_s[rows, :] = -kk
        k_scaled = k * ka_ref[...]
        k_rest = k - k_scaled
        for j in range(2):
            aj = a[:, j * W:(j + 1) * W]
            kd_s[rows, j * W:(j + 1) * W] = k_rest + k_scaled * aj
            b_s[rows, j * W:(j + 1) * W] = kk * aj
        return carry

    lax.fori_loop(0, seq // RWKV_PREP_ROWS, prep, 0)

    y_s[...] = jnp.zeros((seq, W), F32)
    st_s[...] = jnp.zeros((2, W, W), F32)
    if latent:
        for j in range(2):
            for h in range(RWKV_HEADS):
                blk = slice(h * HEAD_DIM, (h + 1) * HEAD_DIM)
                st_s[j, blk, blk] = s0_ref[j, h].T

    HC = RWKV_HEADS * C
    wide_t = _iota((C, HC), 0)
    wide_s = _iota((C, HC), 1) % C
    eye_wide = jnp.where(wide_t == wide_s, 1.0, 0.0).astype(F32)
    same_head = (_iota((HC, HC), 0) // C) == (_iota((HC, HC), 1) // C)
    bd = (_iota((W, W), 0) // HEAD_DIM) == (_iota((W, W), 1) // HEAD_DIM)
    eye_head = _iota((HEAD_DIM, W), 0) == _iota((HEAD_DIM, W), 1) % HEAD_DIM
    tc_i = _iota((C, C), 0)
    sc_i = _iota((C, C), 1)
    head_masks = [(lane >= h * HEAD_DIM) & (lane < (h + 1) * HEAD_DIM) for h in range(RWKV_HEADS)]
    n_chunks = seq // C

    def stack_masked(x):
        return jnp.concatenate([jnp.where(m, x, 0.0) for m in head_masks], axis=0).astype(BF16)

    def heads_transposed(x):
        xt = x.T
        return jnp.concatenate([xt[h * HEAD_DIM:(h + 1) * HEAD_DIM] for h in range(RWKV_HEADS)],
                               axis=1).astype(BF16)

    def block_diag(x, mask):
        return jnp.where(mask, jnp.concatenate([x] * RWKV_HEADS, axis=0), 0.0)

    def bdot(a, b):
        return jnp.dot(a, b, preferred_element_type=F32)

    def chunk(j, c):
        rows = pl.ds(pl.multiple_of(c * C, C), C)
        cols = slice(j * W, (j + 1) * W)
        rc, vc, zc = r_ref[rows, :], v_ref[rows, :], z_s[rows, :]
        lw, kd, bc = logw_s[rows, cols], kd_s[rows, cols], b_s[rows, cols]
        if j == 0:
            strict, incl, tri = wide_s < wide_t, wide_s <= wide_t, sc_i <= tc_i
        else:
            strict, incl, tri = wide_s > wide_t, wide_s >= wide_t, sc_i >= tc_i
        cl = _mm_split(jnp.where(tri, 1.0, 0.0).astype(BF16), lw, split_rhs=True)
        yield
        tot = cl[C - 1:C, :] if j == 0 else cl[0:1, :]
        w_in, w_ex, w_inv, w_rest = jnp.exp(cl), jnp.exp(cl - lw), jnp.exp(-cl), jnp.exp(tot - cl)
        rt = rc * w_in
        zs = stack_masked(zc * w_ex)
        vs = stack_masked(vc)
        aa = lax.dot_general(jnp.concatenate([zc * w_ex, rt], axis=0).astype(BF16),
                             jnp.concatenate([stack_masked(bc * w_inv), stack_masked(kd * w_inv)], axis=0),
                             (((1,), (1,)), ((), ())), preferred_element_type=F32)
        yield
        a_zb = jnp.where(strict, aa[:C, :HC], 0.0)
        a_zk = jnp.where(strict, aa[:C, HC:], 0.0).astype(BF16)
        a_rb = jnp.where(incl, aa[C:, :HC], 0.0).astype(BF16)
        a_rk = jnp.where(incl, aa[C:, HC:], 0.0).astype(BF16)
        tinv = eye_wide + a_zb
        apow = bdot(a_zb.astype(BF16), block_diag(a_zb, same_head).astype(BF16))
        av = bdot(a_zk, vs)
        rk = bdot(a_rk, vs)
        yield
        for _ in range(int(math.log2(C)) - 2):
            both = bdot(jnp.concatenate([apow, tinv], axis=0).astype(BF16), block_diag(apow, same_head).astype(BF16))
            yield
            apow = both[:C]
            tinv = tinv + both[C:]
        tinv = tinv + bdot(tinv.astype(BF16), block_diag(apow, same_head).astype(BF16))
        yield
        x = bdot(tinv.astype(BF16), jnp.concatenate([zs, stack_masked(av)], axis=1))
        yield
        xs = jnp.concatenate([stack_masked(x[:, :W]), stack_masked(x[:, W:])], axis=1)
        ry = bdot(a_rb, xs)
        bh_t = heads_transposed(bc * w_rest)
        mn = bdot(bh_t, xs)
        nk = bdot(heads_transposed(kd * w_rest), vs)
        yield
        rz = rt + ry[:, :W]
        y0 = ry[:, W:] + rk
        mt_wide = mn[:, :W] + jnp.where(eye_head, jnp.exp(tot), 0.0)
        return rows, rz.astype(BF16), y0, mt_wide.astype(BF16), mn[:, W:] + nk

    def advance(j, local):
        rows, rz, y0, mt_wide, nt_wide = local
        s_prev = st_s[j].astype(BF16)
        y_s[rows, :] += bdot(rz, s_prev) + y0
        st_s[j] = block_diag(bdot(mt_wide, s_prev) + nt_wide, bd)

    def step(i, carry):
        n = RWKV_CHUNKS_PER_STEP
        gens = ([chunk(0, n * i + u) for u in range(n)]
                + [chunk(1, n_chunks - 1 - (n * i + u)) for u in range(n)])
        extra, finish = companions() if companions is not None else ([], None)
        out = _run_interleaved(gens + extra)
        if finish is not None:
            finish(out[2 * n:])
        for u in range(n):
            advance(0, out[u])
            advance(1, out[n + u])
        return carry

    assert companions is None or n_chunks == RWKV_CHUNKS_PER_STEP

    lax.fori_loop(0, n_chunks // RWKV_CHUNKS_PER_STEP, step, 0)

    if not latent:
        for j in range(2):
            for h in range(RWKV_HEADS):
                blk = slice(h * HEAD_DIM, (h + 1) * HEAD_DIM)
                sfin_ref[j, h] = st_s[j, blk, blk].T

    def finish(i, carry):
        rows = pl.ds(pl.multiple_of(i * RWKV_PREP_ROWS, RWKV_PREP_ROWS), RWKV_PREP_ROWS)
        y = y_s[rows, :]
        mu = _mm_split(y, gsum) * (1.0 / HEAD_DIM)
        d = y - mu
        var = _mm_split(d * d, gsum) * (1.0 / HEAD_DIM)
        yn = d * lax.rsqrt(var + GN_EPS) * lnw_ref[...] + lnb_ref[...]
        r, k, v = r_ref[rows, :], k_ref[rows, :], v_ref[rows, :]
        bonus = _mm_split(r * k * rk_ref[...], gsum) * v
        o_ref[rows, :] = (yn + bonus) * g_s[rows, :]
        return carry

    lax.fori_loop(0, seq // RWKV_PREP_ROWS, finish, 0)


def _rwkv_io(proj, tok_block0, p, batch, seq, s0=None):
    W = RWKV_WIDTH
    col = lambda c: pl.BlockSpec((seq, COLBLK), lambda b: (tok_block0 + b, c))
    const = lambda shape: pl.BlockSpec(shape, lambda b: (0,) * len(shape))
    state = pl.BlockSpec((None, 2, RWKV_HEADS, HEAD_DIM, HEAD_DIM), lambda b: (b, 0, 0, 0, 0))
    in_specs = [col(5), col(6), col(7), col(8), const((COLBLK, 5 * W)), const((1, 5 * W))] + [const((1, W))] * 5
    args = [proj, proj, proj, proj, p["lrw"], p["lr_bias"], p["k_k"], p["k_a"], p["r_k"], p["ln_w"], p["ln_b"]]
    out_specs = [pl.BlockSpec((seq, W), lambda b: (b, 0))]
    out_shape = [jax.ShapeDtypeStruct((batch * seq, W), F32)]
    if s0 is not None:
        in_specs.append(state)
        args.append(s0)
    else:
        out_specs.append(state)
        out_shape.append(jax.ShapeDtypeStruct((batch, 2, RWKV_HEADS, HEAD_DIM, HEAD_DIM), F32))
    scratch = [
        pltpu.VMEM((seq, 2 * W), F32),
        pltpu.VMEM((seq, 2 * W), F32),
        pltpu.VMEM((seq, 2 * W), F32),
        pltpu.VMEM((seq, W), F32),
        pltpu.VMEM((seq, W), F32),
        pltpu.VMEM((seq, W), F32),
        pltpu.VMEM((2, W, W), F32),
    ]
    return in_specs, args, out_specs, out_shape, scratch


def _rwkv_latent(proj, tok_block0, p, batch, seq, s0):
    in_specs, args, out_specs, out_shape, scratch = _rwkv_io(proj, tok_block0, p, batch, seq, s0)
    return pl.pallas_call(
        functools.partial(_rwkv_kernel, True, seq),
        grid=(batch,),
        in_specs=in_specs,
        out_specs=out_specs,
        out_shape=out_shape,
        scratch_shapes=scratch,
        compiler_params=_params("parallel"),
        name="rwkv_latent",
    )(*args)[0]


def _ctx_mixers_kernel(seq, n_in, n_out, *refs):
    refs = list(refs)
    ins, outs = [], []
    for n in n_in:
        ins.append(refs[:n])
        refs = refs[n:]
    for n in n_out:
        outs.append(refs[:n])
        refs = refs[n:]

    def companions():
        heads, store_heads = _ctx_attn_parts(*ins[0], *outs[0])
        return heads + [_convfour_gen(seq, *ins[1], *outs[1])], lambda out: store_heads(out[:len(heads)])

    _rwkv_kernel(False, seq, *ins[2], *outs[2], *refs, companions=companions)


def _ctx_mixers(proj, q_norm, k_norm, sink, conv_w, p, batch, seq):
    ios = [_ctx_attention_io(proj, q_norm, k_norm, sink, batch, seq), _convfour_io(proj, 0, conv_w, batch, seq),
           _rwkv_io(proj, 0, p, batch, seq)]
    n_in = tuple(len(io[0]) for io in ios)
    n_out = tuple(len(io[2]) for io in ios)
    return pl.pallas_call(
        functools.partial(_ctx_mixers_kernel, seq, n_in, n_out),
        grid=(batch,),
        in_specs=[spec for io in ios for spec in io[0]],
        out_specs=[spec for io in ios for spec in io[2]],
        out_shape=[shape for io in ios for shape in io[3]],
        scratch_shapes=ios[2][4],
        compiler_params=_params("parallel"),
        name="ctx_mixers",
    )(*[arg for io in ios for arg in io[1]])


def _post_prologue(rows, x_ref, mix_refs, wout_ref, n2_ref, g1_ref, sh_ref, sc_ref):
    mix = None
    for i, m_ref in enumerate(mix_refs):
        part = jnp.dot(m_ref[rows, :].astype(BF16), wout_ref[i * COLBLK:(i + 1) * COLBLK, :],
                       preferred_element_type=F32)
        mix = part if mix is None else mix + part
    yield
    xn = x_ref[rows, :] + g1_ref[...] * mix
    return xn, _rms(xn, n2_ref[...]) * (1.0 + sc_ref[...]) + sh_ref[...]


def _swiglu(hb, w1_ref, w3_ref, w2_ref):
    t = _silu(jnp.dot(hb, w1_ref[...], preferred_element_type=F32)) * jnp.dot(hb, w3_ref[...],
                                                                             preferred_element_type=F32)
    return jnp.dot(t.astype(BF16), w2_ref[...], preferred_element_type=F32)


def _post_dense_kernel(x_ref, a_ref, c_ref, rw_ref, f_ref, wout_ref, n2_ref, g1_ref, sh_ref, sc_ref, g2_ref,
                       w1_ref, w3_ref, w2_ref, o_ref, h_s):
    f = pl.program_id(1)

    @pl.when(f == 0)
    def _():
        half = x_ref.shape[0] // 2

        def prologue(r):
            rows = slice(r * half, (r + 1) * half)
            xn, h = yield from _post_prologue(rows, x_ref, (a_ref, c_ref, rw_ref, f_ref), wout_ref, n2_ref, g1_ref,
                                              sh_ref, sc_ref)
            o_ref[rows, :] = xn
            h_s[rows, :] = h.astype(BF16)

        _run_interleaved([prologue(0), prologue(1)])

    o_ref[...] += g2_ref[...] * _swiglu(h_s[...], w1_ref, w3_ref, w2_ref)


def _post_moe_kernel(x_ref, a_ref, c_ref, rw_ref, f_ref, wout_ref, n2_ref, g1_ref, sh_ref, sc_ref, g2_ref,
                     router_ref, order_ref, w1_ref, w3_ref, w2_ref, o_ref, h_s, gate_s, rank_s, cnt_s):
    e = pl.program_id(1)
    lane = _iota((1, LANES), 1)
    sub = MOE_SUB_TILE
    n_sub = x_ref.shape[0] // sub

    @pl.when(e == 0)
    def _():
        def prologue(u):
            rows = slice(u * sub, (u + 1) * sub)
            xn, h = yield from _post_prologue(rows, x_ref, (a_ref, c_ref, rw_ref, f_ref), wout_ref, n2_ref, g1_ref,
                                              sh_ref, sc_ref)
            o_ref[rows, :] = xn
            h_s[rows, :] = h.astype(BF16)
            logits = _mm_split2(h, router_ref[...])
            yield
            logits = logits.T[:N_EXPERTS]
            ex = jnp.exp(logits - jnp.max(logits, axis=0, keepdims=True))
            probs = ex / jnp.sum(ex, axis=0, keepdims=True)
            expert = _iota(probs.shape, 0)
            p1 = jnp.max(probs, axis=0, keepdims=True)
            i1 = jnp.min(jnp.where(probs == p1, expert, N_EXPERTS), axis=0, keepdims=True)
            rest = jnp.where(expert == i1, -1.0, probs)
            p2 = jnp.max(rest, axis=0, keepdims=True)
            i2 = jnp.min(jnp.where(rest == p2, expert, N_EXPERTS), axis=0, keepdims=True)
            top = p1 + p2
            gates_t = jnp.where(expert == i1, p1 / top, jnp.where(expert == i2, p2 / top, 0.0))
            gate_s[rows, :] = jnp.concatenate([gates_t, jnp.zeros((LANES - N_EXPERTS, sub), F32)], axis=0).T
            sel_t = jnp.where((expert == i1) | (expert == i2), 1.0, 0.0)
            rank = jnp.dot(sel_t.astype(BF16), order_ref[...], preferred_element_type=F32)
            rank_s[u] = jnp.where(sel_t > 0.0, rank, -1.0)
            for ex_i in range(N_EXPERTS):
                cnt_s[u * N_EXPERTS + ex_i] = jnp.sum(sel_t[ex_i:ex_i + 1, :]).astype(jnp.int32)

        _run_interleaved([prologue(u) for u in range(n_sub)])

    routed = cnt_s[e]
    for u in range(1, n_sub):
        routed = jnp.maximum(routed, cnt_s[u * N_EXPERTS + e])

    def expert_rows(u, capacity):
        rows = slice(u * sub, (u + 1) * sub)
        if capacity is None:
            hb = h_s[rows, :]
        else:
            slot = _iota((capacity, sub), 0).astype(F32)
            pick = jnp.where(rank_s[u, pl.ds(e, 1), :] == slot, 1.0, 0.0)
            hb = jnp.dot(pick.astype(BF16), h_s[rows, :], preferred_element_type=F32).astype(BF16)
            yield
        t1 = jnp.dot(hb, w1_ref[...], preferred_element_type=F32)
        t3 = jnp.dot(hb, w3_ref[...], preferred_element_type=F32)
        yield
        y = jnp.dot((_silu(t1) * t3).astype(BF16), w2_ref[...], preferred_element_type=F32)
        yield
        if capacity is not None:
            y = _mm(pick.T, y)
            yield
        gate = jnp.sum(jnp.where(lane == e, gate_s[rows, :], 0.0), axis=-1, keepdims=True)
        o_ref[rows, :] += gate * (g2_ref[...] * y)

    def run(capacity):
        _run_interleaved([expert_rows(u, capacity) for u in range(n_sub)])

    below = 0
    for capacity in MOE_CAPACITIES:
        pl.when((routed > below) & (routed <= capacity))(functools.partial(run, capacity))
        below = capacity
    pl.when(routed > below)(functools.partial(run, None))


def _post(x, mixes, w_out, norm_w, mod, row_of_tile, ffn, moe):
    n_tok = x.shape[0]
    tok = MOE_TOKEN_TILE if moe else TOKEN_TILE
    tile = lambda width: pl.BlockSpec((tok, width), lambda i, f: (i, 0))
    const = lambda shape: pl.BlockSpec(shape, lambda i, f: (0, 0))
    mod_spec = lambda chunk: _mod_spec(chunk, functools.partial(row_of_tile, tok))
    in_specs = [tile(D_MODEL)] + [tile(COLBLK)] * 4 + [const((MIX_WIDTH, D_MODEL)), const((1, D_MODEL)),
                                                       mod_spec(2), mod_spec(3), mod_spec(4), mod_spec(5)]
    args = [x, *mixes, w_out, norm_w, mod, mod, mod, mod]
    scratch = [pltpu.VMEM((tok, D_MODEL), BF16)]
    if moe:
        router, w1, w3, w2 = ffn
        sub = MOE_SUB_TILE
        order = jnp.where(_iota((sub, sub), 0) < _iota((sub, sub), 1), 1.0, 0.0).astype(BF16)
        in_specs += [const((D_MODEL, LANES)), const((sub, sub)),
                     pl.BlockSpec((None, D_MODEL, D_FF_EXPERT), lambda i, e: (e, 0, 0)),
                     pl.BlockSpec((None, D_MODEL, D_FF_EXPERT), lambda i, e: (e, 0, 0)),
                     pl.BlockSpec((None, D_FF_EXPERT, D_MODEL), lambda i, e: (e, 0, 0))]
        args += [router, order, w1, w3, w2]
        scratch += [pltpu.VMEM((tok, LANES), F32), pltpu.VMEM((tok // sub, N_EXPERTS, sub), F32),
                    pltpu.SMEM((tok // sub * N_EXPERTS,), jnp.int32)]
        body, steps, name = _post_moe_kernel, N_EXPERTS, "post_moe"
    else:
        w1, w3, w2 = ffn
        in_specs += [pl.BlockSpec((D_MODEL, FF_TILE), lambda i, f: (0, f)),
                     pl.BlockSpec((D_MODEL, FF_TILE), lambda i, f: (0, f)),
                     pl.BlockSpec((FF_TILE, D_MODEL), lambda i, f: (f, 0))]
        args += [w1, w3, w2]
        body, steps, name = _post_dense_kernel, D_FF // FF_TILE, "post_dense"
    return pl.pallas_call(
        body,
        grid=(n_tok // tok, steps),
        in_specs=in_specs,
        out_specs=tile(D_MODEL),
        out_shape=jax.ShapeDtypeStruct((n_tok, D_MODEL), F32),
        scratch_shapes=scratch,
        compiler_params=_params("parallel", "arbitrary"),
        name=name,
    )(*args)


def _rope_tables(n_tokens, n_heads):
    n_rows = n_tokens // GRID_W
    row = np.repeat(np.arange(n_rows), GRID_W).astype(np.float32)
    col = np.tile(np.arange(GRID_W), n_rows).astype(np.float32)
    quarter = HEAD_DIM // 4
    inv_freq = np.float32(ROPE_BASE) ** (-np.arange(quarter, dtype=np.float32) / np.float32(quarter))
    ang_r, ang_c = row[:, None] * inv_freq, col[:, None] * inv_freq
    cos = np.concatenate([np.cos(ang_r)] * 2 + [np.cos(ang_c)] * 2, axis=1)
    sin = np.concatenate([-np.sin(ang_r), np.sin(ang_r), -np.sin(ang_c), np.sin(ang_c)], axis=1)
    return jnp.asarray(np.tile(cos, (1, n_heads)), F32), jnp.asarray(np.tile(sin, (1, n_heads)), F32)


def _layout_w_in(w):
    n_main = w.shape[1] - FOURIER_WIDTH
    return w[:, :n_main].astype(BF16), w[:, n_main:].astype(BF16)


def _layout_rwkv(l, w0, w_up, a0, a_up, g_up, k_k, k_a, r_k, ln_w, ln_b):
    W = RWKV_WIDTH
    lrw = jnp.zeros((COLBLK, 5 * W), F32)
    for j in range(2):
        lrw = lrw.at[j * DECAY_RANK:(j + 1) * DECAY_RANK, j * W:(j + 1) * W].set(w_up[l, j])
        r0 = 2 * DECAY_RANK + j * ICLR_RANK
        lrw = lrw.at[r0:r0 + ICLR_RANK, (2 + j) * W:(3 + j) * W].set(a_up[l, j])
    r0 = 2 * DECAY_RANK + 2 * ICLR_RANK
    lrw = lrw.at[r0:r0 + GATE_RANK, 4 * W:].set(g_up[l])
    bias = jnp.concatenate([w0[l, 0], w0[l, 1], a0[l, 0], a0[l, 1], jnp.zeros((W,), F32)])[None, :]
    row = lambda t: t[l].reshape(1, W)
    return {"lrw": lrw.astype(BF16), "lr_bias": bias, "k_k": row(k_k), "k_a": row(k_a), "r_k": row(r_k),
            "ln_w": row(ln_w), "ln_b": row(ln_b)}


def kernel(x_prompt, x_sample, cache_k, cache_v, state_wkv, c, c_ctx, mod_w, mod_b, norm1_w, norm2_w, w_in, q_norm_w, k_norm_w, attn_sink, conv_w, rwkv_w0, rwkv_w_up, rwkv_a0, rwkv_a_up, rwkv_g_up, rwkv_k_k, rwkv_k_a, rwkv_r_k, rwkv_ln_w, rwkv_ln_b, w_out, ffn_w1, ffn_w3, ffn_w2, router_w, moe_w1, moe_w3, moe_w2):
    batch, seq, _ = x_prompt.shape
    dec_batch, dec_seq, _ = x_sample.shape
    past = cache_k.shape[2]
    assert seq % BLOCK == 0 and dec_seq % MOE_TOKEN_TILE == 0 and (batch * seq) % MOE_TOKEN_TILE == 0
    assert 1 + dec_batch <= MOD_ROWS

    cond = jnp.concatenate([c_ctx[None, :], c, jnp.zeros((MOD_ROWS - 1 - dec_batch, D_MODEL), F32)], axis=0)
    mod_all = _modulation(cond, mod_w, mod_b)
    prompt_row = lambda tile, i: 0
    sample_row = lambda tile, i: 1 + (i * tile) // dec_seq

    rope_q = _rope_tables(dec_seq, N_Q_HEADS)
    rope_k = _rope_tables(dec_seq, N_KV_HEADS)

    xp = x_prompt.reshape(batch * seq, D_MODEL)
    xs = x_sample.reshape(dec_batch * dec_seq, D_MODEL)
    new_k, new_v, new_s = [], [], []
    for l in range(DEPTH):
        mod = mod_all[l].reshape(MOD_ROWS, 6, 1, D_MODEL)
        w_in_l = _layout_w_in(w_in[l])
        n1 = norm1_w[l][None, :]
        n2 = norm2_w[l][None, :]
        qn = jnp.tile(q_norm_w[l], N_Q_HEADS)[None, :]
        kn = jnp.tile(k_norm_w[l], N_KV_HEADS)[None, :]
        cw = jnp.concatenate([conv_w[l], jnp.zeros((5, CONV_CH), F32)], axis=0)
        rp = _layout_rwkv(l, rwkv_w0, rwkv_w_up, rwkv_a0, rwkv_a_up, rwkv_g_up, rwkv_k_k, rwkv_k_a, rwkv_r_k,
                          rwkv_ln_w, rwkv_ln_b)
        w_out_l = w_out[l].astype(BF16)
        if l % 2 == 0:
            j = l // 2
            ffn = (ffn_w1[j].astype(BF16), ffn_w3[j].astype(BF16), ffn_w2[j].astype(BF16))
        else:
            j = l // 2
            router = jnp.concatenate([router_w[j], jnp.zeros((D_MODEL, LANES - N_EXPERTS), F32)], axis=1)
            ffn = (router, moe_w1[j].astype(BF16), moe_w3[j].astype(BF16), moe_w2[j].astype(BF16))

        proj_p = _in_proj(xp, n1, mod, prompt_row, w_in_l)
        attn_p, k_p, v_p, conv_p, four_p, rw_p, sfin = _ctx_mixers(proj_p, qn, kn, attn_sink[l], cw, rp, batch, seq)
        xp = _post(xp, (attn_p, conv_p, rw_p, four_p), w_out_l, n2, mod, prompt_row, ffn, l % 2 == 1)
        new_k.append(k_p.reshape(batch, seq, N_KV_HEADS, HEAD_DIM))
        new_v.append(v_p.reshape(batch, seq, N_KV_HEADS, HEAD_DIM))
        new_s.append(sfin)

        proj_s = _in_proj(xs, n1, mod, sample_row, w_in_l)
        kc = cache_k[:, l].reshape(dec_batch, past, KV_WIDTH)
        vc = cache_v[:, l].reshape(dec_batch, past, KV_WIDTH)
        attn_s = _lat_attention(proj_s, 0, kc, vc, qn, kn, rope_q, rope_k, attn_sink[l], dec_batch, dec_seq, past)
        conv_s, four_s = _convfour(proj_s, 0, cw, dec_batch, dec_seq)
        rw_s = _rwkv_latent(proj_s, 0, rp, dec_batch, dec_seq, state_wkv[:, l])
        xs = _post(xs, (attn_s, conv_s, rw_s, four_s), w_out_l, n2, mod, sample_row, ffn, l % 2 == 1)

    return (xp.reshape(batch, seq, D_MODEL), xs.reshape(dec_batch, dec_seq, D_MODEL),
            jnp.stack(new_k, axis=1), jnp.stack(new_v, axis=1), jnp.stack(new_s, axis=1))
```

```python
import functools
import math

import jax
import jax.numpy as jnp
import numpy as np
from jax import lax
from jax.experimental import pallas as pl
from jax.experimental.pallas import tpu as pltpu

F32 = jnp.float32
BF16 = jnp.bfloat16

D_MODEL = 1024
DEPTH = 2
GRID_W = 64
HEAD_DIM = 64
N_Q_HEADS = 4
N_KV_HEADS = 2
WINDOW = 128
BLOCK = 128
ROPE_BASE = 10000.0
ATTN_WIDTH = N_Q_HEADS * HEAD_DIM
KV_WIDTH = N_KV_HEADS * HEAD_DIM
CONV_CH = 256
RWKV_HEADS = 4
RWKV_WIDTH = RWKV_HEADS * HEAD_DIM
DECAY_RANK = 32
ICLR_RANK = 32
GATE_RANK = 64
FOURIER_GROUPS = 4
FOURIER_WIDTH = FOURIER_GROUPS * HEAD_DIM
MIX_WIDTH = ATTN_WIDTH + CONV_CH + RWKV_WIDTH + FOURIER_WIDTH
D_FF = 2816
N_EXPERTS = 8
D_FF_EXPERT = 1024
NORM_EPS = 1e-6
GN_EPS = 64e-5
NEG_INF = -1e30

LANES = 128
VMEM_LIMIT_BYTES = 56 * 1024 * 1024

PROJ_WIDTH = 2560
COLBLK = 256
LOWRANK_WIDTH = 2 * DECAY_RANK + 2 * ICLR_RANK + GATE_RANK
MOD_ROWS = 16
TOKEN_TILE = 512
RWKV_CHUNK = 64
RWKV_CHUNKS_PER_STEP = 4
RWKV_PREP_ROWS = 256
FF_TILE = 1408
MOE_TOKEN_TILE = 1024
MOE_SUB_TILE = 512
MOE_CAPACITIES = (128, 144, 160, 176, 192, 208, 224, 256)
DECAY_SCALE = -math.exp(-0.5)


def _params(*sem):
    return pltpu.CompilerParams(dimension_semantics=sem, vmem_limit_bytes=VMEM_LIMIT_BYTES)


def _mm(a, b):
    return jnp.dot(a.astype(BF16), b.astype(BF16), preferred_element_type=F32)


def _mm_nt(a, b):
    return lax.dot_general(a.astype(BF16), b.astype(BF16), (((1,), (1,)), ((), ())), preferred_element_type=F32)


def _mm_split(a, b, split_rhs=False, parts=2):
    exact, x = (a, b) if split_rhs else (b, a)
    acc = None
    for _ in range(parts):
        piece = x.astype(BF16)
        x = x - piece.astype(F32)
        term = (jnp.dot(exact, piece, preferred_element_type=F32) if split_rhs
                else jnp.dot(piece, exact, preferred_element_type=F32))
        acc = term if acc is None else acc + term
    return acc


def _sigmoid(x):
    return 0.5 * jnp.tanh(0.5 * x) + 0.5


def _mm_split2(a, b):
    a_hi, b_hi = a.astype(BF16), b.astype(BF16)
    a_lo = (a - a_hi.astype(F32)).astype(BF16)
    b_lo = (b - b_hi.astype(F32)).astype(BF16)
    dot = functools.partial(jnp.dot, preferred_element_type=F32)
    return dot(a_hi, b_hi) + dot(a_lo, b_hi) + dot(a_hi, b_lo)


def _silu(x):
    return x * _sigmoid(x)


def _iota(shape, dim):
    return lax.broadcasted_iota(jnp.int32, shape, dim)


def _run_interleaved(gens):
    results = [None] * len(gens)
    pending = list(range(len(gens)))
    while pending:
        for idx in list(pending):
            try:
                next(gens[idx])
            except StopIteration as done:
                results[idx] = done.value
                pending.remove(idx)
    return results


def _mod_kernel(c_ref, w_ref, b_ref, o_ref):
    o_ref[0] = _mm(_silu(c_ref[...]), w_ref[0]) + b_ref[0]


def _modulation(cond, mod_w, mod_b):
    n_chunks = mod_w.shape[-1] // D_MODEL
    return pl.pallas_call(
        _mod_kernel,
        grid=(DEPTH, n_chunks),
        in_specs=[
            pl.BlockSpec((MOD_ROWS, D_MODEL), lambda l, j: (0, 0)),
            pl.BlockSpec((1, D_MODEL, D_MODEL), lambda l, j: (l, 0, j)),
            pl.BlockSpec((1, 1, D_MODEL), lambda l, j: (l, 0, j)),
        ],
        out_specs=pl.BlockSpec((1, MOD_ROWS, D_MODEL), lambda l, j: (l, 0, j)),
        out_shape=jax.ShapeDtypeStruct((DEPTH, MOD_ROWS, n_chunks * D_MODEL), F32),
        compiler_params=_params("parallel", "parallel"),
        name="modulation",
    )(cond, mod_w, mod_b.reshape(DEPTH, 1, -1))


def _mod_spec(chunk, row_of_tile):
    return pl.BlockSpec((None, None, 1, D_MODEL), lambda i, *_: (row_of_tile(i), chunk, 0, 0))


def _rms(x, w):
    return x * lax.rsqrt(jnp.mean(x * x, axis=-1, keepdims=True) + NORM_EPS) * w


def _in_proj_kernel(x_ref, nw_ref, sh_ref, sc_ref, w_main_ref, w_four_ref, o_ref):
    h = (_rms(x_ref[...], nw_ref[...]) * (1.0 + sc_ref[...]) + sh_ref[...]).astype(BF16)
    n_main = w_main_ref.shape[1]
    gap_end = PROJ_WIDTH - FOURIER_WIDTH
    o_ref[:, :n_main] = jnp.dot(h, w_main_ref[...], preferred_element_type=F32)
    o_ref[:, n_main:gap_end] = jnp.zeros((h.shape[0], gap_end - n_main), F32)
    o_ref[:, gap_end:] = jnp.dot(h, w_four_ref[...], preferred_element_type=F32)


def _in_proj(x, norm_w, mod, row_of_tile, w_in):
    row_of_tile = functools.partial(row_of_tile, TOKEN_TILE)
    n_tok = x.shape[0]
    return pl.pallas_call(
        _in_proj_kernel,
        grid=(n_tok // TOKEN_TILE,),
        in_specs=[
            pl.BlockSpec((TOKEN_TILE, D_MODEL), lambda i: (i, 0)),
            pl.BlockSpec((1, D_MODEL), lambda i: (0, 0)),
            _mod_spec(0, row_of_tile),
            _mod_spec(1, row_of_tile),
            pl.BlockSpec(w_in[0].shape, lambda i: (0, 0)),
            pl.BlockSpec(w_in[1].shape, lambda i: (0, 0)),
        ],
        out_specs=pl.BlockSpec((TOKEN_TILE, PROJ_WIDTH), lambda i: (i, 0)),
        out_shape=jax.ShapeDtypeStruct((n_tok, PROJ_WIDTH), F32),
        compiler_params=_params("parallel"),
        name="in_proj",
    )(x, norm_w, mod, mod, *w_in)


def _group_sum_matrix(width):
    r = _iota((width, width), 0) // HEAD_DIM
    c = _iota((width, width), 1) // HEAD_DIM
    return jnp.where(r == c, 1.0, 0.0).astype(BF16)


def _head_rms(x, w):
    ms = _mm_split(x * x, _group_sum_matrix(x.shape[-1])) * (1.0 / HEAD_DIM)
    return x * lax.rsqrt(ms + NORM_EPS) * w


def _dup_kv_head(x, g):
    lane = _iota((1, KV_WIDTH), 1)
    rolled = pltpu.roll(x, HEAD_DIM, 1)
    first = lane < HEAD_DIM
    return jnp.where(first, x, rolled) if g == 0 else jnp.where(first, rolled, x)


def _half_mask(j):
    lane = _iota((1, KV_WIDTH), 1)
    return (lane >= j * HEAD_DIM) & (lane < (j + 1) * HEAD_DIM)


def _ctx_attn_parts(qkv_ref, qw_ref, kw_ref, sink_ref, o_ref, k_ref, v_ref):
    qkv = qkv_ref[...]
    q = _head_rms(qkv[:, :ATTN_WIDTH], qw_ref[...]) * HEAD_DIM ** -0.5
    k = _head_rms(qkv[:, ATTN_WIDTH:ATTN_WIDTH + KV_WIDTH], kw_ref[...])
    v = qkv[:, ATTN_WIDTH + KV_WIDTH:]
    k_ref[...] = k
    v_ref[...] = v

    def head(g, j):
        mj = _half_mask(j)
        s = sink_ref[2 * g + j]
        logits = _mm_nt(jnp.where(mj, q[:, g * KV_WIDTH:(g + 1) * KV_WIDTH], 0.0), _dup_kv_head(k, g))
        yield
        m = jnp.maximum(jnp.max(logits, axis=-1, keepdims=True), s)
        e = jnp.exp(logits - m)
        den = jnp.sum(e, axis=-1, keepdims=True) + jnp.exp(s - m)
        pv = _mm(e, jnp.where(mj, _dup_kv_head(v, g), 0.0))
        yield
        return pv / den

    def finish(out):
        for g in range(N_KV_HEADS):
            o_ref[:, g * KV_WIDTH:(g + 1) * KV_WIDTH] = out[2 * g] + out[2 * g + 1]

    return [head(g, j) for g in range(N_KV_HEADS) for j in range(2)], finish


def _ctx_attention_io(proj, q_norm, k_norm, sink, batch, seq):
    blk = ATTN_WIDTH + 2 * KV_WIDTH
    in_specs = [
        pl.BlockSpec((seq, blk), lambda b: (b, 0)),
        pl.BlockSpec((1, ATTN_WIDTH), lambda b: (0, 0)),
        pl.BlockSpec((1, KV_WIDTH), lambda b: (0, 0)),
        pl.BlockSpec(memory_space=pltpu.SMEM),
    ]
    out_specs = [
        pl.BlockSpec((seq, ATTN_WIDTH), lambda b: (b, 0)),
        pl.BlockSpec((seq, KV_WIDTH), lambda b: (b, 0)),
        pl.BlockSpec((seq, KV_WIDTH), lambda b: (b, 0)),
    ]
    out_shape = [
        jax.ShapeDtypeStruct((batch * seq, ATTN_WIDTH), F32),
        jax.ShapeDtypeStruct((batch * seq, KV_WIDTH), F32),
        jax.ShapeDtypeStruct((batch * seq, KV_WIDTH), F32),
    ]
    return in_specs, [proj, q_norm, k_norm, sink], out_specs, out_shape


def _rope(x, cos, sin_signed):
    width = x.shape[-1]
    lane = _iota((1, width), 1)
    first_half = (lane % (HEAD_DIM // 2)) < (HEAD_DIM // 4)
    partner = jnp.where(first_half, pltpu.roll(x, width - HEAD_DIM // 4, 1), pltpu.roll(x, HEAD_DIM // 4, 1))
    return x * cos + partner * sin_signed


def _lat_attn_parts(seq, past, qkv_ref, kc_ref, vc_ref, qw_ref, kw_ref, cosq_ref, snq_ref, cosk_ref, snk_ref,
                    sink_ref, o_ref, q_s, k_s, v_s):
    qkv = qkv_ref[...]
    q = _rope(_head_rms(qkv[:, :ATTN_WIDTH], qw_ref[...]), cosq_ref[...], snq_ref[...]) * HEAD_DIM ** -0.5
    k = _rope(_head_rms(qkv[:, ATTN_WIDTH:ATTN_WIDTH + KV_WIDTH], kw_ref[...]), cosk_ref[...], snk_ref[...])
    v = qkv[:, ATTN_WIDTH + KV_WIDTH:]
    kc = kc_ref[...]
    vc = vc_ref[...]
    q_s[...] = q
    zero_blk = jnp.zeros((BLOCK, KV_WIDTH), BF16)
    for g in range(N_KV_HEADS):
        k_s[g, 0:BLOCK] = zero_blk
        k_s[g, BLOCK:BLOCK + seq] = _dup_kv_head(k, g).astype(BF16)
        k_s[g, BLOCK + seq:2 * BLOCK + seq] = zero_blk
        k_s[g, 2 * BLOCK + seq:] = _dup_kv_head(kc, g).astype(BF16)
        vg = _dup_kv_head(v, g)
        vcg = _dup_kv_head(vc, g)
        for j in range(2):
            mj = _half_mask(j)
            v_s[2 * g + j, 0:BLOCK] = zero_blk
            v_s[2 * g + j, BLOCK:BLOCK + seq] = jnp.where(mj, vg, 0.0).astype(BF16)
            v_s[2 * g + j, BLOCK + seq:2 * BLOCK + seq] = zero_blk
            v_s[2 * g + j, 2 * BLOCK + seq:] = jnp.where(mj, vcg, 0.0).astype(BF16)

    win = 3 * BLOCK
    ctx0 = 2 * BLOCK + seq

    def block(n):
        q0 = pl.multiple_of(n * BLOCK, BLOCK)
        qpos = n * BLOCK + _iota((BLOCK, win), 0)
        kpos = (n - 1) * BLOCK + _iota((BLOCK, win), 1)
        valid = (jnp.abs(qpos - kpos) <= WINDOW) & (kpos >= 0) & (kpos < seq)

        def head(g, j):
            h = 2 * g + j
            s = sink_ref[h]
            qm = jnp.where(_half_mask(j), q_s[pl.ds(q0, BLOCK), g * KV_WIDTH:(g + 1) * KV_WIDTH], 0.0)
            lw = jnp.where(valid, _mm_nt(qm, k_s[g, pl.ds(q0, win), :]), NEG_INF)
            lc = _mm_nt(qm, k_s[g, ctx0:ctx0 + past, :])
            yield
            m = jnp.maximum(jnp.maximum(jnp.max(lw, axis=-1, keepdims=True),
                                        jnp.max(lc, axis=-1, keepdims=True)), s)
            ew = jnp.exp(lw - m)
            ec = jnp.exp(lc - m)
            den = jnp.sum(ew, axis=-1, keepdims=True) + jnp.sum(ec, axis=-1, keepdims=True) + jnp.exp(s - m)
            pv = (jnp.dot(ew.astype(BF16), v_s[h, pl.ds(q0, win), :], preferred_element_type=F32)
                  + jnp.dot(ec.astype(BF16), v_s[h, ctx0:ctx0 + past, :], preferred_element_type=F32))
            yield
            return pv / den

        def finish(out):
            for g in range(N_KV_HEADS):
                o_ref[pl.ds(q0, BLOCK), g * KV_WIDTH:(g + 1) * KV_WIDTH] = out[2 * g] + out[2 * g + 1]

        return [head(g, j) for g in range(N_KV_HEADS) for j in range(2)], finish

    return block


def _lat_attention_io(proj, tok_block0, cache_k, cache_v, q_norm, k_norm, rope_q, rope_k, sink, batch, seq, past):
    blk = ATTN_WIDTH + 2 * KV_WIDTH
    cos_q, sin_q = rope_q
    cos_k, sin_k = rope_k
    rows = 2 * BLOCK + seq + past
    const = lambda shape: pl.BlockSpec(shape, lambda b: (0, 0))
    in_specs = [
        pl.BlockSpec((seq, blk), lambda b: (tok_block0 + b, 0)),
        pl.BlockSpec((None, past, KV_WIDTH), lambda b: (b, 0, 0)),
        pl.BlockSpec((None, past, KV_WIDTH), lambda b: (b, 0, 0)),
        const((1, ATTN_WIDTH)),
        const((1, KV_WIDTH)),
        const((seq, ATTN_WIDTH)),
        const((seq, ATTN_WIDTH)),
        const((seq, KV_WIDTH)),
        const((seq, KV_WIDTH)),
        pl.BlockSpec(memory_space=pltpu.SMEM),
    ]
    args = [proj, cache_k, cache_v, q_norm, k_norm, cos_q, sin_q, cos_k, sin_k, sink]
    out_specs = [pl.BlockSpec((seq, ATTN_WIDTH), lambda b: (b, 0))]
    out_shape = [jax.ShapeDtypeStruct((batch * seq, ATTN_WIDTH), F32)]
    scratch = [
        pltpu.VMEM((seq, ATTN_WIDTH), F32),
        pltpu.VMEM((N_KV_HEADS, rows, KV_WIDTH), BF16),
        pltpu.VMEM((N_Q_HEADS, rows, KV_WIDTH), BF16),
    ]
    return in_specs, args, out_specs, out_shape, scratch


def _convfour_gen(seq, bx_ref, bb_ref, bc_ref, dx_ref, cw_ref, ct_ref, st_ref, cc_ref, sc_ref, conv_ref, four_ref):
    u = bc_ref[...] * bx_ref[...]
    row = _iota((seq, 1), 0)
    prev = jnp.where(row == 0, 0.0, pltpu.roll(u, 1, 0))
    nxt = jnp.where(row == seq - 1, 0.0, pltpu.roll(u, seq - 1, 0))
    cw = cw_ref[...]
    conv_ref[...] = bb_ref[...] * (cw[0:1] * prev + cw[1:2] * u + cw[2:3] * nxt)
    x = dx_ref[...].astype(BF16)
    a = jnp.dot(ct_ref[...], x, preferred_element_type=F32)
    b = jnp.dot(st_ref[...], x, preferred_element_type=F32)
    yield
    y = (jnp.dot(a.astype(BF16), cc_ref[...], preferred_element_type=F32)
         - jnp.dot(b.astype(BF16), sc_ref[...], preferred_element_type=F32))
    yield
    four_ref[...] = y * (1.0 / math.sqrt(seq * HEAD_DIM))


def _convfour_kernel(seq, *refs):
    _run_interleaved([_convfour_gen(seq, *refs)])


def _dft_tables(n, groups=1):
    idx = np.arange(n, dtype=np.int64)
    ang = ((idx[:, None] * idx[None, :]) % n).astype(np.float64) * (2.0 * math.pi / n)
    eye = np.eye(groups)
    return jnp.asarray(np.kron(eye, np.cos(ang)), F32), jnp.asarray(np.kron(eye, np.sin(ang)), F32)


def _convfour_io(proj, tok_block0, conv_w, batch, seq):
    ct, st = _dft_tables(seq)
    cc, sc = _dft_tables(HEAD_DIM, FOURIER_GROUPS)
    col = lambda c: pl.BlockSpec((seq, COLBLK), lambda b: (tok_block0 + b, c))
    const = lambda shape: pl.BlockSpec(shape, lambda b: (0, 0))
    out = pl.BlockSpec((seq, COLBLK), lambda b: (b, 0))
    in_specs = [col(2), col(3), col(4), col(9), const((8, CONV_CH)), const((seq, seq)), const((seq, seq)),
                const((FOURIER_WIDTH, FOURIER_WIDTH)), const((FOURIER_WIDTH, FOURIER_WIDTH))]
    args = [proj, proj, proj, proj, conv_w, ct.astype(BF16), st.astype(BF16), cc.astype(BF16), sc.astype(BF16)]
    return in_specs, args, [out, out], [jax.ShapeDtypeStruct((batch * seq, COLBLK), F32)] * 2


def _convfour(proj, tok_block0, conv_w, batch, seq):
    in_specs, args, out_specs, out_shape = _convfour_io(proj, tok_block0, conv_w, batch, seq)
    return pl.pallas_call(
        functools.partial(_convfour_kernel, seq),
        grid=(batch,),
        in_specs=in_specs,
        out_specs=out_specs,
        out_shape=out_shape,
        compiler_params=_params("parallel"),
        name="conv_fourier",
    )(*args)


def _rwkv_kernel(latent, seq, *refs, companions=None):
    if latent:
        (r_ref, k_ref, v_ref, lr_ref, lrw_ref, bias_ref, kk_ref, ka_ref, rk_ref, lnw_ref, lnb_ref, s0_ref,
         o_ref, logw_s, kd_s, b_s, z_s, g_s, y_s, st_s) = refs
    else:
        (r_ref, k_ref, v_ref, lr_ref, lrw_ref, bias_ref, kk_ref, ka_ref, rk_ref, lnw_ref, lnb_ref,
         o_ref, sfin_ref, logw_s, kd_s, b_s, z_s, g_s, y_s, st_s) = refs
    W = RWKV_WIDTH
    C = RWKV_CHUNK
    gsum = _group_sum_matrix(W)
    lane = _iota((1, W), 1)

    def prep(i, carry):
        rows = pl.ds(pl.multiple_of(i * RWKV_PREP_ROWS, RWKV_PREP_ROWS), RWKV_PREP_ROWS)
        lr = lr_ref[rows, :]
        f = jnp.where(lane < 2 * DECAY_RANK, jnp.tanh(lr),
                      jnp.where(lane < 2 * DECAY_RANK + 2 * ICLR_RANK, lr, _sigmoid(lr)))
        low = jnp.dot(f.astype(BF16), lrw_ref[...], preferred_element_type=F32) + bias_ref[...]
        logw_s[rows, :] = DECAY_SCALE * _sigmoid(low[:, :2 * W])
        a = _sigmoid(low[:, 2 * W:4 * W])
        g_s[rows, :] = low[:, 4 * W:]
        k = k_ref[rows, :]
        kk = k * kk_ref[...]
        kk = kk * jnp.minimum(lax.rsqrt(_mm_split(kk * kk, gsum)), 1e12)
        z_s[rows, :] = -kk
        k_scaled = k * ka_ref[...]
        k_rest = k - k_scaled
        for j in range(2):
            aj = a[:, j * W:(j + 1) * W]
            kd_s[rows, j * W:(j + 1) * W] = k_rest + k_scaled * aj
            b_s[rows, j * W:(j + 1) * W] = kk * aj
        return carry

    lax.fori_loop(0, seq // RWKV_PREP_ROWS, prep, 0)

    y_s[...] = jnp.zeros((seq, W), F32)
    st_s[...] = jnp.zeros((2, W, W), F32)
    if latent:
        for j in range(2):
            for h in range(RWKV_HEADS):
                blk = slice(h * HEAD_DIM, (h + 1) * HEAD_DIM)
                st_s[j, blk, blk] = s0_ref[j, h].T

    HC = RWKV_HEADS * C
    wide_t = _iota((C, HC), 0)
    wide_s = _iota((C, HC), 1) % C
    eye_wide = jnp.where(wide_t == wide_s, 1.0, 0.0).astype(F32)
    same_head = (_iota((HC, HC), 0) // C) == (_iota((HC, HC), 1) // C)
    bd = (_iota((W, W), 0) // HEAD_DIM) == (_iota((W, W), 1) // HEAD_DIM)
    eye_head = _iota((HEAD_DIM, W), 0) == _iota((HEAD_DIM, W), 1) % HEAD_DIM
    tc_i = _iota((C, C), 0)
    sc_i = _iota((C, C), 1)
    head_masks = [(lane >= h * HEAD_DIM) & (lane < (h + 1) * HEAD_DIM) for h in range(RWKV_HEADS)]
    n_chunks = seq // C

    def stack_masked(x):
        return jnp.concatenate([jnp.where(m, x, 0.0) for m in head_masks], axis=0).astype(BF16)

    def heads_transposed(x):
        xt = x.T
        return jnp.concatenate([xt[h * HEAD_DIM:(h + 1) * HEAD_DIM] for h in range(RWKV_HEADS)],
                               axis=1).astype(BF16)

    def block_diag(x, mask):
        return jnp.where(mask, jnp.concatenate([x] * RWKV_HEADS, axis=0), 0.0)

    def bdot(a, b):
        return jnp.dot(a, b, preferred_element_type=F32)

    def chunk(j, c):
        rows = pl.ds(pl.multiple_of(c * C, C), C)
        cols = slice(j * W, (j + 1) * W)
        rc, vc, zc = r_ref[rows, :], v_ref[rows, :], z_s[rows, :]
        lw, kd, bc = logw_s[rows, cols], kd_s[rows, cols], b_s[rows, cols]
        if j == 0:
            strict, incl, tri = wide_s < wide_t, wide_s <= wide_t, sc_i <= tc_i
        else:
            strict, incl, tri = wide_s > wide_t, wide_s >= wide_t, sc_i >= tc_i
        cl = _mm_split(jnp.where(tri, 1.0, 0.0).astype(BF16), lw, split_rhs=True)
        yield
        tot = cl[C - 1:C, :] if j == 0 else cl[0:1, :]
        w_in, w_ex, w_inv, w_rest = jnp.exp(cl), jnp.exp(cl - lw), jnp.exp(-cl), jnp.exp(tot - cl)
        rt = rc * w_in
        zs = stack_masked(zc * w_ex)
        vs = stack_masked(vc)
        aa = lax.dot_general(jnp.concatenate([zc * w_ex, rt], axis=0).astype(BF16),
                             jnp.concatenate([stack_masked(bc * w_inv), stack_masked(kd * w_inv)], axis=0),
                             (((1,), (1,)), ((), ())), preferred_element_type=F32)
        yield
        a_zb = jnp.where(strict, aa[:C, :HC], 0.0)
        a_zk = jnp.where(strict, aa[:C, HC:], 0.0).astype(BF16)
        a_rb = jnp.where(incl, aa[C:, :HC], 0.0).astype(BF16)
        a_rk = jnp.where(incl, aa[C:, HC:], 0.0).astype(BF16)
        tinv = eye_wide + a_zb
        apow = bdot(a_zb.astype(BF16), block_diag(a_zb, same_head).astype(BF16))
        av = bdot(a_zk, vs)
        rk = bdot(a_rk, vs)
        yield
        for _ in range(int(math.log2(C)) - 2):
            both = bdot(jnp.concatenate([apow, tinv], axis=0).astype(BF16), block_diag(apow, same_head).astype(BF16))
            yield
            apow = both[:C]
            tinv = tinv + both[C:]
        tinv = tinv + bdot(tinv.astype(BF16), block_diag(apow, same_head).astype(BF16))
        yield
        x = bdot(tinv.astype(BF16), jnp.concatenate([zs, stack_masked(av)], axis=1))
        yield
        xs = jnp.concatenate([stack_masked(x[:, :W]), stack_masked(x[:, W:])], axis=1)
        ry = bdot(a_rb, xs)
        bh_t = heads_transposed(bc * w_rest)
        mn = bdot(bh_t, xs)
        nk = bdot(heads_transposed(kd * w_rest), vs)
        yield
        rz = rt + ry[:, :W]
        y0 = ry[:, W:] + rk
        mt_wide = mn[:, :W] + jnp.where(eye_head, jnp.exp(tot), 0.0)
        return rows, rz.astype(BF16), y0, mt_wide.astype(BF16), mn[:, W:] + nk

    def advance(j, local):
        rows, rz, y0, mt_wide, nt_wide = local
        s_prev = st_s[j].astype(BF16)
        y_s[rows, :] += bdot(rz, s_prev) + y0
        st_s[j] = block_diag(bdot(mt_wide, s_prev) + nt_wide, bd)

    def step(i, carry):
        n = RWKV_CHUNKS_PER_STEP
        gens = ([chunk(0, n * i + u) for u in range(n)]
                + [chunk(1, n_chunks - 1 - (n * i + u)) for u in range(n)])
        extra, finish = companions(i) if companions is not None else ([], None)
        out = _run_interleaved(gens + extra)
        if finish is not None:
            finish(out[2 * n:])
        for u in range(n):
            advance(0, out[u])
            advance(1, out[n + u])
        return carry


    lax.fori_loop(0, n_chunks // RWKV_CHUNKS_PER_STEP, step, 0)

    if not latent:
        for j in range(2):
            for h in range(RWKV_HEADS):
                blk = slice(h * HEAD_DIM, (h + 1) * HEAD_DIM)
                sfin_ref[j, h] = st_s[j, blk, blk].T

    def finish(i, carry):
        rows = pl.ds(pl.multiple_of(i * RWKV_PREP_ROWS, RWKV_PREP_ROWS), RWKV_PREP_ROWS)
        y = y_s[rows, :]
        mu = _mm_split(y, gsum) * (1.0 / HEAD_DIM)
        d = y - mu
        var = _mm_split(d * d, gsum) * (1.0 / HEAD_DIM)
        yn = d * lax.rsqrt(var + GN_EPS) * lnw_ref[...] + lnb_ref[...]
        r, k, v = r_ref[rows, :], k_ref[rows, :], v_ref[rows, :]
        bonus = _mm_split(r * k * rk_ref[...], gsum) * v
        o_ref[rows, :] = (yn + bonus) * g_s[rows, :]
        return carry

    lax.fori_loop(0, seq // RWKV_PREP_ROWS, finish, 0)


def _rwkv_io(proj, tok_block0, p, batch, seq, s0=None):
    W = RWKV_WIDTH
    col = lambda c: pl.BlockSpec((seq, COLBLK), lambda b: (tok_block0 + b, c))
    const = lambda shape: pl.BlockSpec(shape, lambda b: (0,) * len(shape))
    state = pl.BlockSpec((None, 2, RWKV_HEADS, HEAD_DIM, HEAD_DIM), lambda b: (b, 0, 0, 0, 0))
    in_specs = [col(5), col(6), col(7), col(8), const((COLBLK, 5 * W)), const((1, 5 * W))] + [const((1, W))] * 5
    args = [proj, proj, proj, proj, p["lrw"], p["lr_bias"], p["k_k"], p["k_a"], p["r_k"], p["ln_w"], p["ln_b"]]
    out_specs = [pl.BlockSpec((seq, W), lambda b: (b, 0))]
    out_shape = [jax.ShapeDtypeStruct((batch * seq, W), F32)]
    if s0 is not None:
        in_specs.append(state)
        args.append(s0)
    else:
        out_specs.append(state)
        out_shape.append(jax.ShapeDtypeStruct((batch, 2, RWKV_HEADS, HEAD_DIM, HEAD_DIM), F32))
    scratch = [
        pltpu.VMEM((seq, 2 * W), F32),
        pltpu.VMEM((seq, 2 * W), F32),
        pltpu.VMEM((seq, 2 * W), F32),
        pltpu.VMEM((seq, W), F32),
        pltpu.VMEM((seq, W), F32),
        pltpu.VMEM((seq, W), F32),
        pltpu.VMEM((2, W, W), F32),
    ]
    return in_specs, args, out_specs, out_shape, scratch


def _latent_mixers_kernel(seq, past, n_in, n_out, n_scratch, *refs):
    refs = list(refs)
    groups = []
    for counts in (n_in, n_out, n_scratch):
        group = []
        for n in counts:
            group.append(refs[:n])
            refs = refs[n:]
        groups.append(group)
    ins, outs, scratch = groups
    block = _lat_attn_parts(seq, past, *ins[0], *outs[0], *scratch[0])
    blocks_per_step = (seq // BLOCK) // (seq // RWKV_CHUNK // RWKV_CHUNKS_PER_STEP)

    def companions(step):
        parts = [block(blocks_per_step * step + u) for u in range(blocks_per_step)]
        heads = [g for gens, _ in parts for g in gens]

        def store(out):
            for u, (gens, finish) in enumerate(parts):
                finish(out[u * len(gens):(u + 1) * len(gens)])

        return heads, store

    _rwkv_kernel(True, seq, *ins[1], *outs[1], *scratch[1], companions=companions)


def _latent_mixers(proj, cache_k, cache_v, q_norm, k_norm, rope_q, rope_k, sink, p, s0, batch, seq, past):
    ios = [_lat_attention_io(proj, 0, cache_k, cache_v, q_norm, k_norm, rope_q, rope_k, sink, batch, seq, past),
           _rwkv_io(proj, 0, p, batch, seq, s0)]
    counts = lambda k: tuple(len(io[k]) for io in ios)
    return pl.pallas_call(
        functools.partial(_latent_mixers_kernel, seq, past, counts(0), counts(2), counts(4)),
        grid=(batch,),
        in_specs=[spec for io in ios for spec in io[0]],
        out_specs=[spec for io in ios for spec in io[2]],
        out_shape=[shape for io in ios for shape in io[3]],
        scratch_shapes=[buf for io in ios for buf in io[4]],
        compiler_params=_params("parallel"),
        name="latent_mixers",
    )(*[arg for io in ios for arg in io[1]])


def _ctx_mixers_kernel(seq, n_in, n_out, *refs):
    refs = list(refs)
    ins, outs = [], []
    for n in n_in:
        ins.append(refs[:n])
        refs = refs[n:]
    for n in n_out:
        outs.append(refs[:n])
        refs = refs[n:]

    assert seq // RWKV_CHUNK == RWKV_CHUNKS_PER_STEP

    def companions(step):
        heads, store_heads = _ctx_attn_parts(*ins[0], *outs[0])
        return heads + [_convfour_gen(seq, *ins[1], *outs[1])], lambda out: store_heads(out[:len(heads)])

    _rwkv_kernel(False, seq, *ins[2], *outs[2], *refs, companions=companions)


def _ctx_mixers(proj, q_norm, k_norm, sink, conv_w, p, batch, seq):
    ios = [_ctx_attention_io(proj, q_norm, k_norm, sink, batch, seq), _convfour_io(proj, 0, conv_w, batch, seq),
           _rwkv_io(proj, 0, p, batch, seq)]
    n_in = tuple(len(io[0]) for io in ios)
    n_out = tuple(len(io[2]) for io in ios)
    return pl.pallas_call(
        functools.partial(_ctx_mixers_kernel, seq, n_in, n_out),
        grid=(batch,),
        in_specs=[spec for io in ios for spec in io[0]],
        out_specs=[spec for io in ios for spec in io[2]],
        out_shape=[shape for io in ios for shape in io[3]],
        scratch_shapes=ios[2][4],
        compiler_params=_params("parallel"),
        name="ctx_mixers",
    )(*[arg for io in ios for arg in io[1]])


def _post_prologue(rows, x_ref, mix_refs, wout_ref, n2_ref, g1_ref, sh_ref, sc_ref):
    mix = None
    for i, m_ref in enumerate(mix_refs):
        part = jnp.dot(m_ref[rows, :].astype(BF16), wout_ref[i * COLBLK:(i + 1) * COLBLK, :],
                       preferred_element_type=F32)
        mix = part if mix is None else mix + part
    yield
    xn = x_ref[rows, :] + g1_ref[...] * mix
    return xn, _rms(xn, n2_ref[...]) * (1.0 + sc_ref[...]) + sh_ref[...]


def _swiglu(hb, w1_ref, w3_ref, w2_ref):
    t = _silu(jnp.dot(hb, w1_ref[...], preferred_element_type=F32)) * jnp.dot(hb, w3_ref[...],
                                                                             preferred_element_type=F32)
    return jnp.dot(t.astype(BF16), w2_ref[...], preferred_element_type=F32)


def _post_dense_kernel(x_ref, a_ref, c_ref, rw_ref, f_ref, wout_ref, n2_ref, g1_ref, sh_ref, sc_ref, g2_ref,
                       w1_ref, w3_ref, w2_ref, o_ref, h_s):
    f = pl.program_id(1)

    @pl.when(f == 0)
    def _():
        half = x_ref.shape[0] // 2

        def prologue(r):
            rows = slice(r * half, (r + 1) * half)
            xn, h = yield from _post_prologue(rows, x_ref, (a_ref, c_ref, rw_ref, f_ref), wout_ref, n2_ref, g1_ref,
                                              sh_ref, sc_ref)
            o_ref[rows, :] = xn
            h_s[rows, :] = h.astype(BF16)

        _run_interleaved([prologue(0), prologue(1)])

    o_ref[...] += g2_ref[...] * _swiglu(h_s[...], w1_ref, w3_ref, w2_ref)


def _post_moe_kernel(x_ref, a_ref, c_ref, rw_ref, f_ref, wout_ref, n2_ref, g1_ref, sh_ref, sc_ref, g2_ref,
                     router_ref, order_ref, w1_ref, w3_ref, w2_ref, o_ref, h_s, gate_s, rank_s, cnt_s):
    e = pl.program_id(1)
    lane = _iota((1, LANES), 1)
    sub = MOE_SUB_TILE
    n_sub = x_ref.shape[0] // sub

    @pl.when(e == 0)
    def _():
        def prologue(u):
            rows = slice(u * sub, (u + 1) * sub)
            xn, h = yield from _post_prologue(rows, x_ref, (a_ref, c_ref, rw_ref, f_ref), wout_ref, n2_ref, g1_ref,
                                              sh_ref, sc_ref)
            o_ref[rows, :] = xn
            h_s[rows, :] = h.astype(BF16)
            logits = _mm_split2(h, router_ref[...])
            yield
            logits = logits.T[:N_EXPERTS]
            ex = jnp.exp(logits - jnp.max(logits, axis=0, keepdims=True))
            probs = ex / jnp.sum(ex, axis=0, keepdims=True)
            expert = _iota(probs.shape, 0)
            p1 = jnp.max(probs, axis=0, keepdims=True)
            i1 = jnp.min(jnp.where(probs == p1, expert, N_EXPERTS), axis=0, keepdims=True)
            rest = jnp.where(expert == i1, -1.0, probs)
            p2 = jnp.max(rest, axis=0, keepdims=True)
            i2 = jnp.min(jnp.where(rest == p2, expert, N_EXPERTS), axis=0, keepdims=True)
            top = p1 + p2
            gates_t = jnp.where(expert == i1, p1 / top, jnp.where(expert == i2, p2 / top, 0.0))
            gate_s[rows, :] = jnp.concatenate([gates_t, jnp.zeros((LANES - N_EXPERTS, sub), F32)], axis=0).T
            sel_t = jnp.where((expert == i1) | (expert == i2), 1.0, 0.0)
            rank = jnp.dot(sel_t.astype(BF16), order_ref[...], preferred_element_type=F32)
            rank_s[u] = jnp.where(sel_t > 0.0, rank, -1.0)
            for ex_i in range(N_EXPERTS):
                cnt_s[u * N_EXPERTS + ex_i] = jnp.sum(sel_t[ex_i:ex_i + 1, :]).astype(jnp.int32)

        _run_interleaved([prologue(u) for u in range(n_sub)])

    routed = cnt_s[e]
    for u in range(1, n_sub):
        routed = jnp.maximum(routed, cnt_s[u * N_EXPERTS + e])

    def expert_rows(u, capacity):
        rows = slice(u * sub, (u + 1) * sub)
        if capacity is None:
            hb = h_s[rows, :]
        else:
            slot = _iota((capacity, sub), 0).astype(F32)
            pick = jnp.where(rank_s[u, pl.ds(e, 1), :] == slot, 1.0, 0.0)
            hb = jnp.dot(pick.astype(BF16), h_s[rows, :], preferred_element_type=F32).astype(BF16)
            yield
        t1 = jnp.dot(hb, w1_ref[...], preferred_element_type=F32)
        t3 = jnp.dot(hb, w3_ref[...], preferred_element_type=F32)
        yield
        y = jnp.dot((_silu(t1) * t3).astype(BF16), w2_ref[...], preferred_element_type=F32)
        yield
        if capacity is not None:
            y = _mm(pick.T, y)
            yield
        gate = jnp.sum(jnp.where(lane == e, gate_s[rows, :], 0.0), axis=-1, keepdims=True)
        o_ref[rows, :] += gate * (g2_ref[...] * y)

    def run(capacity):
        _run_interleaved([expert_rows(u, capacity) for u in range(n_sub)])

    below = 0
    for capacity in MOE_CAPACITIES:
        pl.when((routed > below) & (routed <= capacity))(functools.partial(run, capacity))
        below = capacity
    pl.when(routed > below)(functools.partial(run, None))


def _post(x, mixes, w_out, norm_w, mod, row_of_tile, ffn, moe):
    n_tok = x.shape[0]
    tok = MOE_TOKEN_TILE if moe else TOKEN_TILE
    tile = lambda width: pl.BlockSpec((tok, width), lambda i, f: (i, 0))
    const = lambda shape: pl.BlockSpec(shape, lambda i, f: (0, 0))
    mod_spec = lambda chunk: _mod_spec(chunk, functools.partial(row_of_tile, tok))
    in_specs = [tile(D_MODEL)] + [tile(COLBLK)] * 4 + [const((MIX_WIDTH, D_MODEL)), const((1, D_MODEL)),
                                                       mod_spec(2), mod_spec(3), mod_spec(4), mod_spec(5)]
    args = [x, *mixes, w_out, norm_w, mod, mod, mod, mod]
    scratch = [pltpu.VMEM((tok, D_MODEL), BF16)]
    if moe:
        router, w1, w3, w2 = ffn
        sub = MOE_SUB_TILE
        order = jnp.where(_iota((sub, sub), 0) < _iota((sub, sub), 1), 1.0, 0.0).astype(BF16)
        in_specs += [const((D_MODEL, LANES)), const((sub, sub)),
                     pl.BlockSpec((None, D_MODEL, D_FF_EXPERT), lambda i, e: (e, 0, 0)),
                     pl.BlockSpec((None, D_MODEL, D_FF_EXPERT), lambda i, e: (e, 0, 0)),
                     pl.BlockSpec((None, D_FF_EXPERT, D_MODEL), lambda i, e: (e, 0, 0))]
        args += [router, order, w1, w3, w2]
        scratch += [pltpu.VMEM((tok, LANES), F32), pltpu.VMEM((tok // sub, N_EXPERTS, sub), F32),
                    pltpu.SMEM((tok // sub * N_EXPERTS,), jnp.int32)]
        body, steps, name = _post_moe_kernel, N_EXPERTS, "post_moe"
    else:
        w1, w3, w2 = ffn
        in_specs += [pl.BlockSpec((D_MODEL, FF_TILE), lambda i, f: (0, f)),
                     pl.BlockSpec((D_MODEL, FF_TILE), lambda i, f: (0, f)),
                     pl.BlockSpec((FF_TILE, D_MODEL), lambda i, f: (f, 0))]
        args += [w1, w3, w2]
        body, steps, name = _post_dense_kernel, D_FF // FF_TILE, "post_dense"
    return pl.pallas_call(
        body,
        grid=(n_tok // tok, steps),
        in_specs=in_specs,
        out_specs=tile(D_MODEL),
        out_shape=jax.ShapeDtypeStruct((n_tok, D_MODEL), F32),
        scratch_shapes=scratch,
        compiler_params=_params("parallel", "arbitrary"),
        name=name,
    )(*args)


def _rope_tables(n_tokens, n_heads):
    n_rows = n_tokens // GRID_W
    row = np.repeat(np.arange(n_rows), GRID_W).astype(np.float32)
    col = np.tile(np.arange(GRID_W), n_rows).astype(np.float32)
    quarter = HEAD_DIM // 4
    inv_freq = np.float32(ROPE_BASE) ** (-np.arange(quarter, dtype=np.float32) / np.float32(quarter))
    ang_r, ang_c = row[:, None] * inv_freq, col[:, None] * inv_freq
    cos = np.concatenate([np.cos(ang_r)] * 2 + [np.cos(ang_c)] * 2, axis=1)
    sin = np.concatenate([-np.sin(ang_r), np.sin(ang_r), -np.sin(ang_c), np.sin(ang_c)], axis=1)
    return jnp.asarray(np.tile(cos, (1, n_heads)), F32), jnp.asarray(np.tile(sin, (1, n_heads)), F32)


def _layout_w_in(w):
    n_main = w.shape[1] - FOURIER_WIDTH
    return w[:, :n_main].astype(BF16), w[:, n_main:].astype(BF16)


def _layout_rwkv(l, w0, w_up, a0, a_up, g_up, k_k, k_a, r_k, ln_w, ln_b):
    W = RWKV_WIDTH
    lrw = jnp.zeros((COLBLK, 5 * W), F32)
    for j in range(2):
        lrw = lrw.at[j * DECAY_RANK:(j + 1) * DECAY_RANK, j * W:(j + 1) * W].set(w_up[l, j])
        r0 = 2 * DECAY_RANK + j * ICLR_RANK
        lrw = lrw.at[r0:r0 + ICLR_RANK, (2 + j) * W:(3 + j) * W].set(a_up[l, j])
    r0 = 2 * DECAY_RANK + 2 * ICLR_RANK
    lrw = lrw.at[r0:r0 + GATE_RANK, 4 * W:].set(g_up[l])
    bias = jnp.concatenate([w0[l, 0], w0[l, 1], a0[l, 0], a0[l, 1], jnp.zeros((W,), F32)])[None, :]
    row = lambda t: t[l].reshape(1, W)
    return {"lrw": lrw.astype(BF16), "lr_bias": bias, "k_k": row(k_k), "k_a": row(k_a), "r_k": row(r_k),
            "ln_w": row(ln_w), "ln_b": row(ln_b)}


def kernel(x_prompt, x_sample, cache_k, cache_v, state_wkv, c, c_ctx, mod_w, mod_b, norm1_w, norm2_w, w_in, q_norm_w, k_norm_w, attn_sink, conv_w, rwkv_w0, rwkv_w_up, rwkv_a0, rwkv_a_up, rwkv_g_up, rwkv_k_k, rwkv_k_a, rwkv_r_k, rwkv_ln_w, rwkv_ln_b, w_out, ffn_w1, ffn_w3, ffn_w2, router_w, moe_w1, moe_w3, moe_w2):
    batch, seq, _ = x_prompt.shape
    dec_batch, dec_seq, _ = x_sample.shape
    past = cache_k.shape[2]
    assert seq % BLOCK == 0 and dec_seq % MOE_TOKEN_TILE == 0 and (batch * seq) % MOE_TOKEN_TILE == 0
    assert 1 + dec_batch <= MOD_ROWS

    cond = jnp.concatenate([c_ctx[None, :], c, jnp.zeros((MOD_ROWS - 1 - dec_batch, D_MODEL), F32)], axis=0)
    mod_all = _modulation(cond, mod_w, mod_b)
    prompt_row = lambda tile, i: 0
    sample_row = lambda tile, i: 1 + (i * tile) // dec_seq

    rope_q = _rope_tables(dec_seq, N_Q_HEADS)
    rope_k = _rope_tables(dec_seq, N_KV_HEADS)

    xp = x_prompt.reshape(batch * seq, D_MODEL)
    xs = x_sample.reshape(dec_batch * dec_seq, D_MODEL)
    new_k, new_v, new_s = [], [], []
    for l in range(DEPTH):
        mod = mod_all[l].reshape(MOD_ROWS, 6, 1, D_MODEL)
        w_in_l = _layout_w_in(w_in[l])
        n1 = norm1_w[l][None, :]
        n2 = norm2_w[l][None, :]
        qn = jnp.tile(q_norm_w[l], N_Q_HEADS)[None, :]
        kn = jnp.tile(k_norm_w[l], N_KV_HEADS)[None, :]
        cw = jnp.concatenate([conv_w[l], jnp.zeros((5, CONV_CH), F32)], axis=0)
        rp = _layout_rwkv(l, rwkv_w0, rwkv_w_up, rwkv_a0, rwkv_a_up, rwkv_g_up, rwkv_k_k, rwkv_k_a, rwkv_r_k,
                          rwkv_ln_w, rwkv_ln_b)
        w_out_l = w_out[l].astype(BF16)
        if l % 2 == 0:
            j = l // 2
            ffn = (ffn_w1[j].astype(BF16), ffn_w3[j].astype(BF16), ffn_w2[j].astype(BF16))
        else:
            j = l // 2
            router = jnp.concatenate([router_w[j], jnp.zeros((D_MODEL, LANES - N_EXPERTS), F32)], axis=1)
            ffn = (router, moe_w1[j].astype(BF16), moe_w3[j].astype(BF16), moe_w2[j].astype(BF16))

        proj_p = _in_proj(xp, n1, mod, prompt_row, w_in_l)
        attn_p, k_p, v_p, conv_p, four_p, rw_p, sfin = _ctx_mixers(proj_p, qn, kn, attn_sink[l], cw, rp, batch, seq)
        xp = _post(xp, (attn_p, conv_p, rw_p, four_p), w_out_l, n2, mod, prompt_row, ffn, l % 2 == 1)
        new_k.append(k_p.reshape(batch, seq, N_KV_HEADS, HEAD_DIM))
        new_v.append(v_p.reshape(batch, seq, N_KV_HEADS, HEAD_DIM))
        new_s.append(sfin)

        proj_s = _in_proj(xs, n1, mod, sample_row, w_in_l)
        kc = cache_k[:, l].reshape(dec_batch, past, KV_WIDTH)
        vc = cache_v[:, l].reshape(dec_batch, past, KV_WIDTH)
        attn_s, rw_s = _latent_mixers(proj_s, kc, vc, qn, kn, rope_q, rope_k, attn_sink[l], rp, state_wkv[:, l],
                                      dec_batch, dec_seq, past)
        conv_s, four_s = _convfour(proj_s, 0, cw, dec_batch, dec_seq)
        xs = _post(xs, (attn_s, conv_s, rw_s, four_s), w_out_l, n2, mod, sample_row, ffn, l % 2 == 1)

    return (xp.reshape(batch, seq, D_MODEL), xs.reshape(dec_batch, dec_seq, D_MODEL),
            jnp.stack(new_k, axis=1), jnp.stack(new_v, axis=1), jnp.stack(new_s, axis=1))
```

```python
import functools
import math

import jax
import jax.numpy as jnp
import numpy as np
from jax import lax
from jax.experimental import pallas as pl
from jax.experimental.pallas import tpu as pltpu

F32 = jnp.float32
BF16 = jnp.bfloat16

D_MODEL = 1024
DEPTH = 2
GRID_W = 64
HEAD_DIM = 64
N_Q_HEADS = 4
N_KV_HEADS = 2
WINDOW = 128
BLOCK = 128
ROPE_BASE = 10000.0
ATTN_WIDTH = N_Q_HEADS * HEAD_DIM
KV_WIDTH = N_KV_HEADS * HEAD_DIM
CONV_CH = 256
RWKV_HEADS = 4
RWKV_WIDTH = RWKV_HEADS * HEAD_DIM
DECAY_RANK = 32
ICLR_RANK = 32
GATE_RANK = 64
FOURIER_GROUPS = 4
FOURIER_WIDTH = FOURIER_GROUPS * HEAD_DIM
MIX_WIDTH = ATTN_WIDTH + CONV_CH + RWKV_WIDTH + FOURIER_WIDTH
D_FF = 2816
N_EXPERTS = 8
D_FF_EXPERT = 1024
NORM_EPS = 1e-6
GN_EPS = 64e-5
NEG_INF = -1e30

LANES = 128
VMEM_LIMIT_BYTES = 56 * 1024 * 1024

PROJ_WIDTH = 2560
COLBLK = 256
MOD_ROWS = 16
TOKEN_TILE = 512
RWKV_CHUNK = 64
RWKV_CHUNKS_PER_STEP = 4
RWKV_PREP_ROWS = 256
FF_TILE = 1408
MOE_TOKEN_TILE = 1024
MOE_SUB_TILE = 512
MOE_CAPACITIES = (128, 160, 192, 224)
DECAY_SCALE = -math.exp(-0.5)


def _params(*sem):
    return pltpu.CompilerParams(dimension_semantics=sem, vmem_limit_bytes=VMEM_LIMIT_BYTES)


def _mm(a, b):
    return jnp.dot(a.astype(BF16), b.astype(BF16), preferred_element_type=F32)


def _mm_nt(a, b):
    return lax.dot_general(a.astype(BF16), b.astype(BF16), (((1,), (1,)), ((), ())), preferred_element_type=F32)


def _mm_split(a, b, split_rhs=False, parts=2):
    exact, x = (a, b) if split_rhs else (b, a)
    acc = None
    for _ in range(parts):
        piece = x.astype(BF16)
        x = x - piece.astype(F32)
        term = (jnp.dot(exact, piece, preferred_element_type=F32) if split_rhs
                else jnp.dot(piece, exact, preferred_element_type=F32))
        acc = term if acc is None else acc + term
    return acc


def _sigmoid(x):
    return 0.5 * jnp.tanh(0.5 * x) + 0.5


def _mm_split2(a, b):
    a_hi, b_hi = a.astype(BF16), b.astype(BF16)
    a_lo = (a - a_hi.astype(F32)).astype(BF16)
    b_lo = (b - b_hi.astype(F32)).astype(BF16)
    dot = functools.partial(jnp.dot, preferred_element_type=F32)
    return dot(a_hi, b_hi) + dot(a_lo, b_hi) + dot(a_hi, b_lo)


def _silu(x):
    return x * _sigmoid(x)


def _iota(shape, dim):
    return lax.broadcasted_iota(jnp.int32, shape, dim)


def _run_interleaved(gens):
    results = [None] * len(gens)
    pending = list(range(len(gens)))
    while pending:
        for idx in list(pending):
            try:
                next(gens[idx])
            except StopIteration as done:
                results[idx] = done.value
                pending.remove(idx)
    return results


def _mod_kernel(c_ref, w_ref, b_ref, o_ref):
    o_ref[0] = _mm(_silu(c_ref[...]), w_ref[0]) + b_ref[0]


def _modulation(cond, mod_w, mod_b):
    n_chunks = mod_w.shape[-1] // D_MODEL
    return pl.pallas_call(
        _mod_kernel,
        grid=(DEPTH, n_chunks),
        in_specs=[
            pl.BlockSpec((MOD_ROWS, D_MODEL), lambda l, j: (0, 0)),
            pl.BlockSpec((1, D_MODEL, D_MODEL), lambda l, j: (l, 0, j)),
            pl.BlockSpec((1, 1, D_MODEL), lambda l, j: (l, 0, j)),
        ],
        out_specs=pl.BlockSpec((1, MOD_ROWS, D_MODEL), lambda l, j: (l, 0, j)),
        out_shape=jax.ShapeDtypeStruct((DEPTH, MOD_ROWS, n_chunks * D_MODEL), F32),
        compiler_params=_params("parallel", "parallel"),
        name="modulation",
    )(cond, mod_w, mod_b.reshape(DEPTH, 1, -1))


def _mod_spec(chunk, row_of_tile):
    return pl.BlockSpec((None, None, 1, D_MODEL), lambda i, *_: (row_of_tile(i), chunk, 0, 0))


def _rms(x, w):
    return x * lax.rsqrt(jnp.mean(x * x, axis=-1, keepdims=True) + NORM_EPS) * w


def _in_proj_kernel(x_ref, nw_ref, sh_ref, sc_ref, w_main_ref, w_four_ref, o_ref):
    h = (_rms(x_ref[...], nw_ref[...]) * (1.0 + sc_ref[...]) + sh_ref[...]).astype(BF16)
    n_main = w_main_ref.shape[1]
    gap_end = PROJ_WIDTH - FOURIER_WIDTH
    o_ref[:, :n_main] = jnp.dot(h, w_main_ref[...], preferred_element_type=F32)
    o_ref[:, n_main:gap_end] = jnp.zeros((h.shape[0], gap_end - n_main), F32)
    o_ref[:, gap_end:] = jnp.dot(h, w_four_ref[...], preferred_element_type=F32)


def _in_proj(x, norm_w, mod, row_of_tile, w_in):
    row_of_tile = functools.partial(row_of_tile, TOKEN_TILE)
    n_tok = x.shape[0]
    return pl.pallas_call(
        _in_proj_kernel,
        grid=(n_tok // TOKEN_TILE,),
        in_specs=[
            pl.BlockSpec((TOKEN_TILE, D_MODEL), lambda i: (i, 0)),
            pl.BlockSpec((1, D_MODEL), lambda i: (0, 0)),
            _mod_spec(0, row_of_tile),
            _mod_spec(1, row_of_tile),
            pl.BlockSpec(w_in[0].shape, lambda i: (0, 0)),
            pl.BlockSpec(w_in[1].shape, lambda i: (0, 0)),
        ],
        out_specs=pl.BlockSpec((TOKEN_TILE, PROJ_WIDTH), lambda i: (i, 0)),
        out_shape=jax.ShapeDtypeStruct((n_tok, PROJ_WIDTH), F32),
        compiler_params=_params("parallel"),
        name="in_proj",
    )(x, norm_w, mod, mod, *w_in)


def _group_sum_matrix(width):
    r = _iota((width, width), 0) // HEAD_DIM
    c = _iota((width, width), 1) // HEAD_DIM
    return jnp.where(r == c, 1.0, 0.0).astype(BF16)


def _head_rms(x, w):
    ms = _mm_split(x * x, _group_sum_matrix(x.shape[-1])) * (1.0 / HEAD_DIM)
    return x * lax.rsqrt(ms + NORM_EPS) * w


def _dup_kv_head(x, g):
    lane = _iota((1, KV_WIDTH), 1)
    rolled = pltpu.roll(x, HEAD_DIM, 1)
    first = lane < HEAD_DIM
    return jnp.where(first, x, rolled) if g == 0 else jnp.where(first, rolled, x)


def _half_mask(j):
    lane = _iota((1, KV_WIDTH), 1)
    return (lane >= j * HEAD_DIM) & (lane < (j + 1) * HEAD_DIM)


def _ctx_attn_parts(qkv_ref, qw_ref, kw_ref, sink_ref, o_ref, k_ref, v_ref):
    qkv = qkv_ref[...]
    q = _head_rms(qkv[:, :ATTN_WIDTH], qw_ref[...]) * HEAD_DIM ** -0.5
    k = _head_rms(qkv[:, ATTN_WIDTH:ATTN_WIDTH + KV_WIDTH], kw_ref[...])
    v = qkv[:, ATTN_WIDTH + KV_WIDTH:]
    k_ref[...] = k
    v_ref[...] = v

    def head(g, j):
        mj = _half_mask(j)
        s = sink_ref[2 * g + j]
        logits = _mm_nt(jnp.where(mj, q[:, g * KV_WIDTH:(g + 1) * KV_WIDTH], 0.0), _dup_kv_head(k, g))
        yield
        m = jnp.maximum(jnp.max(logits, axis=-1, keepdims=True), s)
        e = jnp.exp(logits - m)
        den = jnp.sum(e, axis=-1, keepdims=True) + jnp.exp(s - m)
        pv = _mm(e, jnp.where(mj, _dup_kv_head(v, g), 0.0))
        yield
        return pv / den

    def finish(out):
        for g in range(N_KV_HEADS):
            o_ref[:, g * KV_WIDTH:(g + 1) * KV_WIDTH] = out[2 * g] + out[2 * g + 1]

    return [head(g, j) for g in range(N_KV_HEADS) for j in range(2)], finish


def _ctx_attention_io(proj, q_norm, k_norm, sink, batch, seq):
    blk = ATTN_WIDTH + 2 * KV_WIDTH
    in_specs = [
        pl.BlockSpec((seq, blk), lambda b: (b, 0)),
        pl.BlockSpec((1, ATTN_WIDTH), lambda b: (0, 0)),
        pl.BlockSpec((1, KV_WIDTH), lambda b: (0, 0)),
        pl.BlockSpec(memory_space=pltpu.SMEM),
    ]
    out_specs = [
        pl.BlockSpec((seq, ATTN_WIDTH), lambda b: (b, 0)),
        pl.BlockSpec((seq, KV_WIDTH), lambda b: (b, 0)),
        pl.BlockSpec((seq, KV_WIDTH), lambda b: (b, 0)),
    ]
    out_shape = [
        jax.ShapeDtypeStruct((batch * seq, ATTN_WIDTH), F32),
        jax.ShapeDtypeStruct((batch * seq, KV_WIDTH), F32),
        jax.ShapeDtypeStruct((batch * seq, KV_WIDTH), F32),
    ]
    return in_specs, [proj, q_norm, k_norm, sink], out_specs, out_shape


def _rope(x, cos, sin_signed):
    width = x.shape[-1]
    lane = _iota((1, width), 1)
    first_half = (lane % (HEAD_DIM // 2)) < (HEAD_DIM // 4)
    partner = jnp.where(first_half, pltpu.roll(x, width - HEAD_DIM // 4, 1), pltpu.roll(x, HEAD_DIM // 4, 1))
    return x * cos + partner * sin_signed


def _lat_attn_parts(seq, past, qkv_ref, kc_ref, vc_ref, qw_ref, kw_ref, cosq_ref, snq_ref, cosk_ref, snk_ref,
                    sink_ref, o_ref, q_s, k_s, v_s):
    qkv = qkv_ref[...]
    q = _rope(_head_rms(qkv[:, :ATTN_WIDTH], qw_ref[...]), cosq_ref[...], snq_ref[...]) * HEAD_DIM ** -0.5
    k = _rope(_head_rms(qkv[:, ATTN_WIDTH:ATTN_WIDTH + KV_WIDTH], kw_ref[...]), cosk_ref[...], snk_ref[...])
    v = qkv[:, ATTN_WIDTH + KV_WIDTH:]
    kc = kc_ref[...]
    vc = vc_ref[...]
    q_s[...] = q
    zero_blk = jnp.zeros((BLOCK, KV_WIDTH), BF16)
    for g in range(N_KV_HEADS):
        k_s[g, 0:BLOCK] = zero_blk
        k_s[g, BLOCK:BLOCK + seq] = _dup_kv_head(k, g).astype(BF16)
        k_s[g, BLOCK + seq:2 * BLOCK + seq] = zero_blk
        k_s[g, 2 * BLOCK + seq:] = _dup_kv_head(kc, g).astype(BF16)
        vg = _dup_kv_head(v, g)
        vcg = _dup_kv_head(vc, g)
        for j in range(2):
            mj = _half_mask(j)
            v_s[2 * g + j, 0:BLOCK] = zero_blk
            v_s[2 * g + j, BLOCK:BLOCK + seq] = jnp.where(mj, vg, 0.0).astype(BF16)
            v_s[2 * g + j, BLOCK + seq:2 * BLOCK + seq] = zero_blk
            v_s[2 * g + j, 2 * BLOCK + seq:] = jnp.where(mj, vcg, 0.0).astype(BF16)

    win = 3 * BLOCK
    ctx0 = 2 * BLOCK + seq

    def block(n):
        q0 = pl.multiple_of(n * BLOCK, BLOCK)
        qpos = n * BLOCK + _iota((BLOCK, win), 0)
        kpos = (n - 1) * BLOCK + _iota((BLOCK, win), 1)
        valid = (jnp.abs(qpos - kpos) <= WINDOW) & (kpos >= 0) & (kpos < seq)

        def head(g, j):
            h = 2 * g + j
            s = sink_ref[h]
            qm = jnp.where(_half_mask(j), q_s[pl.ds(q0, BLOCK), g * KV_WIDTH:(g + 1) * KV_WIDTH], 0.0)
            lw = jnp.where(valid, _mm_nt(qm, k_s[g, pl.ds(q0, win), :]), NEG_INF)
            lc = _mm_nt(qm, k_s[g, ctx0:ctx0 + past, :])
            yield
            m = jnp.maximum(jnp.maximum(jnp.max(lw, axis=-1, keepdims=True),
                                        jnp.max(lc, axis=-1, keepdims=True)), s)
            ew = jnp.exp(lw - m)
            ec = jnp.exp(lc - m)
            den = jnp.sum(ew, axis=-1, keepdims=True) + jnp.sum(ec, axis=-1, keepdims=True) + jnp.exp(s - m)
            pv = (jnp.dot(ew.astype(BF16), v_s[h, pl.ds(q0, win), :], preferred_element_type=F32)
                  + jnp.dot(ec.astype(BF16), v_s[h, ctx0:ctx0 + past, :], preferred_element_type=F32))
            yield
            return pv / den

        def finish(out):
            for g in range(N_KV_HEADS):
                o_ref[pl.ds(q0, BLOCK), g * KV_WIDTH:(g + 1) * KV_WIDTH] = out[2 * g] + out[2 * g + 1]

        return [head(g, j) for g in range(N_KV_HEADS) for j in range(2)], finish

    return block


def _lat_attention_io(proj, tok_block0, cache_k, cache_v, q_norm, k_norm, rope_q, rope_k, sink, batch, seq, past):
    blk = ATTN_WIDTH + 2 * KV_WIDTH
    cos_q, sin_q = rope_q
    cos_k, sin_k = rope_k
    rows = 2 * BLOCK + seq + past
    const = lambda shape: pl.BlockSpec(shape, lambda b: (0, 0))
    in_specs = [
        pl.BlockSpec((seq, blk), lambda b: (tok_block0 + b, 0)),
        pl.BlockSpec((None, past, KV_WIDTH), lambda b: (b, 0, 0)),
        pl.BlockSpec((None, past, KV_WIDTH), lambda b: (b, 0, 0)),
        const((1, ATTN_WIDTH)),
        const((1, KV_WIDTH)),
        const((seq, ATTN_WIDTH)),
        const((seq, ATTN_WIDTH)),
        const((seq, KV_WIDTH)),
        const((seq, KV_WIDTH)),
        pl.BlockSpec(memory_space=pltpu.SMEM),
    ]
    args = [proj, cache_k, cache_v, q_norm, k_norm, cos_q, sin_q, cos_k, sin_k, sink]
    out_specs = [pl.BlockSpec((seq, ATTN_WIDTH), lambda b: (b, 0))]
    out_shape = [jax.ShapeDtypeStruct((batch * seq, ATTN_WIDTH), F32)]
    scratch = [
        pltpu.VMEM((seq, ATTN_WIDTH), F32),
        pltpu.VMEM((N_KV_HEADS, rows, KV_WIDTH), BF16),
        pltpu.VMEM((N_Q_HEADS, rows, KV_WIDTH), BF16),
    ]
    return in_specs, args, out_specs, out_shape, scratch


def _convfour_gen(seq, bx_ref, bb_ref, bc_ref, dx_ref, cw_ref, ct_ref, st_ref, cc_ref, sc_ref, conv_ref, four_ref):
    u = bc_ref[...] * bx_ref[...]
    row = _iota((seq, 1), 0)
    prev = jnp.where(row == 0, 0.0, pltpu.roll(u, 1, 0))
    nxt = jnp.where(row == seq - 1, 0.0, pltpu.roll(u, seq - 1, 0))
    cw = cw_ref[...]
    conv_ref[...] = bb_ref[...] * (cw[0:1] * prev + cw[1:2] * u + cw[2:3] * nxt)
    x = dx_ref[...].astype(BF16)
    a = jnp.dot(ct_ref[...], x, preferred_element_type=F32)
    b = jnp.dot(st_ref[...], x, preferred_element_type=F32)
    yield
    y = (jnp.dot(a.astype(BF16), cc_ref[...], preferred_element_type=F32)
         - jnp.dot(b.astype(BF16), sc_ref[...], preferred_element_type=F32))
    yield
    four_ref[...] = y * (1.0 / math.sqrt(seq * HEAD_DIM))


def _convfour_kernel(seq, *refs):
    _run_interleaved([_convfour_gen(seq, *refs)])


def _dft_tables(n, groups=1):
    idx = np.arange(n, dtype=np.int64)
    ang = ((idx[:, None] * idx[None, :]) % n).astype(np.float64) * (2.0 * math.pi / n)
    eye = np.eye(groups)
    return jnp.asarray(np.kron(eye, np.cos(ang)), F32), jnp.asarray(np.kron(eye, np.sin(ang)), F32)


def _convfour_io(proj, tok_block0, conv_w, batch, seq):
    ct, st = _dft_tables(seq)
    cc, sc = _dft_tables(HEAD_DIM, FOURIER_GROUPS)
    col = lambda c: pl.BlockSpec((seq, COLBLK), lambda b: (tok_block0 + b, c))
    const = lambda shape: pl.BlockSpec(shape, lambda b: (0, 0))
    out = pl.BlockSpec((seq, COLBLK), lambda b: (b, 0))
    in_specs = [col(2), col(3), col(4), col(9), const((8, CONV_CH)), const((seq, seq)), const((seq, seq)),
                const((FOURIER_WIDTH, FOURIER_WIDTH)), const((FOURIER_WIDTH, FOURIER_WIDTH))]
    args = [proj, proj, proj, proj, conv_w, ct.astype(BF16), st.astype(BF16), cc.astype(BF16), sc.astype(BF16)]
    return in_specs, args, [out, out], [jax.ShapeDtypeStruct((batch * seq, COLBLK), F32)] * 2


def _convfour(proj, tok_block0, conv_w, batch, seq):
    in_specs, args, out_specs, out_shape = _convfour_io(proj, tok_block0, conv_w, batch, seq)
    return pl.pallas_call(
        functools.partial(_convfour_kernel, seq),
        grid=(batch,),
        in_specs=in_specs,
        out_specs=out_specs,
        out_shape=out_shape,
        compiler_params=_params("parallel"),
        name="conv_fourier",
    )(*args)


def _rwkv_kernel(latent, seq, *refs, companions=None):
    if latent:
        (r_ref, k_ref, v_ref, lr_ref, lrw_ref, bias_ref, kk_ref, ka_ref, rk_ref, lnw_ref, lnb_ref, s0_ref,
         o_ref, logw_s, kd_s, b_s, z_s, g_s, y_s, st_s) = refs
    else:
        (r_ref, k_ref, v_ref, lr_ref, lrw_ref, bias_ref, kk_ref, ka_ref, rk_ref, lnw_ref, lnb_ref,
         o_ref, sfin_ref, logw_s, kd_s, b_s, z_s, g_s, y_s, st_s) = refs
    W = RWKV_WIDTH
    C = RWKV_CHUNK
    gsum = _group_sum_matrix(W)
    lane = _iota((1, W), 1)

    def prep(i, carry):
        rows = pl.ds(pl.multiple_of(i * RWKV_PREP_ROWS, RWKV_PREP_ROWS), RWKV_PREP_ROWS)
        lr = lr_ref[rows, :]
        f = jnp.where(lane < 2 * DECAY_RANK, jnp.tanh(lr),
                      jnp.where(lane < 2 * DECAY_RANK + 2 * ICLR_RANK, lr, _sigmoid(lr)))
        low = jnp.dot(f.astype(BF16), lrw_ref[...], preferred_element_type=F32) + bias_ref[...]
        logw_s[rows, :] = DECAY_SCALE * _sigmoid(low[:, :2 * W])
        a = _sigmoid(low[:, 2 * W:4 * W])
        g_s[rows, :] = low[:, 4 * W:]
        k = k_ref[rows, :]
        kk = k * kk_ref[...]
        kk = kk * jnp.minimum(lax.rsqrt(_mm_split(kk * kk, gsum)), 1e12)
        z_s[rows, :] = -kk
        k_scaled = k * ka_ref[...]
        k_rest = k - k_scaled
        for j in range(2):
            aj = a[:, j * W:(j + 1) * W]
            kd_s[rows, j * W:(j + 1) * W] = k_rest + k_scaled * aj
            b_s[rows, j * W:(j + 1) * W] = kk * aj
        return carry

    lax.fori_loop(0, seq // RWKV_PREP_ROWS, prep, 0)

    y_s[...] = jnp.zeros((seq, W), F32)
    st_s[...] = jnp.zeros((2, W, W), F32)
    if latent:
        for j in range(2):
            for h in range(RWKV_HEADS):
                blk = slice(h * HEAD_DIM, (h + 1) * HEAD_DIM)
                st_s[j, blk, blk] = s0_ref[j, h].T

    HC = RWKV_HEADS * C
    wide_t = _iota((C, HC), 0)
    wide_s = _iota((C, HC), 1) % C
    eye_wide = jnp.where(wide_t == wide_s, 1.0, 0.0).astype(F32)
    same_head = (_iota((HC, HC), 0) // C) == (_iota((HC, HC), 1) // C)
    bd = (_iota((W, W), 0) // HEAD_DIM) == (_iota((W, W), 1) // HEAD_DIM)
    eye_head = _iota((HEAD_DIM, W), 0) == _iota((HEAD_DIM, W), 1) % HEAD_DIM
    tc_i = _iota((C, C), 0)
    sc_i = _iota((C, C), 1)
    head_masks = [(lane >= h * HEAD_DIM) & (lane < (h + 1) * HEAD_DIM) for h in range(RWKV_HEADS)]
    n_chunks = seq // C

    def stack_masked(x):
        return jnp.concatenate([jnp.where(m, x, 0.0) for m in head_masks], axis=0).astype(BF16)

    def heads_transposed(x):
        xt = x.T
        return jnp.concatenate([xt[h * HEAD_DIM:(h + 1) * HEAD_DIM] for h in range(RWKV_HEADS)],
                               axis=1).astype(BF16)

    def block_diag(x, mask):
        return jnp.where(mask, jnp.concatenate([x] * RWKV_HEADS, axis=0), 0.0)

    def bdot(a, b):
        return jnp.dot(a, b, preferred_element_type=F32)

    def chunk(j, c):
        rows = pl.ds(pl.multiple_of(c * C, C), C)
        cols = slice(j * W, (j + 1) * W)
        rc, vc, zc = r_ref[rows, :], v_ref[rows, :], z_s[rows, :]
        lw, kd, bc = logw_s[rows, cols], kd_s[rows, cols], b_s[rows, cols]
        if j == 0:
            strict, incl, tri = wide_s < wide_t, wide_s <= wide_t, sc_i <= tc_i
        else:
            strict, incl, tri = wide_s > wide_t, wide_s >= wide_t, sc_i >= tc_i
        cl = _mm_split(jnp.where(tri, 1.0, 0.0).astype(BF16), lw, split_rhs=True)
        yield
        tot = cl[C - 1:C, :] if j == 0 else cl[0:1, :]
        w_in, w_ex, w_inv, w_rest = jnp.exp(cl), jnp.exp(cl - lw), jnp.exp(-cl), jnp.exp(tot - cl)
        rt = rc * w_in
        zs = stack_masked(zc * w_ex)
        vs = stack_masked(vc)
        aa = lax.dot_general(jnp.concatenate([zc * w_ex, rt], axis=0).astype(BF16),
                             jnp.concatenate([stack_masked(bc * w_inv), stack_masked(kd * w_inv)], axis=0),
                             (((1,), (1,)), ((), ())), preferred_element_type=F32)
        yield
        a_zb = jnp.where(strict, aa[:C, :HC], 0.0)
        a_zk = jnp.where(strict, aa[:C, HC:], 0.0).astype(BF16)
        a_rb = jnp.where(incl, aa[C:, :HC], 0.0).astype(BF16)
        a_rk = jnp.where(incl, aa[C:, HC:], 0.0).astype(BF16)
        tinv = eye_wide + a_zb
        apow = bdot(a_zb.astype(BF16), block_diag(a_zb, same_head).astype(BF16))
        av = bdot(a_zk, vs)
        rk = bdot(a_rk, vs)
        yield
        for _ in range(int(math.log2(C)) - 2):
            both = bdot(jnp.concatenate([apow, tinv], axis=0).astype(BF16), block_diag(apow, same_head).astype(BF16))
            yield
            apow = both[:C]
            tinv = tinv + both[C:]
        tinv = tinv + bdot(tinv.astype(BF16), block_diag(apow, same_head).astype(BF16))
        yield
        x = bdot(tinv.astype(BF16), jnp.concatenate([zs, stack_masked(av)], axis=1))
        yield
        xs = jnp.concatenate([stack_masked(x[:, :W]), stack_masked(x[:, W:])], axis=1)
        ry = bdot(a_rb, xs)
        bh_t = heads_transposed(bc * w_rest)
        mn = bdot(bh_t, xs)
        nk = bdot(heads_transposed(kd * w_rest), vs)
        yield
        rz = rt + ry[:, :W]
        y0 = ry[:, W:] + rk
        mt_wide = mn[:, :W] + jnp.where(eye_head, jnp.exp(tot), 0.0)
        return rows, rz.astype(BF16), y0, mt_wide.astype(BF16), mn[:, W:] + nk

    def advance(j, local):
        rows, rz, y0, mt_wide, nt_wide = local
        s_prev = st_s[j].astype(BF16)
        y_s[rows, :] += bdot(rz, s_prev) + y0
        st_s[j] = block_diag(bdot(mt_wide, s_prev) + nt_wide, bd)

    def step(i, carry):
        n = RWKV_CHUNKS_PER_STEP
        gens = ([chunk(0, n * i + u) for u in range(n)]
                + [chunk(1, n_chunks - 1 - (n * i + u)) for u in range(n)])
        extra, finish = companions(i) if companions is not None else ([], None)
        out = _run_interleaved(gens + extra)
        if finish is not None:
            finish(out[2 * n:])
        for u in range(n):
            advance(0, out[u])
            advance(1, out[n + u])
        return carry


    lax.fori_loop(0, n_chunks // RWKV_CHUNKS_PER_STEP, step, 0)

    if not latent:
        for j in range(2):
            for h in range(RWKV_HEADS):
                blk = slice(h * HEAD_DIM, (h + 1) * HEAD_DIM)
                sfin_ref[j, h] = st_s[j, blk, blk].T

    def finish(i, carry):
        rows = pl.ds(pl.multiple_of(i * RWKV_PREP_ROWS, RWKV_PREP_ROWS), RWKV_PREP_ROWS)
        y = y_s[rows, :]
        mu = _mm_split(y, gsum) * (1.0 / HEAD_DIM)
        d = y - mu
        var = _mm_split(d * d, gsum) * (1.0 / HEAD_DIM)
        yn = d * lax.rsqrt(var + GN_EPS) * lnw_ref[...] + lnb_ref[...]
        r, k, v = r_ref[rows, :], k_ref[rows, :], v_ref[rows, :]
        bonus = _mm_split(r * k * rk_ref[...], gsum) * v
        o_ref[rows, :] = (yn + bonus) * g_s[rows, :]
        return carry

    lax.fori_loop(0, seq // RWKV_PREP_ROWS, finish, 0)


def _rwkv_io(proj, tok_block0, p, batch, seq, s0=None):
    W = RWKV_WIDTH
    col = lambda c: pl.BlockSpec((seq, COLBLK), lambda b: (tok_block0 + b, c))
    const = lambda shape: pl.BlockSpec(shape, lambda b: (0,) * len(shape))
    state = pl.BlockSpec((None, 2, RWKV_HEADS, HEAD_DIM, HEAD_DIM), lambda b: (b, 0, 0, 0, 0))
    in_specs = [col(5), col(6), col(7), col(8), const((COLBLK, 5 * W)), const((1, 5 * W))] + [const((1, W))] * 5
    args = [proj, proj, proj, proj, p["lrw"], p["lr_bias"], p["k_k"], p["k_a"], p["r_k"], p["ln_w"], p["ln_b"]]
    out_specs = [pl.BlockSpec((seq, W), lambda b: (b, 0))]
    out_shape = [jax.ShapeDtypeStruct((batch * seq, W), F32)]
    if s0 is not None:
        in_specs.append(state)
        args.append(s0)
    else:
        out_specs.append(state)
        out_shape.append(jax.ShapeDtypeStruct((batch, 2, RWKV_HEADS, HEAD_DIM, HEAD_DIM), F32))
    scratch = [
        pltpu.VMEM((seq, 2 * W), F32),
        pltpu.VMEM((seq, 2 * W), F32),
        pltpu.VMEM((seq, 2 * W), F32),
        pltpu.VMEM((seq, W), F32),
        pltpu.VMEM((seq, W), F32),
        pltpu.VMEM((seq, W), F32),
        pltpu.VMEM((2, W, W), F32),
    ]
    return in_specs, args, out_specs, out_shape, scratch


def _latent_mixers_kernel(seq, past, n_in, n_out, n_scratch, *refs):
    refs = list(refs)
    groups = []
    for counts in (n_in, n_out, n_scratch):
        group = []
        for n in counts:
            group.append(refs[:n])
            refs = refs[n:]
        groups.append(group)
    ins, outs, scratch = groups
    block = _lat_attn_parts(seq, past, *ins[0], *outs[0], *scratch[0])
    blocks_per_step = (seq // BLOCK) // (seq // RWKV_CHUNK // RWKV_CHUNKS_PER_STEP)

    def companions(step):
        parts = [block(blocks_per_step * step + u) for u in range(blocks_per_step)]
        heads = [g for gens, _ in parts for g in gens]

        def store(out):
            for u, (gens, finish) in enumerate(parts):
                finish(out[u * len(gens):(u + 1) * len(gens)])

        return heads, store

    _rwkv_kernel(True, seq, *ins[1], *outs[1], *scratch[1], companions=companions)


def _latent_mixers(proj, cache_k, cache_v, q_norm, k_norm, rope_q, rope_k, sink, p, s0, batch, seq, past):
    ios = [_lat_attention_io(proj, 0, cache_k, cache_v, q_norm, k_norm, rope_q, rope_k, sink, batch, seq, past),
           _rwkv_io(proj, 0, p, batch, seq, s0)]
    counts = lambda k: tuple(len(io[k]) for io in ios)
    return pl.pallas_call(
        functools.partial(_latent_mixers_kernel, seq, past, counts(0), counts(2), counts(4)),
        grid=(batch,),
        in_specs=[spec for io in ios for spec in io[0]],
        out_specs=[spec for io in ios for spec in io[2]],
        out_shape=[shape for io in ios for shape in io[3]],
        scratch_shapes=[buf for io in ios for buf in io[4]],
        compiler_params=_params("parallel"),
        name="latent_mixers",
    )(*[arg for io in ios for arg in io[1]])


def _ctx_mixers_kernel(seq, n_in, n_out, *refs):
    refs = list(refs)
    ins, outs = [], []
    for n in n_in:
        ins.append(refs[:n])
        refs = refs[n:]
    for n in n_out:
        outs.append(refs[:n])
        refs = refs[n:]

    assert seq // RWKV_CHUNK == RWKV_CHUNKS_PER_STEP

    def companions(step):
        heads, store_heads = _ctx_attn_parts(*ins[0], *outs[0])
        return heads + [_convfour_gen(seq, *ins[1], *outs[1])], lambda out: store_heads(out[:len(heads)])

    _rwkv_kernel(False, seq, *ins[2], *outs[2], *refs, companions=companions)


def _ctx_mixers(proj, q_norm, k_norm, sink, conv_w, p, batch, seq):
    ios = [_ctx_attention_io(proj, q_norm, k_norm, sink, batch, seq), _convfour_io(proj, 0, conv_w, batch, seq),
           _rwkv_io(proj, 0, p, batch, seq)]
    n_in = tuple(len(io[0]) for io in ios)
    n_out = tuple(len(io[2]) for io in ios)
    return pl.pallas_call(
        functools.partial(_ctx_mixers_kernel, seq, n_in, n_out),
        grid=(batch,),
        in_specs=[spec for io in ios for spec in io[0]],
        out_specs=[spec for io in ios for spec in io[2]],
        out_shape=[shape for io in ios for shape in io[3]],
        scratch_shapes=ios[2][4],
        compiler_params=_params("parallel"),
        name="ctx_mixers",
    )(*[arg for io in ios for arg in io[1]])


def _post_prologue(rows, x_ref, mix_refs, wout_ref, n2_ref, g1_ref, sh_ref, sc_ref):
    mix = None
    for i, m_ref in enumerate(mix_refs):
        part = jnp.dot(m_ref[rows, :].astype(BF16), wout_ref[i * COLBLK:(i + 1) * COLBLK, :],
                       preferred_element_type=F32)
        mix = part if mix is None else mix + part
    yield
    xn = x_ref[rows, :] + g1_ref[...] * mix
    return xn, _rms(xn, n2_ref[...]) * (1.0 + sc_ref[...]) + sh_ref[...]


def _swiglu(hb, w1_ref, w3_ref, w2_ref):
    t = _silu(jnp.dot(hb, w1_ref[...], preferred_element_type=F32)) * jnp.dot(hb, w3_ref[...],
                                                                             preferred_element_type=F32)
    return jnp.dot(t.astype(BF16), w2_ref[...], preferred_element_type=F32)


def _post_dense_kernel(x_ref, a_ref, c_ref, rw_ref, f_ref, wout_ref, n2_ref, g1_ref, sh_ref, sc_ref, g2_ref,
                       w1_ref, w3_ref, w2_ref, o_ref, h_s):
    f = pl.program_id(1)

    @pl.when(f == 0)
    def _():
        half = x_ref.shape[0] // 2

        def prologue(r):
            rows = slice(r * half, (r + 1) * half)
            xn, h = yield from _post_prologue(rows, x_ref, (a_ref, c_ref, rw_ref, f_ref), wout_ref, n2_ref, g1_ref,
                                              sh_ref, sc_ref)
            o_ref[rows, :] = xn
            h_s[rows, :] = h.astype(BF16)

        _run_interleaved([prologue(0), prologue(1)])

    o_ref[...] += g2_ref[...] * _swiglu(h_s[...], w1_ref, w3_ref, w2_ref)


def _post_moe_kernel(x_ref, a_ref, c_ref, rw_ref, f_ref, wout_ref, n2_ref, g1_ref, sh_ref, sc_ref, g2_ref,
                     router_ref, order_ref, w1_ref, w3_ref, w2_ref, o_ref, h_s, gate_s, rank_s, cnt_s):
    e = pl.program_id(1)
    lane = _iota((1, LANES), 1)
    sub = MOE_SUB_TILE
    n_sub = x_ref.shape[0] // sub

    @pl.when(e == 0)
    def _():
        def prologue(u):
            rows = slice(u * sub, (u + 1) * sub)
            xn, h = yield from _post_prologue(rows, x_ref, (a_ref, c_ref, rw_ref, f_ref), wout_ref, n2_ref, g1_ref,
                                              sh_ref, sc_ref)
            o_ref[rows, :] = xn
            h_s[rows, :] = h.astype(BF16)
            logits = _mm_split2(h, router_ref[...])
            yield
            logits = logits.T[:N_EXPERTS]
            ex = jnp.exp(logits - jnp.max(logits, axis=0, keepdims=True))
            probs = ex / jnp.sum(ex, axis=0, keepdims=True)
            expert = _iota(probs.shape, 0)
            p1 = jnp.max(probs, axis=0, keepdims=True)
            i1 = jnp.min(jnp.where(probs == p1, expert, N_EXPERTS), axis=0, keepdims=True)
            rest = jnp.where(expert == i1, -1.0, probs)
            p2 = jnp.max(rest, axis=0, keepdims=True)
            i2 = jnp.min(jnp.where(rest == p2, expert, N_EXPERTS), axis=0, keepdims=True)
            top = p1 + p2
            gates_t = jnp.where(expert == i1, p1 / top, jnp.where(expert == i2, p2 / top, 0.0))
            gate_s[rows, :] = jnp.concatenate([gates_t, jnp.zeros((LANES - N_EXPERTS, sub), F32)], axis=0).T
            sel_t = jnp.where((expert == i1) | (expert == i2), 1.0, 0.0)
            rank = jnp.dot(sel_t.astype(BF16), order_ref[...], preferred_element_type=F32)
            rank_s[u] = jnp.where(sel_t > 0.0, rank, -1.0)
            for ex_i in range(N_EXPERTS):
                cnt_s[u * N_EXPERTS + ex_i] = jnp.sum(sel_t[ex_i:ex_i + 1, :]).astype(jnp.int32)

        _run_interleaved([prologue(u) for u in range(n_sub)])

    routed = cnt_s[e]
    for u in range(1, n_sub):
        routed = jnp.maximum(routed, cnt_s[u * N_EXPERTS + e])

    def expert_rows(u, capacity):
        rows = slice(u * sub, (u + 1) * sub)
        if capacity is None:
            hb = h_s[rows, :]
        else:
            slot = _iota((capacity, sub), 0).astype(F32)
            pick = jnp.where(rank_s[u, pl.ds(e, 1), :] == slot, 1.0, 0.0)
            hb = jnp.dot(pick.astype(BF16), h_s[rows, :], preferred_element_type=F32).astype(BF16)
            yield
        t1 = jnp.dot(hb, w1_ref[...], preferred_element_type=F32)
        t3 = jnp.dot(hb, w3_ref[...], preferred_element_type=F32)
        yield
        y = jnp.dot((_silu(t1) * t3).astype(BF16), w2_ref[...], preferred_element_type=F32)
        yield
        if capacity is not None:
            y = _mm(pick.T, y)
            yield
        gate = jnp.sum(jnp.where(lane == e, gate_s[rows, :], 0.0), axis=-1, keepdims=True)
        o_ref[rows, :] += gate * (g2_ref[...] * y)

    def run(capacity):
        _run_interleaved([expert_rows(u, capacity) for u in range(n_sub)])

    below = 0
    for capacity in MOE_CAPACITIES:
        pl.when((routed > below) & (routed <= capacity))(functools.partial(run, capacity))
        below = capacity
    pl.when(routed > below)(functools.partial(run, None))


def _post(x, mixes, w_out, norm_w, mod, row_of_tile, ffn, moe):
    n_tok = x.shape[0]
    tok = MOE_TOKEN_TILE if moe else TOKEN_TILE
    tile = lambda width: pl.BlockSpec((tok, width), lambda i, f: (i, 0))
    const = lambda shape: pl.BlockSpec(shape, lambda i, f: (0, 0))
    mod_spec = lambda chunk: _mod_spec(chunk, functools.partial(row_of_tile, tok))
    in_specs = [tile(D_MODEL)] + [tile(COLBLK)] * 4 + [const((MIX_WIDTH, D_MODEL)), const((1, D_MODEL)),
                                                       mod_spec(2), mod_spec(3), mod_spec(4), mod_spec(5)]
    args = [x, *mixes, w_out, norm_w, mod, mod, mod, mod]
    scratch = [pltpu.VMEM((tok, D_MODEL), BF16)]
    if moe:
        router, w1, w3, w2 = ffn
        sub = MOE_SUB_TILE
        order = jnp.where(_iota((sub, sub), 0) < _iota((sub, sub), 1), 1.0, 0.0).astype(BF16)
        in_specs += [const((D_MODEL, LANES)), const((sub, sub)),
                     pl.BlockSpec((None, D_MODEL, D_FF_EXPERT), lambda i, e: (e, 0, 0)),
                     pl.BlockSpec((None, D_MODEL, D_FF_EXPERT), lambda i, e: (e, 0, 0)),
                     pl.BlockSpec((None, D_FF_EXPERT, D_MODEL), lambda i, e: (e, 0, 0))]
        args += [router, order, w1, w3, w2]
        scratch += [pltpu.VMEM((tok, LANES), F32), pltpu.VMEM((tok // sub, N_EXPERTS, sub), F32),
                    pltpu.SMEM((tok // sub * N_EXPERTS,), jnp.int32)]
        body, steps, name = _post_moe_kernel, N_EXPERTS, "post_moe"
    else:
        w1, w3, w2 = ffn
        in_specs += [pl.BlockSpec((D_MODEL, FF_TILE), lambda i, f: (0, f)),
                     pl.BlockSpec((D_MODEL, FF_TILE), lambda i, f: (0, f)),
                     pl.BlockSpec((FF_TILE, D_MODEL), lambda i, f: (f, 0))]
        args += [w1, w3, w2]
        body, steps, name = _post_dense_kernel, D_FF // FF_TILE, "post_dense"
    return pl.pallas_call(
        body,
        grid=(n_tok // tok, steps),
        in_specs=in_specs,
        out_specs=tile(D_MODEL),
        out_shape=jax.ShapeDtypeStruct((n_tok, D_MODEL), F32),
        scratch_shapes=scratch,
        compiler_params=_params("parallel", "arbitrary"),
        name=name,
    )(*args)


def _rope_tables(n_tokens, n_heads):
    n_rows = n_tokens // GRID_W
    row = np.repeat(np.arange(n_rows), GRID_W).astype(np.float32)
    col = np.tile(np.arange(GRID_W), n_rows).astype(np.float32)
    quarter = HEAD_DIM // 4
    inv_freq = np.float32(ROPE_BASE) ** (-np.arange(quarter, dtype=np.float32) / np.float32(quarter))
    ang_r, ang_c = row[:, None] * inv_freq, col[:, None] * inv_freq
    cos = np.concatenate([np.cos(ang_r)] * 2 + [np.cos(ang_c)] * 2, axis=1)
    sin = np.concatenate([-np.sin(ang_r), np.sin(ang_r), -np.sin(ang_c), np.sin(ang_c)], axis=1)
    return jnp.asarray(np.tile(cos, (1, n_heads)), F32), jnp.asarray(np.tile(sin, (1, n_heads)), F32)


def _layout_w_in(w):
    n_main = w.shape[1] - FOURIER_WIDTH
    return w[:, :n_main].astype(BF16), w[:, n_main:].astype(BF16)


def _layout_rwkv(l, w0, w_up, a0, a_up, g_up, k_k, k_a, r_k, ln_w, ln_b):
    W = RWKV_WIDTH
    lrw = jnp.zeros((COLBLK, 5 * W), F32)
    for j in range(2):
        lrw = lrw.at[j * DECAY_RANK:(j + 1) * DECAY_RANK, j * W:(j + 1) * W].set(w_up[l, j])
        r0 = 2 * DECAY_RANK + j * ICLR_RANK
        lrw = lrw.at[r0:r0 + ICLR_RANK, (2 + j) * W:(3 + j) * W].set(a_up[l, j])
    r0 = 2 * DECAY_RANK + 2 * ICLR_RANK
    lrw = lrw.at[r0:r0 + GATE_RANK, 4 * W:].set(g_up[l])
    bias = jnp.concatenate([w0[l, 0], w0[l, 1], a0[l, 0], a0[l, 1], jnp.zeros((W,), F32)])[None, :]
    row = lambda t: t[l].reshape(1, W)
    return {"lrw": lrw.astype(BF16), "lr_bias": bias, "k_k": row(k_k), "k_a": row(k_a), "r_k": row(r_k),
            "ln_w": row(ln_w), "ln_b": row(ln_b)}


def kernel(x_prompt, x_sample, cache_k, cache_v, state_wkv, c, c_ctx, mod_w, mod_b, norm1_w, norm2_w, w_in, q_norm_w, k_norm_w, attn_sink, conv_w, rwkv_w0, rwkv_w_up, rwkv_a0, rwkv_a_up, rwkv_g_up, rwkv_k_k, rwkv_k_a, rwkv_r_k, rwkv_ln_w, rwkv_ln_b, w_out, ffn_w1, ffn_w3, ffn_w2, router_w, moe_w1, moe_w3, moe_w2):
    batch, seq, _ = x_prompt.shape
    dec_batch, dec_seq, _ = x_sample.shape
    past = cache_k.shape[2]
    assert seq % BLOCK == 0 and dec_seq % MOE_TOKEN_TILE == 0 and (batch * seq) % MOE_TOKEN_TILE == 0
    assert 1 + dec_batch <= MOD_ROWS

    cond = jnp.concatenate([c_ctx[None, :], c, jnp.zeros((MOD_ROWS - 1 - dec_batch, D_MODEL), F32)], axis=0)
    mod_all = _modulation(cond, mod_w, mod_b)
    prompt_row = lambda tile, i: 0
    sample_row = lambda tile, i: 1 + (i * tile) // dec_seq

    rope_q = _rope_tables(dec_seq, N_Q_HEADS)
    rope_k = _rope_tables(dec_seq, N_KV_HEADS)

    xp = x_prompt.reshape(batch * seq, D_MODEL)
    xs = x_sample.reshape(dec_batch * dec_seq, D_MODEL)
    new_k, new_v, new_s = [], [], []
    for l in range(DEPTH):
        mod = mod_all[l].reshape(MOD_ROWS, 6, 1, D_MODEL)
        w_in_l = _layout_w_in(w_in[l])
        n1 = norm1_w[l][None, :]
        n2 = norm2_w[l][None, :]
        qn = jnp.tile(q_norm_w[l], N_Q_HEADS)[None, :]
        kn = jnp.tile(k_norm_w[l], N_KV_HEADS)[None, :]
        cw = jnp.concatenate([conv_w[l], jnp.zeros((5, CONV_CH), F32)], axis=0)
        rp = _layout_rwkv(l, rwkv_w0, rwkv_w_up, rwkv_a0, rwkv_a_up, rwkv_g_up, rwkv_k_k, rwkv_k_a, rwkv_r_k,
                          rwkv_ln_w, rwkv_ln_b)
        w_out_l = w_out[l].astype(BF16)
        if l % 2 == 0:
            j = l // 2
            ffn = (ffn_w1[j].astype(BF16), ffn_w3[j].astype(BF16), ffn_w2[j].astype(BF16))
        else:
            j = l // 2
            router = jnp.concatenate([router_w[j], jnp.zeros((D_MODEL, LANES - N_EXPERTS), F32)], axis=1)
            ffn = (router, moe_w1[j].astype(BF16), moe_w3[j].astype(BF16), moe_w2[j].astype(BF16))

        proj_p = _in_proj(xp, n1, mod, prompt_row, w_in_l)
        attn_p, k_p, v_p, conv_p, four_p, rw_p, sfin = _ctx_mixers(proj_p, qn, kn, attn_sink[l], cw, rp, batch, seq)
        xp = _post(xp, (attn_p, conv_p, rw_p, four_p), w_out_l, n2, mod, prompt_row, ffn, l % 2 == 1)
        new_k.append(k_p.reshape(batch, seq, N_KV_HEADS, HEAD_DIM))
        new_v.append(v_p.reshape(batch, seq, N_KV_HEADS, HEAD_DIM))
        new_s.append(sfin)

        proj_s = _in_proj(xs, n1, mod, sample_row, w_in_l)
        kc = cache_k[:, l].reshape(dec_batch, past, KV_WIDTH)
        vc = cache_v[:, l].reshape(dec_batch, past, KV_WIDTH)
        attn_s, rw_s = _latent_mixers(proj_s, kc, vc, qn, kn, rope_q, rope_k, attn_sink[l], rp, state_wkv[:, l],
                                      dec_batch, dec_seq, past)
        conv_s, four_s = _convfour(proj_s, 0, cw, dec_batch, dec_seq)
        xs = _post(xs, (attn_s, conv_s, rw_s, four_s), w_out_l, n2, mod, sample_row, ffn, l % 2 == 1)

    return (xp.reshape(batch, seq, D_MODEL), xs.reshape(dec_batch, dec_seq, D_MODEL),
            jnp.stack(new_k, axis=1), jnp.stack(new_v, axis=1), jnp.stack(new_s, axis=1))
```
